```python
import functools
import jax, jax.numpy as jnp
from jax import lax
import numpy as np

D_MODEL = 1024
BATCH = 4
SEQ = 4096
DEPTH = 2

HEAD_DIM = 64
N_MOBA_HEADS = 8
N_FOX_HEADS = 8
N_ATTN_HEADS = N_MOBA_HEADS + N_FOX_HEADS
ATTN_WIDTH = N_ATTN_HEADS * HEAD_DIM
ROPE_DIMS = HEAD_DIM // 4
ROPE_THETA = 500000.0
MOBA_BLOCK = 256
MOBA_TOPK = 3
MOBA_QCHUNK = 64
FOX_QBLOCK = 128
FOX_BIAS_MEAN = 2.0
D_RNN = 1280
N_RNN_BLOCKS = 10
RNN_BLOCK_W = D_RNN // N_RNN_BLOCKS
CONV_WIDTH = 4
RG_C = 8.0
D_FF = 2816
N_EXPERTS = 8
TOP_K = 2
D_EXPERT = 3584
MOE_BLOCK = 256
N_EVEN_LAYERS = (DEPTH + 1) // 2
N_ODD_LAYERS = DEPTH // 2
NORM_EPS = 1e-6

kernel_name = "hybrid_moba_fox_rglru_moe"


def rms_norm(x, g):
    xf = x.astype(jnp.float32)
    y = xf * lax.rsqrt(jnp.mean(xf * xf, axis=-1, keepdims=True) + NORM_EPS)
    return (y * g.astype(jnp.float32)).astype(x.dtype)


def apply_partial_rope(x, pos):
    half = ROPE_DIMS // 2
    inv_freq = jnp.power(ROPE_THETA, -jnp.arange(half, dtype=jnp.float32) / half)
    ang = pos.astype(jnp.float32)[:, None] * inv_freq[None, :]
    cos, sin = jnp.cos(ang), jnp.sin(ang)
    x1 = x[..., :half].astype(jnp.float32)
    x2 = x[..., half:ROPE_DIMS].astype(jnp.float32)
    rot = jnp.concatenate([x1 * cos - x2 * sin, x2 * cos + x1 * sin], axis=-1).astype(x.dtype)
    return jnp.concatenate([rot, x[..., ROPE_DIMS:]], axis=-1)


def moba_attention(q, k, v):
    b, h, s, dh = q.shape
    n_blk = -(-s // MOBA_BLOCK)
    s_pad = n_blk * MOBA_BLOCK
    pad = ((0, 0), (0, 0), (0, s_pad - s), (0, 0))
    q, k, v = (jnp.pad(t, pad) for t in (q, k, v))
    k_blocks = k.reshape(b, h, n_blk, MOBA_BLOCK, dh)
    v_blocks = v.reshape(b, h, n_blk, MOBA_BLOCK, dh)
    k_mean = jnp.mean(k_blocks.astype(jnp.float32), axis=3)
    q_block_id = jnp.arange(s_pad) // MOBA_BLOCK
    gate = jnp.einsum('bhsd,bhnd->bhsn', q.astype(jnp.float32), k_mean)
    fully_past = jnp.arange(n_blk)[None, :] < q_block_id[:, None]
    gate = jnp.where(fully_past, gate, -jnp.inf)
    k_sel = min(MOBA_TOPK, n_blk)
    _, sel = lax.top_k(gate, k_sel)
    n_chunks = s_pad // MOBA_QCHUNK
    q_chunks = q.reshape(b, h, n_chunks, MOBA_QCHUNK, dh).transpose(2, 0, 1, 3, 4)
    sel_chunks = sel.reshape(b, h, n_chunks, MOBA_QCHUNK, k_sel).transpose(2, 0, 1, 3, 4)
    scale = HEAD_DIM ** -0.5
    gather = jax.vmap(jax.vmap(lambda blocks, idx: blocks[idx]))

    def one_chunk(args):
        ci, q_c, sel_c = args
        q_start = ci * MOBA_QCHUNK
        blk = q_start // MOBA_BLOCK
        k_own = lax.dynamic_index_in_dim(k_blocks, blk, axis=2, keepdims=False)
        v_own = lax.dynamic_index_in_dim(v_blocks, blk, axis=2, keepdims=False)
        q_pos = q_start + jnp.arange(MOBA_QCHUNK)
        k_pos = blk * MOBA_BLOCK + jnp.arange(MOBA_BLOCK)
        s_own = jnp.einsum('bhqd,bhkd->bhqk', q_c, k_own, preferred_element_type=jnp.float32) * scale
        s_own = jnp.where(k_pos[None, :] <= q_pos[:, None], s_own, -jnp.inf)
        k_g = gather(k_blocks, sel_c)
        v_g = gather(v_blocks, sel_c)
        s_sel = jnp.einsum('bhqd,bhqnkd->bhqnk', q_c, k_g, preferred_element_type=jnp.float32) * scale
        s_sel = jnp.where((sel_c < blk)[..., None], s_sel, -jnp.inf)
        scores = jnp.concatenate([s_own, s_sel.reshape(b, h, MOBA_QCHUNK, k_sel * MOBA_BLOCK)], axis=-1)
        p = jax.nn.softmax(scores, axis=-1).astype(v.dtype)
        p_own = p[..., :MOBA_BLOCK]
        p_sel = p[..., MOBA_BLOCK:].reshape(b, h, MOBA_QCHUNK, k_sel, MOBA_BLOCK)
        return (jnp.einsum('bhqk,bhkd->bhqd', p_own, v_own)
                + jnp.einsum('bhqnk,bhqnkd->bhqd', p_sel, v_g))

    out = lax.map(one_chunk, (jnp.arange(n_chunks), q_chunks, sel_chunks))
    out = out.transpose(1, 2, 0, 3, 4).reshape(b, h, s_pad, dh)
    return out[:, :, :s]


def forgetting_attention(q, k, v, log_f):
    b, h, s, dh = q.shape
    cum = jnp.cumsum(log_f, axis=-1)
    n_qb = s // FOX_QBLOCK
    q_blocks = q.reshape(b, h, n_qb, FOX_QBLOCK, dh).transpose(2, 0, 1, 3, 4)
    cum_blocks = cum.reshape(b, h, n_qb, FOX_QBLOCK).transpose(2, 0, 1, 3)
    k_pos = jnp.arange(s)
    scale = HEAD_DIM ** -0.5

    def one_block(args):
        bi, q_b, cum_q = args
        q_pos = bi * FOX_QBLOCK + jnp.arange(FOX_QBLOCK)
        scores = jnp.einsum('bhqd,bhkd->bhqk', q_b, k, preferred_element_type=jnp.float32) * scale
        scores = scores + cum_q[..., :, None] - cum[:, :, None, :]
        scores = jnp.where(k_pos[None, :] <= q_pos[:, None], scores, -jnp.inf)
        p = jax.nn.softmax(scores, axis=-1).astype(v.dtype)
        return jnp.einsum('bhqk,bhkd->bhqd', p, v)

    out = lax.map(one_block, (jnp.arange(n_qb), q_blocks, cum_blocks))
    return out.transpose(1, 2, 0, 3, 4).reshape(b, h, s, dh)


def hybrid_attention_mixer(h, w_in, b_f, w_out):
    b, s, _ = h.shape
    proj = h @ w_in

    def to_heads(t):
        return t.reshape(b, s, N_ATTN_HEADS, HEAD_DIM).transpose(0, 2, 1, 3)

    q = to_heads(proj[..., :ATTN_WIDTH])
    k = to_heads(proj[..., ATTN_WIDTH:2 * ATTN_WIDTH])
    v = to_heads(proj[..., 2 * ATTN_WIDTH:3 * ATTN_WIDTH])
    f_logit = proj[..., 3 * ATTN_WIDTH:] + b_f
    pos = jnp.arange(s)
    q_m = apply_partial_rope(q[:, :N_MOBA_HEADS], pos)
    k_m = apply_partial_rope(k[:, :N_MOBA_HEADS], pos)
    o_moba = moba_attention(q_m, k_m, v[:, :N_MOBA_HEADS])
    log_f = jax.nn.log_sigmoid(f_logit.astype(jnp.float32)).transpose(0, 2, 1)
    o_fox = forgetting_attention(q[:, N_MOBA_HEADS:], k[:, N_MOBA_HEADS:], v[:, N_MOBA_HEADS:], log_f)
    o = jnp.concatenate([o_moba, o_fox], axis=1).transpose(0, 2, 1, 3).reshape(b, s, ATTN_WIDTH)
    return o @ w_out


def _linear_recurrence_combine(left, right):
    a_l, b_l = left
    a_r, b_r = right
    return a_l * a_r, a_r * b_l + b_r


def rglru_mixer(h, w_in, conv_w, conv_b, w_a, b_a, w_x, b_x, lam, w_out):
    b, s, _ = h.shape
    proj = h @ w_in
    gate_branch, u = proj[..., :D_RNN], proj[..., D_RNN:]
    u_pad = jnp.pad(u, ((0, 0), (CONV_WIDTH - 1, 0), (0, 0)))
    conv = conv_b
    for j in range(CONV_WIDTH):
        conv = conv + u_pad[:, j:j + s] * conv_w[j]
    ub = conv.reshape(b, s, N_RNN_BLOCKS, RNN_BLOCK_W)
    r = jax.nn.sigmoid(jnp.einsum('bsnc,ncd->bsnd', ub, w_a).reshape(b, s, D_RNN) + b_a)
    i = jax.nn.sigmoid(jnp.einsum('bsnc,ncd->bsnd', ub, w_x).reshape(b, s, D_RNN) + b_x)
    log_a = (-RG_C * r.astype(jnp.float32)) * jax.nn.softplus(-lam.astype(jnp.float32))
    a = jnp.exp(log_a)
    mult = jnp.sqrt(-jnp.expm1(2.0 * log_a))
    xin = mult * (i * conv).astype(jnp.float32)
    _, hs = lax.associative_scan(_linear_recurrence_combine, (a, xin), axis=1)
    y = jax.nn.gelu(gate_branch) * hs.astype(h.dtype)
    return y @ w_out


def swiglu(h, w_gate, w_up, w_down):
    return (jax.nn.silu(h @ w_gate) * (h @ w_up)) @ w_down


def moe_swiglu(h, w_router, b_router, w_gate, w_up, w_down):
    b, s, d = h.shape
    n_tok = b * s
    xt = h.reshape(n_tok, d)
    logits = (xt @ w_router).astype(jnp.float32) + b_router.astype(jnp.float32)
    top_logit, top_idx = lax.top_k(logits, TOP_K)
    gates = jax.nn.softmax(top_logit, axis=-1)
    n_asg = n_tok * TOP_K
    exp_flat = top_idx.reshape(-1).astype(jnp.int32)
    tok_flat = jnp.repeat(jnp.arange(n_tok, dtype=jnp.int32), TOP_K)
    gate_flat = gates.reshape(-1)
    order = jnp.argsort(exp_flat)
    exp_sorted = exp_flat[order]
    counts = jnp.zeros((N_EXPERTS,), jnp.int32).at[exp_flat].add(1)
    padded = ((counts + MOE_BLOCK - 1) // MOE_BLOCK) * MOE_BLOCK
    starts = jnp.cumsum(counts) - counts
    pends = jnp.cumsum(padded)
    pstarts = pends - padded
    dest = pstarts[exp_sorted] + (jnp.arange(n_asg, dtype=jnp.int32) - starts[exp_sorted])
    n_blocks = -(-n_asg // MOE_BLOCK) + N_EXPERTS
    cap = n_blocks * MOE_BLOCK
    slot_tok = jnp.zeros((cap,), jnp.int32).at[dest].set(tok_flat[order])
    slot_gate = jnp.zeros((cap,), gates.dtype).at[dest].set(gate_flat[order])
    block_starts = jnp.arange(n_blocks, dtype=jnp.int32) * MOE_BLOCK
    block_exp = jnp.minimum(jnp.searchsorted(pends, block_starts, side='right'), N_EXPERTS - 1)

    def one_block(args):
        tok, e = args
        xb = xt[tok]
        return (jax.nn.silu(xb @ w_gate[e]) * (xb @ w_up[e])) @ w_down[e]

    y_slots = lax.map(one_block, (slot_tok.reshape(n_blocks, MOE_BLOCK), block_exp))
    y_slots = y_slots.reshape(cap, d) * slot_gate[:, None].astype(y_slots.dtype)
    y = jax.ops.segment_sum(y_slots, slot_tok, num_segments=n_tok)
    return y.reshape(b, s, d)


def modulated_sublayer(x, fn, shift, scale, gate, g_pre, g_post):
    h = rms_norm(x, g_pre) * (1.0 + scale[:, None, :]) + shift[:, None, :]
    y = rms_norm(fn(h), g_post)
    return x + gate[:, None, :] * y


def setup_inputs(seed: int = 0) -> dict:
    key = jax.random.key(seed)
    ks = jax.random.split(key, 25)
    f32 = jnp.float32
    ne, no = N_EVEN_LAYERS, N_ODD_LAYERS

    def normal(k, shape, fan_in, gain=1.0):
        return jax.random.normal(k, shape, f32) * (gain * fan_in ** -0.5)

    def small(k, shape, s=0.02):
        return s * jax.random.normal(k, shape, f32)

    x = jax.random.normal(ks[0], (BATCH, SEQ, D_MODEL), f32)
    c = jax.random.normal(ks[1], (BATCH, D_MODEL), f32)
    w_ada = normal(ks[2], (DEPTH, D_MODEL, 6 * D_MODEL), D_MODEL, 0.5)
    b_ada = small(ks[3], (DEPTH, 6 * D_MODEL))
    norm_g = 1.0 + 0.05 * jax.random.normal(ks[4], (DEPTH, 4, D_MODEL), f32)
    attn_w_in = normal(ks[5], (ne, D_MODEL, 3 * ATTN_WIDTH + N_FOX_HEADS), D_MODEL)
    fox_b_f = FOX_BIAS_MEAN + 0.5 * jax.random.normal(ks[6], (ne, N_FOX_HEADS), f32)
    attn_w_out = normal(ks[7], (ne, ATTN_WIDTH, D_MODEL), ATTN_WIDTH)
    ffn_w_gate = normal(ks[8], (ne, D_MODEL, D_FF), D_MODEL)
    ffn_w_up = normal(ks[9], (ne, D_MODEL, D_FF), D_MODEL)
    ffn_w_down = normal(ks[10], (ne, D_FF, D_MODEL), D_FF)
    lru_w_in = normal(ks[11], (no, D_MODEL, 2 * D_RNN), D_MODEL)
    lru_conv_w = normal(ks[12], (no, CONV_WIDTH, D_RNN), CONV_WIDTH)
    lru_conv_b = small(ks[13], (no, D_RNN))
    lru_w_a = normal(ks[14], (no, N_RNN_BLOCKS, RNN_BLOCK_W, RNN_BLOCK_W), RNN_BLOCK_W)
    lru_b_a = small(ks[15], (no, D_RNN))
    lru_w_x = normal(ks[16], (no, N_RNN_BLOCKS, RNN_BLOCK_W, RNN_BLOCK_W), RNN_BLOCK_W)
    lru_b_x = small(ks[17], (no, D_RNN))
    a_pow = jax.random.uniform(ks[18], (no, D_RNN), f32, minval=0.9, maxval=0.999)
    a_base = a_pow ** (1.0 / RG_C)
    lru_lambda = jnp.log(a_base) - jnp.log1p(-a_base)
    lru_w_out = normal(ks[19], (no, D_RNN, D_MODEL), D_RNN)
    moe_w_router = normal(ks[20], (no, D_MODEL, N_EXPERTS), D_MODEL)
    moe_b_router = small(ks[21], (no, N_EXPERTS), 0.01)
    moe_w_gate = normal(ks[22], (no, N_EXPERTS, D_MODEL, D_EXPERT), D_MODEL)
    moe_w_up = normal(ks[23], (no, N_EXPERTS, D_MODEL, D_EXPERT), D_MODEL)
    moe_w_down = normal(ks[24], (no, N_EXPERTS, D_EXPERT, D_MODEL), D_EXPERT)
    return {"x": x, "c": c, "w_ada": w_ada, "b_ada": b_ada, "norm_g": norm_g,
            "attn_w_in": attn_w_in, "fox_b_f": fox_b_f, "attn_w_out": attn_w_out,
            "ffn_w_gate": ffn_w_gate, "ffn_w_up": ffn_w_up, "ffn_w_down": ffn_w_down,
            "lru_w_in": lru_w_in, "lru_conv_w": lru_conv_w, "lru_conv_b": lru_conv_b,
            "lru_w_a": lru_w_a, "lru_b_a": lru_b_a, "lru_w_x": lru_w_x, "lru_b_x": lru_b_x,
            "lru_lambda": lru_lambda, "lru_w_out": lru_w_out,
            "moe_w_router": moe_w_router, "moe_b_router": moe_b_router,
            "moe_w_gate": moe_w_gate, "moe_w_up": moe_w_up, "moe_w_down": moe_w_down}


def reference(x, c, w_ada, b_ada, norm_g, attn_w_in, fox_b_f, attn_w_out,
              ffn_w_gate, ffn_w_up, ffn_w_down, lru_w_in, lru_conv_w, lru_conv_b,
              lru_w_a, lru_b_a, lru_w_x, lru_b_x, lru_lambda, lru_w_out,
              moe_w_router, moe_b_router, moe_w_gate, moe_w_up, moe_w_down):
    cond = jax.nn.silu(c)
    for layer in range(DEPTH):
        i = layer // 2
        mod = cond @ w_ada[layer] + b_ada[layer]
        sh_m, sc_m, g_m, sh_f, sc_f, g_f = jnp.split(mod, 6, axis=-1)
        if layer % 2 == 0:
            mixer = functools.partial(hybrid_attention_mixer, w_in=attn_w_in[i], b_f=fox_b_f[i],
                                      w_out=attn_w_out[i])
            ffn = functools.partial(swiglu, w_gate=ffn_w_gate[i], w_up=ffn_w_up[i], w_down=ffn_w_down[i])
        else:
            mixer = functools.partial(rglru_mixer, w_in=lru_w_in[i], conv_w=lru_conv_w[i],
                                      conv_b=lru_conv_b[i], w_a=lru_w_a[i], b_a=lru_b_a[i],
                                      w_x=lru_w_x[i], b_x=lru_b_x[i], lam=lru_lambda[i],
                                      w_out=lru_w_out[i])
            ffn = functools.partial(moe_swiglu, w_router=moe_w_router[i], b_router=moe_b_router[i],
                                    w_gate=moe_w_gate[i], w_up=moe_w_up[i], w_down=moe_w_down[i])
        x = modulated_sublayer(x, mixer, sh_m, sc_m, g_m, norm_g[layer, 0], norm_g[layer, 1])
        x = modulated_sublayer(x, ffn, sh_f, sc_f, g_f, norm_g[layer, 2], norm_g[layer, 3])
    return x
```

```python
import functools

import jax
import jax.numpy as jnp
from jax import lax
from jax.experimental import pallas as pl
from jax.experimental.pallas import tpu as pltpu

F32 = jnp.float32
BF16 = jnp.bfloat16
HIGHEST = lax.Precision.HIGHEST

NORM_EPS = 1e-6
HEAD_DIM = 64
N_MOBA_HEADS = 8
N_FOX_HEADS = 8
ROPE_DIMS = 16
ROPE_THETA = 500000.0
MOBA_BLOCK = 256
MOBA_TOPK = 3
CONV_WIDTH = 4
RG_C = 8.0
TOP_K = 2

LANES = 128
ATT_BLOCK = 256
NEG = -1e30
MOE_BLOCK_ROWS = 512
MOE_FF_CHUNK = 512
VMEM_LIMIT = 56 * 1024 * 1024


def _cparams(sem, vmem=None):
    return pltpu.CompilerParams(dimension_semantics=sem, vmem_limit_bytes=vmem)


def _rms(x, g):
    return x * lax.rsqrt(jnp.mean(x * x, axis=-1, keepdims=True) + NORM_EPS) * g


def _const_spec(shape):
    n = len(shape)
    return pl.BlockSpec(shape, lambda *_: (0,) * n)


def _ada_kernel(c_ref, w_ref, b_ref, o_ref):
    c = c_ref[...]
    cond = c * jax.nn.sigmoid(c)
    o_ref[...] = jnp.dot(cond, w_ref[...], preferred_element_type=F32, precision=HIGHEST) + b_ref[...]


def _ada(c_pad, w_ada, b_ada):
    depth, d, d6 = w_ada.shape
    rows = c_pad.shape[0]
    nj = d6 // d
    return pl.pallas_call(
        _ada_kernel,
        grid=(depth, nj),
        in_specs=[
            pl.BlockSpec((rows, d), lambda l, j: (0, 0)),
            pl.BlockSpec((None, d, d), lambda l, j: (l, 0, j)),
            pl.BlockSpec((None, 1, d), lambda l, j: (l, 0, j)),
        ],
        out_specs=pl.BlockSpec((None, rows, d), lambda l, j: (l, 0, j)),
        out_shape=jax.ShapeDtypeStruct((depth, rows, d6), F32),
        compiler_params=_cparams(("parallel", "parallel")),
        name="ada",
    )(c_pad, w_ada, b_ada.reshape(depth, 1, d6))


def _l0_in_kernel(x_ref, mod_ref, g_ref, w_ref, wf_ref, bf_ref, rc_ref, rsa_ref, rsb_ref,
                  q_ref, k_ref, v_ref, lf_ref, *, d, aw, n_rope_chunks):
    m = mod_ref[...]
    h = _rms(x_ref[...], g_ref[...]) * (1.0 + m[:, d:2 * d]) + m[:, 0:d]
    hb = h.astype(BF16)
    proj = jnp.dot(hb, w_ref[...], preferred_element_type=F32)
    rc, rsa, rsb = rc_ref[...], rsa_ref[...], rsb_ref[...]

    def rope(t):
        return t * rc + pltpu.roll(t, ROPE_DIMS // 2, 1) * rsa + pltpu.roll(t, LANES - ROPE_DIMS // 2, 1) * rsb

    scale = HEAD_DIM ** -0.5
    for c in range(aw // LANES):
        sl = slice(c * LANES, (c + 1) * LANES)
        qc = proj[:, sl] * scale
        kc = proj[:, aw + c * LANES:aw + (c + 1) * LANES]
        if c < n_rope_chunks:
            qc, kc = rope(qc), rope(kc)
        q_ref[:, sl] = qc.astype(BF16)
        k_ref[:, sl] = kc.astype(BF16)
    v_ref[...] = proj[:, 2 * aw:3 * aw].astype(BF16)
    fl = jnp.dot(hb, wf_ref[...], preferred_element_type=F32) + bf_ref[...]
    lf_ref[...] = jnp.minimum(fl, 0.0) - jnp.log(1.0 + jnp.exp(-jnp.abs(fl)))


def _l0_in(x, mods, g_pre, w_qkv, w_f, b_f, rope_c, rope_sa, rope_sb, tm):
    b, s, d = x.shape
    aw = w_qkv.shape[1] // 3
    d6 = mods.shape[-1]
    tok = lambda bi, i: (bi, i, 0)
    kern = functools.partial(_l0_in_kernel, d=d, aw=aw,
                             n_rope_chunks=N_MOBA_HEADS * HEAD_DIM // LANES)
    return pl.pallas_call(
        kern,
        grid=(b, s // tm),
        in_specs=[
            pl.BlockSpec((None, tm, d), tok),
            pl.BlockSpec((None, None, 1, d6), lambda bi, i: (0, bi, 0, 0)),
            _const_spec((1, d)),
            _const_spec(w_qkv.shape),
            _const_spec(w_f.shape),
            _const_spec((1, LANES)),
            pl.BlockSpec((tm, LANES), lambda bi, i: (i, 0)),
            pl.BlockSpec((tm, LANES), lambda bi, i: (i, 0)),
            pl.BlockSpec((tm, LANES), lambda bi, i: (i, 0)),
        ],
        out_specs=[
            pl.BlockSpec((None, tm, aw), tok),
            pl.BlockSpec((None, tm, aw), tok),
            pl.BlockSpec((None, tm, aw), tok),
            pl.BlockSpec((None, tm, LANES), tok),
        ],
        out_shape=[
            jax.ShapeDtypeStruct((b, s, aw), BF16),
            jax.ShapeDtypeStruct((b, s, aw), BF16),
            jax.ShapeDtypeStruct((b, s, aw), BF16),
            jax.ShapeDtypeStruct((b, s, LANES), F32),
        ],
        compiler_params=_cparams(("parallel", "parallel"), VMEM_LIMIT),
        name="l0_in",
    )(x, mods, g_pre, w_qkv, w_f, b_f, rope_c, rope_sa, rope_sb)


def _fox_prep_kernel(lf_ref, qa_ref, ka_ref, carry_ref, *, n_heads):
    t = lf_ref.shape[0]

    @pl.when(pl.program_id(1) == 0)
    def _():
        carry_ref[...] = jnp.zeros_like(carry_ref)

    row = lax.broadcasted_iota(jnp.int32, (t, t), 0)
    col = lax.broadcasted_iota(jnp.int32, (t, t), 1)
    tri = (col <= row).astype(F32)
    cum = jnp.dot(tri, lf_ref[...], preferred_element_type=F32, precision=HIGHEST) + carry_ref[...]
    carry_ref[...] = cum[t - 1:t, :]
    lane = lax.broadcasted_iota(jnp.int32, (t, LANES), 1)
    for h in range(n_heads):
        c = jnp.broadcast_to(cum[:, h:h + 1], (t, LANES))
        hi = c.astype(BF16).astype(F32)
        r1 = c - hi
        mid = r1.astype(BF16).astype(F32)
        lo = r1 - mid
        qa = jnp.where(lane == 0, hi, jnp.where(lane == 1, mid, jnp.where(lane == 2, lo,
                       jnp.where(lane < 6, 1.0, 0.0))))
        ka = jnp.where(lane < 3, 1.0, jnp.where(lane == 3, -hi, jnp.where(lane == 4, -mid,
                       jnp.where(lane == 5, -lo, 0.0))))
        qa_ref[h] = qa.astype(BF16)
        ka_ref[h] = ka.astype(BF16)


def _fox_prep(logf, n_heads):
    b, s, _ = logf.shape
    t = ATT_BLOCK
    aux = jax.ShapeDtypeStruct((b, n_heads, s, LANES), BF16)
    aux_spec = pl.BlockSpec((None, n_heads, t, LANES), lambda bi, i: (bi, 0, i, 0))
    return pl.pallas_call(
        functools.partial(_fox_prep_kernel, n_heads=n_heads),
        grid=(b, s // t),
        in_specs=[pl.BlockSpec((None, t, LANES), lambda bi, i: (bi, i, 0))],
        out_specs=[aux_spec, aux_spec],
        out_shape=[aux, aux],
        scratch_shapes=[pltpu.VMEM((1, LANES), F32)],
        compiler_params=_cparams(("parallel", "arbitrary")),
        name="fox_prep",
    )(logf)


def _moba_gate_kernel(q_ref, k_ref, qa_ref, *, n_blk):
    s = k_ref.shape[0]
    t = MOBA_BLOCK
    half = HEAD_DIM
    rn = lax.broadcasted_iota(jnp.int32, (n_blk, s), 0)
    cs = lax.broadcasted_iota(jnp.int32, (n_blk, s), 1)
    avg = jnp.where((cs >= rn * t) & (cs < (rn + 1) * t), 1.0 / t, 0.0).astype(BF16)
    km = jnp.dot(avg, k_ref[...], preferred_element_type=F32)
    lane_k = lax.broadcasted_iota(jnp.int32, (n_blk, LANES), 1)
    km2 = jnp.concatenate([jnp.where(lane_k < half, km, 0.0), jnp.where(lane_k >= half, km, 0.0)], axis=0)
    pad = jnp.zeros((LANES - 2 * n_blk, LANES), F32)
    km2 = jnp.concatenate([km2, pad], axis=0)
    hi = km2.astype(BF16)
    r1 = km2 - hi.astype(F32)
    mid = r1.astype(BF16)
    lo = (r1 - mid.astype(F32)).astype(BF16)
    rhs = jnp.concatenate([hi, mid, lo], axis=0)
    lane = lax.broadcasted_iota(jnp.int32, (t, LANES), 1)

    def body(n, carry):
        rows = pl.ds(pl.multiple_of(n * t, t), t)
        g3 = lax.dot_general(q_ref[rows, :], rhs, (((1,), (1,)), ((), ())), preferred_element_type=F32)
        g = g3[:, 0:LANES] + g3[:, LANES:2 * LANES] + g3[:, 2 * LANES:3 * LANES]
        for hh in range(2):
            local = lane - hh * n_blk
            in_head = (local >= 0) & (local < n_blk)
            gv = jnp.where(in_head & (local < n), g, -jnp.inf)
            keep = local == n
            for _ in range(MOBA_TOPK):
                mx = jnp.max(gv, axis=-1, keepdims=True)
                cand = jnp.where((gv == mx) & (mx > -jnp.inf), lane, LANES)
                pick = lane == jnp.min(cand, axis=-1, keepdims=True)
                keep = keep | pick
                gv = jnp.where(pick, -jnp.inf, gv)
            qa_ref[hh, rows, :] = jnp.where(in_head & jnp.logical_not(keep), NEG, 0.0).astype(BF16)
        return carry

    lax.fori_loop(0, n_blk, body, 0)


def _moba_gate(q, k, n_heads):
    b, s, _ = q.shape
    n_blk = s // MOBA_BLOCK
    assert 2 * n_blk <= LANES
    return pl.pallas_call(
        functools.partial(_moba_gate_kernel, n_blk=n_blk),
        grid=(b, n_heads // 2),
        in_specs=[
            pl.BlockSpec((None, s, LANES), lambda bi, p: (bi, 0, p)),
            pl.BlockSpec((None, s, LANES), lambda bi, p: (bi, 0, p)),
        ],
        out_specs=pl.BlockSpec((None, 2, s, LANES), lambda bi, p: (bi, p, 0, 0)),
        out_shape=jax.ShapeDtypeStruct((b, n_heads, s, LANES), BF16),
        compiler_params=_cparams(("parallel", "parallel")),
        name="moba_gate",
    )(q, k)


def _attn_kernel(q_ref, k_ref, v_ref, qa_ref, ka_ref, o_ref):
    t = ATT_BLOCK
    i = pl.program_id(2)
    lane = lax.broadcasted_iota(jnp.int32, (t, LANES), 1)
    row = lax.broadcasted_iota(jnp.int32, (t, t), 0)
    col = lax.broadcasted_iota(jnp.int32, (t, t), 1)
    q = q_ref[...]
    nt = (((1,), (1,)), ((), ()))
    outs = []
    for hh in range(2):
        in_head = (lane < HEAD_DIM) if hh == 0 else (lane >= HEAD_DIM)
        qa = jnp.concatenate([jnp.where(in_head, q, jnp.zeros_like(q)), qa_ref[hh]], axis=1)

        def scores(j, qa=qa, hh=hh):
            rows = pl.ds(pl.multiple_of(j * t, t), t)
            kk = jnp.concatenate([k_ref[rows, :], ka_ref[hh, rows, :]], axis=1)
            return lax.dot_general(qa, kk, nt, preferred_element_type=F32), rows

        sc, rows = scores(i)
        sc = jnp.where(col <= row, sc, NEG)
        m = jnp.max(sc, axis=-1, keepdims=True)
        p = jnp.exp(sc - m)
        l = jnp.sum(p, axis=-1, keepdims=True)
        acc = jnp.dot(p.astype(BF16), v_ref[rows, :], preferred_element_type=F32)

        def body(j, carry, scores=scores):
            m, l, acc = carry
            sc, rows = scores(j)
            m_new = jnp.maximum(m, jnp.max(sc, axis=-1, keepdims=True))
            alpha = jnp.exp(m - m_new)
            p = jnp.exp(sc - m_new)
            l = alpha * l + jnp.sum(p, axis=-1, keepdims=True)
            acc = alpha * acc + jnp.dot(p.astype(BF16), v_ref[rows, :], preferred_element_type=F32)
            return m_new, l, acc

        m, l, acc = lax.fori_loop(0, i, body, (m, l, acc))
        outs.append(acc / l)
    o_ref[...] = jnp.where(lane < HEAD_DIM, outs[0], outs[1]).astype(o_ref.dtype)


def _attention(q, k, v, qaux, kaux, n_heads, head_off):
    b, s, _ = q.shape
    t = ATT_BLOCK
    po = head_off // 2
    kb, kh = kaux.shape[0], kaux.shape[1]
    ka_map = (lambda bi, p, i: (bi, p, 0, 0)) if kb == b and kh == n_heads else (lambda bi, p, i: (0, 0, 0, 0))
    return pl.pallas_call(
        _attn_kernel,
        grid=(b, n_heads // 2, s // t),
        in_specs=[
            pl.BlockSpec((None, t, LANES), lambda bi, p, i: (bi, i, p + po)),
            pl.BlockSpec((None, s, LANES), lambda bi, p, i: (bi, 0, p + po)),
            pl.BlockSpec((None, s, LANES), lambda bi, p, i: (bi, 0, p + po)),
            pl.BlockSpec((None, 2, t, LANES), lambda bi, p, i: (bi, p, i, 0)),
            pl.BlockSpec((None, 2, s, LANES), ka_map),
        ],
        out_specs=pl.BlockSpec((None, t, LANES), lambda bi, p, i: (bi, i, p)),
        out_shape=jax.ShapeDtypeStruct((b, s, n_heads * HEAD_DIM), BF16),
        compiler_params=_cparams(("parallel", "parallel", "arbitrary")),
        name="attn",
    )(q, k, v, qaux, kaux)


def _l0_out_kernel(om_ref, of_ref, w_ref, x_ref, mod_ref, g1_ref, g2_ref, x1_ref, h2_ref, *, d):
    m = mod_ref[...]
    hw = om_ref.shape[1]
    y = (jnp.dot(om_ref[...], w_ref[0:hw, :], preferred_element_type=F32)
         + jnp.dot(of_ref[...], w_ref[hw:, :], preferred_element_type=F32))
    x1 = x_ref[...] + m[:, 2 * d:3 * d] * _rms(y, g1_ref[...])
    x1_ref[...] = x1
    h2_ref[...] = (_rms(x1, g2_ref[...]) * (1.0 + m[:, 4 * d:5 * d]) + m[:, 3 * d:4 * d]).astype(BF16)


def _l0_out(o_m, o_f, w_out, x, mods, g_post, g_pre2, tm):
    b, s, d = x.shape
    d6 = mods.shape[-1]
    tok = lambda bi, i: (bi, i, 0)
    return pl.pallas_call(
        functools.partial(_l0_out_kernel, d=d),
        grid=(b, s // tm),
        in_specs=[
            pl.BlockSpec((None, tm, o_m.shape[2]), tok),
            pl.BlockSpec((None, tm, o_f.shape[2]), tok),
            _const_spec(w_out.shape),
            pl.BlockSpec((None, tm, d), tok),
            pl.BlockSpec((None, None, 1, d6), lambda bi, i: (0, bi, 0, 0)),
            _const_spec((1, d)),
            _const_spec((1, d)),
        ],
        out_specs=[pl.BlockSpec((None, tm, d), tok), pl.BlockSpec((None, tm, d), tok)],
        out_shape=[jax.ShapeDtypeStruct((b, s, d), F32), jax.ShapeDtypeStruct((b, s, d), BF16)],
        compiler_params=_cparams(("parallel", "parallel"), VMEM_LIMIT),
        name="l0_out",
    )(o_m, o_f, w_out, x, mods, g_post, g_pre2)


def _ffn_kernel(h_ref, x_ref, wg_ref, wu_ref, wd_ref, mod0_ref, mod1_ref, g1_ref, g2_ref,
                x2_ref, h3_ref, *, d):
    h = h_ref[...]
    a = jnp.dot(h, wg_ref[...], preferred_element_type=F32)
    u = jnp.dot(h, wu_ref[...], preferred_element_type=F32)
    act = (a * jax.nn.sigmoid(a) * u).astype(BF16)
    y = jnp.dot(act, wd_ref[...], preferred_element_type=F32)
    m0 = mod0_ref[...]
    m1 = mod1_ref[...]
    x2 = x_ref[...] + m0[:, 5 * d:6 * d] * _rms(y, g1_ref[...])
    x2_ref[...] = x2
    h3_ref[...] = (_rms(x2, g2_ref[...]) * (1.0 + m1[:, d:2 * d]) + m1[:, 0:d]).astype(BF16)


def _ffn(h2, x1, w_gate, w_up, w_down, mods, g_post, g_pre_next, tm):
    b, s, d = x1.shape
    d6 = mods.shape[-1]
    tok = lambda bi, i: (bi, i, 0)
    single = dict(pipeline_mode=pl.Buffered(1))
    return pl.pallas_call(
        functools.partial(_ffn_kernel, d=d),
        grid=(b, s // tm),
        in_specs=[
            pl.BlockSpec((None, tm, d), tok),
            pl.BlockSpec((None, tm, d), tok),
            pl.BlockSpec(w_gate.shape, lambda bi, i: (0, 0), **single),
            pl.BlockSpec(w_up.shape, lambda bi, i: (0, 0), **single),
            pl.BlockSpec(w_down.shape, lambda bi, i: (0, 0), **single),
            pl.BlockSpec((None, None, 1, d6), lambda bi, i: (0, bi, 0, 0)),
            pl.BlockSpec((None, None, 1, d6), lambda bi, i: (1, bi, 0, 0)),
            _const_spec((1, d)),
            _const_spec((1, d)),
        ],
        out_specs=[pl.BlockSpec((None, tm, d), tok), pl.BlockSpec((None, tm, d), tok)],
        out_shape=[jax.ShapeDtypeStruct((b, s, d), F32), jax.ShapeDtypeStruct((b, s, d), BF16)],
        compiler_params=_cparams(("parallel", "parallel"), VMEM_LIMIT),
        name="ffn",
    )(h2, x1, w_gate, w_up, w_down, mods, mods, g_post, g_pre_next)


def _lru_kernel(h_ref, x_ref, win_ref, cw_ref, cb_ref, wax_ref, ba_ref, bx_ref, lam_ref, wout_ref,
                mod_ref, g1_ref, g2_ref, wr_ref, br_ref,
                x3_ref, h4_ref, lg_ref, ubuf_ref, hc_ref, *, d, dr, n_rnn_blocks):
    tm = h_ref.shape[0]
    tail = 8

    @pl.when(pl.program_id(1) == 0)
    def _():
        ubuf_ref[0:tail, :] = jnp.zeros((tail, dr), F32)
        hc_ref[...] = jnp.zeros_like(hc_ref)

    proj = jnp.dot(h_ref[...], win_ref[...], preferred_element_type=F32)
    gate_branch = proj[:, :dr]
    ubuf_ref[tail:tail + tm, :] = proj[:, dr:]
    conv = jnp.broadcast_to(cb_ref[...], (tm, dr))
    for j in range(CONV_WIDTH):
        off = tail - (CONV_WIDTH - 1) + j
        conv = conv + ubuf_ref[off:off + tm, :] * cw_ref[j:j + 1, :]
    ubuf_ref[0:tail, :] = ubuf_ref[tm:tm + tail, :]

    w = dr // n_rnn_blocks
    rs, is_ = [], []
    for n in range(n_rnn_blocks):
        cbk = conv[:, n * w:(n + 1) * w].astype(BF16)
        ra = jnp.dot(cbk, wax_ref[n], preferred_element_type=F32)
        rs.append(ra[:, :w])
        is_.append(ra[:, w:])
    r = jax.nn.sigmoid(jnp.concatenate(rs, axis=1) + ba_ref[...])
    ig = jax.nn.sigmoid(jnp.concatenate(is_, axis=1) + bx_ref[...])
    nl = -lam_ref[...]
    softplus = jnp.maximum(nl, 0.0) + jnp.log(1.0 + jnp.exp(-jnp.abs(nl)))
    log_a = (-RG_C * r) * softplus
    a = jnp.exp(log_a)
    xin = jnp.sqrt(1.0 - jnp.exp(2.0 * log_a)) * (ig * conv)

    rowi = lax.broadcasted_iota(jnp.int32, (tm, dr), 0)
    sa, sx = a, xin
    dist = 1
    while dist < tm:
        keep = rowi >= dist
        xs = jnp.where(keep, pltpu.roll(sx, dist, 0), 0.0)
        as_ = jnp.where(keep, pltpu.roll(sa, dist, 0), 1.0)
        sx = sx + sa * xs
        sa = sa * as_
        dist *= 2
    hs = sx + sa * hc_ref[...]
    hc_ref[...] = hs[tm - 1:tm, :]

    y = (jax.nn.gelu(gate_branch, approximate=True) * hs).astype(BF16)
    out = jnp.dot(y, wout_ref[...], preferred_element_type=F32)
    m = mod_ref[...]
    x3 = x_ref[...] + m[:, 2 * d:3 * d] * _rms(out, g1_ref[...])
    x3_ref[...] = x3
    h4 = _rms(x3, g2_ref[...]) * (1.0 + m[:, 4 * d:5 * d]) + m[:, 3 * d:4 * d]
    h4_ref[...] = h4
    lg_ref[...] = jnp.dot(h4, wr_ref[...], preferred_element_type=F32, precision=HIGHEST) + br_ref[...]


def _lru(h3, x2, w_in, conv_w, conv_b, wax, b_a, b_x, lam, w_out, mods, g_post, g_pre2, w_r, b_r, tm):
    b, s, d = x2.shape
    dr = w_out.shape[0]
    d6 = mods.shape[-1]
    nb = wax.shape[0]
    tok = lambda bi, i: (bi, i, 0)
    return pl.pallas_call(
        functools.partial(_lru_kernel, d=d, dr=dr, n_rnn_blocks=nb),
        grid=(b, s // tm),
        in_specs=[
            pl.BlockSpec((None, tm, d), tok),
            pl.BlockSpec((None, tm, d), tok),
            _const_spec(w_in.shape),
            _const_spec(conv_w.shape),
            _const_spec((1, dr)),
            _const_spec(wax.shape),
            _const_spec((1, dr)),
            _const_spec((1, dr)),
            _const_spec((1, dr)),
            _const_spec(w_out.shape),
            pl.BlockSpec((None, None, 1, d6), lambda bi, i: (1, bi, 0, 0)),
            _const_spec((1, d)),
            _const_spec((1, d)),
            _const_spec(w_r.shape),
            _const_spec((1, LANES)),
        ],
        out_specs=[pl.BlockSpec((None, tm, d), tok), pl.BlockSpec((None, tm, d), tok),
                   pl.BlockSpec((None, tm, LANES), tok)],
        out_shape=[jax.ShapeDtypeStruct((b, s, d), F32), jax.ShapeDtypeStruct((b, s, d), F32),
                   jax.ShapeDtypeStruct((b, s, LANES), F32)],
        scratch_shapes=[pltpu.VMEM((tm + 8, dr), F32), pltpu.VMEM((1, dr), F32)],
        compiler_params=_cparams(("parallel", "arbitrary"), VMEM_LIMIT),
        name="lru",
    )(h3, x2, w_in, conv_w, conv_b, wax, b_a, b_x, lam, w_out, mods, g_post, g_pre2, w_r, b_r)


def _moe_kernel(bexp_ref, nused_ref, tok_ref, h_hbm, wg_ref, wu_ref, wd_ref, y_ref, xbuf, sem, *, n_blocks):
    del bexp_ref
    mb = y_ref.shape[0]
    i = pl.program_id(0)
    nused = nused_ref[0]

    def row_copy(blk, slot, r):
        tok = tok_ref[blk * mb + r]
        return pltpu.make_async_copy(h_hbm.at[pl.ds(tok, 1), :], xbuf.at[slot, pl.ds(r, 1), :], sem.at[slot])

    def start_gather(blk, slot):
        def body(r, c):
            row_copy(blk, slot, r).start()
            return c
        lax.fori_loop(0, mb, body, 0)

    def wait_gather(blk, slot):
        def body(r, c):
            row_copy(blk, slot, r).wait()
            return c
        lax.fori_loop(0, mb, body, 0)

    slot = lax.rem(i, 2)

    @pl.when((i == 0) & (nused > 0))
    def _():
        start_gather(0, 0)

    @pl.when(i + 1 < nused)
    def _():
        start_gather(i + 1, 1 - slot)

    @pl.when(i < nused)
    def _():
        wait_gather(i, slot)
        x = xbuf[slot].astype(BF16)
        dff = wg_ref.shape[1]
        acc = jnp.zeros(y_ref.shape, F32)
        for c in range(dff // MOE_FF_CHUNK):
            cs = slice(c * MOE_FF_CHUNK, (c + 1) * MOE_FF_CHUNK)
            a = jnp.dot(x, wg_ref[:, cs], preferred_element_type=F32)
            u = jnp.dot(x, wu_ref[:, cs], preferred_element_type=F32)
            act = (a * jax.nn.sigmoid(a) * u).astype(BF16)
            acc = acc + jnp.dot(act, wd_ref[cs, :], preferred_element_type=F32)
        y_ref[...] = acc

    @pl.when(i >= nused)
    def _():
        y_ref[...] = jnp.zeros_like(y_ref)


def _moe(block_exp, nused, slot_tok, h4, w_gate, w_up, w_down):
    n_tok, d = h4.shape
    mb = MOE_BLOCK_ROWS
    cap = slot_tok.shape[0]
    n_blocks = cap // mb
    e, _, dff = w_gate.shape
    single = dict(pipeline_mode=pl.Buffered(1))
    grid_spec = pltpu.PrefetchScalarGridSpec(
        num_scalar_prefetch=3,
        grid=(n_blocks,),
        in_specs=[
            pl.BlockSpec(memory_space=pl.ANY),
            pl.BlockSpec((None, d, dff), lambda i, be, nu, st: (be[i], 0, 0), **single),
            pl.BlockSpec((None, d, dff), lambda i, be, nu, st: (be[i], 0, 0), **single),
            pl.BlockSpec((None, dff, d), lambda i, be, nu, st: (be[i], 0, 0), **single),
        ],
        out_specs=pl.BlockSpec((mb, d), lambda i, be, nu, st: (i, 0)),
        scratch_shapes=[pltpu.VMEM((2, mb, d), F32), pltpu.SemaphoreType.DMA((2,))],
    )
    return pl.pallas_call(
        functools.partial(_moe_kernel, n_blocks=n_blocks),
        grid_spec=grid_spec,
        out_shape=jax.ShapeDtypeStruct((cap, d), F32),
        compiler_params=_cparams(("arbitrary",), VMEM_LIMIT),
        name="moe",
    )(block_exp, nused, slot_tok, h4, w_gate, w_up, w_down)


def _moe_out_kernel(pos_ref, y_hbm, gate_ref, x_ref, mod_ref, g_ref, o_ref, gbuf, sem, *, d):
    tm = x_ref.shape[0]
    i = pl.program_id(0)
    n = pl.num_programs(0)

    def row_copy(blk, slot, r, kk):
        p = pos_ref[(blk * tm + r) * TOP_K + kk]
        return pltpu.make_async_copy(y_hbm.at[pl.ds(p, 1), :], gbuf.at[slot, kk, pl.ds(r, 1), :], sem.at[slot])

    def start_gather(blk, slot):
        def body(r, c):
            for kk in range(TOP_K):
                row_copy(blk, slot, r, kk).start()
            return c
        lax.fori_loop(0, tm, body, 0)

    def wait_gather(blk, slot):
        def body(r, c):
            for kk in range(TOP_K):
                row_copy(blk, slot, r, kk).wait()
            return c
        lax.fori_loop(0, tm, body, 0)

    slot = lax.rem(i, 2)

    @pl.when(i == 0)
    def _():
        start_gather(0, 0)

    @pl.when(i + 1 < n)
    def _():
        start_gather(i + 1, 1 - slot)

    wait_gather(i, slot)
    g = gate_ref[...]
    y = gbuf[slot, 0] * g[:, 0:1]
    for kk in range(1, TOP_K):
        y = y + gbuf[slot, kk] * g[:, kk:kk + 1]
    m = mod_ref[...]
    o_ref[...] = x_ref[...] + m[:, 5 * d:6 * d] * _rms(y, g_ref[...])


def _moe_out(pos, y_slots, gates, x3, mods, g_post, s, tm):
    n_tok, d = x3.shape
    d6 = mods.shape[-1]
    per_b = s // tm
    grid_spec = pltpu.PrefetchScalarGridSpec(
        num_scalar_prefetch=1,
        grid=(n_tok // tm,),
        in_specs=[
            pl.BlockSpec(memory_space=pl.ANY),
            pl.BlockSpec((tm, TOP_K), lambda i, p: (i, 0)),
            pl.BlockSpec((tm, d), lambda i, p: (i, 0)),
            pl.BlockSpec((None, None, 1, d6), lambda i, p: (1, i // per_b, 0, 0)),
            pl.BlockSpec((1, d), lambda i, p: (0, 0)),
        ],
        out_specs=pl.BlockSpec((tm, d), lambda i, p: (i, 0)),
        scratch_shapes=[pltpu.VMEM((2, TOP_K, tm, d), F32), pltpu.SemaphoreType.DMA((2,))],
    )
    return pl.pallas_call(
        functools.partial(_moe_out_kernel, d=d),
        grid_spec=grid_spec,
        out_shape=jax.ShapeDtypeStruct((n_tok, d), F32),
        compiler_params=_cparams(("arbitrary",), VMEM_LIMIT),
        name="moe_out",
    )(pos, y_slots, gates, x3, mods, g_post)


def _route(logits, n_experts, mb):
    n_tok = logits.shape[0]
    top_logit, top_idx = lax.top_k(logits, TOP_K)
    gates = jax.nn.softmax(top_logit, axis=-1)
    exp_flat = top_idx.reshape(-1).astype(jnp.int32)
    n_asg = n_tok * TOP_K
    onehot = (exp_flat[:, None] == jnp.arange(n_experts, dtype=jnp.int32)[None, :]).astype(jnp.int32)
    csum = jnp.cumsum(onehot, axis=0)
    counts = csum[-1]
    rank = jnp.sum((csum - onehot) * onehot, axis=1)
    padded = ((counts + mb - 1) // mb) * mb
    pends = jnp.cumsum(padded)
    pstarts = pends - padded
    pos = (pstarts[exp_flat] + rank).astype(jnp.int32)
    cap = (-(-n_asg // mb) + n_experts) * mb
    tok_flat = jnp.arange(n_asg, dtype=jnp.int32) // TOP_K
    slot_tok = jnp.zeros((cap,), jnp.int32).at[pos].set(tok_flat)
    n_blocks = cap // mb
    block_starts = jnp.arange(n_blocks, dtype=jnp.int32) * mb
    block_exp = jnp.minimum(jnp.searchsorted(pends, block_starts, side='right'), n_experts - 1).astype(jnp.int32)
    nused = (pends[-1] // mb).astype(jnp.int32).reshape(1)
    return gates, pos, slot_tok, block_exp, nused


def _rope_tables(s):
    half = ROPE_DIMS // 2
    inv_freq = jnp.power(ROPE_THETA, -jnp.arange(half, dtype=F32) / half)
    ang = jnp.arange(s, dtype=F32)[:, None] * inv_freq[None, :]
    cos, sin = jnp.cos(ang), jnp.sin(ang)
    lane = jnp.arange(LANES) % HEAD_DIM
    idx = lane % half
    is_x1 = lane < half
    is_x2 = (lane >= half) & (lane < ROPE_DIMS)
    c = jnp.where((is_x1 | is_x2)[None, :], cos[:, idx], 1.0)
    sa = jnp.where(is_x2[None, :], sin[:, idx], 0.0)
    sb = jnp.where(is_x1[None, :], -sin[:, idx], 0.0)
    return c.astype(F32), sa.astype(F32), sb.astype(F32)


def kernel(x, c, w_ada, b_ada, norm_g, attn_w_in, fox_b_f, attn_w_out, ffn_w_gate, ffn_w_up, ffn_w_down,
           lru_w_in, lru_conv_w, lru_conv_b, lru_w_a, lru_b_a, lru_w_x, lru_b_x, lru_lambda, lru_w_out,
           moe_w_router, moe_b_router, moe_w_gate, moe_w_up, moe_w_down):
    b, s, d = x.shape
    aw = attn_w_out.shape[1]
    n_experts = moe_w_router.shape[2]
    assert s % ATT_BLOCK == 0 and ATT_BLOCK == MOBA_BLOCK
    tm = min(512, s)

    c_pad = jnp.zeros((8, d), F32).at[:b].set(c)
    mods = _ada(c_pad, w_ada, b_ada)[:, :b].reshape(w_ada.shape[0], b, 1, 6 * d)

    w_in = attn_w_in[0]
    w_qkv = w_in[:, :3 * aw].astype(BF16)
    w_f = jnp.zeros((d, LANES), F32).at[:, :N_FOX_HEADS].set(w_in[:, 3 * aw:]).astype(BF16)
    b_f = jnp.zeros((1, LANES), F32).at[0, :N_FOX_HEADS].set(fox_b_f[0])
    rope_c, rope_sa, rope_sb = _rope_tables(s)
    q, k, v, logf = _l0_in(x, mods, norm_g[0, 0][None], w_qkv, w_f, b_f, rope_c, rope_sa, rope_sb, tm)

    qaux_m = _moba_gate(q, k, N_MOBA_HEADS)
    blk_of_pos = jnp.arange(s, dtype=jnp.int32) // MOBA_BLOCK
    lane = jnp.arange(LANES, dtype=jnp.int32)
    n_blk = s // MOBA_BLOCK
    kaux_row = ((lane[None, :] == blk_of_pos[:, None]) | (lane[None, :] == blk_of_pos[:, None] + n_blk))
    kaux_m = jnp.broadcast_to(kaux_row.astype(BF16)[None, None], (1, 2, s, LANES))
    o_m = _attention(q, k, v, qaux_m, kaux_m, N_MOBA_HEADS, 0)

    qaux_f, kaux_f = _fox_prep(logf, N_FOX_HEADS)
    o_f = _attention(q, k, v, qaux_f, kaux_f, N_FOX_HEADS, N_MOBA_HEADS)

    x1, h2 = _l0_out(o_m, o_f, attn_w_out[0].astype(BF16), x, mods, norm_g[0, 1][None], norm_g[0, 2][None], tm)

    x2, h3 = _ffn(h2, x1, ffn_w_gate[0].astype(BF16), ffn_w_up[0].astype(BF16), ffn_w_down[0].astype(BF16),
                  mods, norm_g[0, 3][None], norm_g[1, 0][None], min(256, s))

    wax = jnp.concatenate([lru_w_a[0], lru_w_x[0]], axis=-1).astype(BF16)
    w_r = jnp.zeros((d, LANES), F32).at[:, :n_experts].set(moe_w_router[0])
    b_r = jnp.full((1, LANES), NEG, F32).at[0, :n_experts].set(moe_b_router[0])
    x3, h4, logits = _lru(h3, x2, lru_w_in[0].astype(BF16), lru_conv_w[0], lru_conv_b[0][None], wax,
                          lru_b_a[0][None], lru_b_x[0][None], lru_lambda[0][None], lru_w_out[0].astype(BF16),
                          mods, norm_g[1, 1][None], norm_g[1, 2][None], w_r, b_r, min(256, s))

    n_tok = b * s
    gates, pos, slot_tok, block_exp, nused = _route(logits.reshape(n_tok, LANES)[:, :n_experts],
                                                    n_experts, MOE_BLOCK_ROWS)
    y_slots = _moe(block_exp, nused, slot_tok, h4.reshape(n_tok, d),
                   moe_w_gate[0].astype(BF16), moe_w_up[0].astype(BF16), moe_w_down[0].astype(BF16))
    out = _moe_out(pos, y_slots, gates, x3.reshape(n_tok, d), mods, norm_g[1, 3][None], s, min(256, s))
    return out.reshape(b, s, d)
```

```python
import functools

import jax
import jax.numpy as jnp
from jax import lax
from jax.experimental import pallas as pl
from jax.experimental.pallas import tpu as pltpu

F32 = jnp.float32
BF16 = jnp.bfloat16
HIGHEST = lax.Precision.HIGHEST

NORM_EPS = 1e-6
HEAD_DIM = 64
N_MOBA_HEADS = 8
N_FOX_HEADS = 8
ROPE_DIMS = 16
ROPE_THETA = 500000.0
MOBA_BLOCK = 256
MOBA_TOPK = 3
CONV_WIDTH = 4
RG_C = 8.0
TOP_K = 2

LANES = 128
ATT_BLOCK = 256
NEG = -1e30
LOG2E = 1.4426950408889634
MOE_BLOCK_ROWS = 512
MOE_FF_CHUNK = 512
VMEM_LIMIT = 56 * 1024 * 1024


def _cparams(sem, vmem=None):
    return pltpu.CompilerParams(dimension_semantics=sem, vmem_limit_bytes=vmem)


def _rms(x, g):
    return x * lax.rsqrt(jnp.mean(x * x, axis=-1, keepdims=True) + NORM_EPS) * g


def _const_spec(shape):
    n = len(shape)
    return pl.BlockSpec(shape, lambda *_: (0,) * n)


def _ada_kernel(c_ref, w_ref, b_ref, o_ref):
    c = c_ref[...]
    cond = c * jax.nn.sigmoid(c)
    o_ref[...] = jnp.dot(cond, w_ref[...], preferred_element_type=F32, precision=HIGHEST) + b_ref[...]


def _ada(c_pad, w_ada, b_ada):
    depth, d, d6 = w_ada.shape
    rows = c_pad.shape[0]
    nj = d6 // d
    return pl.pallas_call(
        _ada_kernel,
        grid=(depth, nj),
        in_specs=[
            pl.BlockSpec((rows, d), lambda l, j: (0, 0)),
            pl.BlockSpec((None, d, d), lambda l, j: (l, 0, j)),
            pl.BlockSpec((None, 1, d), lambda l, j: (l, 0, j)),
        ],
        out_specs=pl.BlockSpec((None, rows, d), lambda l, j: (l, 0, j)),
        out_shape=jax.ShapeDtypeStruct((depth, rows, d6), F32),
        compiler_params=_cparams(("parallel", "parallel")),
        name="ada",
    )(c_pad, w_ada, b_ada.reshape(depth, 1, d6))


def _l0_in_kernel(x_ref, mod_ref, g_ref, w_ref, wf_ref, bf_ref, rc_ref, rsa_ref, rsb_ref,
                  q_ref, k_ref, v_ref, lf_ref, *, d, aw, n_rope_chunks):
    m = mod_ref[...]
    h = _rms(x_ref[...], g_ref[...]) * (1.0 + m[:, d:2 * d]) + m[:, 0:d]
    hb = h.astype(BF16)
    proj = jnp.dot(hb, w_ref[...], preferred_element_type=F32)
    rc, rsa, rsb = rc_ref[...], rsa_ref[...], rsb_ref[...]

    def rope(t):
        return t * rc + pltpu.roll(t, ROPE_DIMS // 2, 1) * rsa + pltpu.roll(t, LANES - ROPE_DIMS // 2, 1) * rsb

    scale = HEAD_DIM ** -0.5 * LOG2E
    for c in range(aw // LANES):
        sl = slice(c * LANES, (c + 1) * LANES)
        qc = proj[:, sl] * scale
        kc = proj[:, aw + c * LANES:aw + (c + 1) * LANES]
        if c < n_rope_chunks:
            qc, kc = rope(qc), rope(kc)
        q_ref[:, sl] = qc.astype(BF16)
        k_ref[:, sl] = kc.astype(BF16)
    v_ref[...] = proj[:, 2 * aw:3 * aw].astype(BF16)
    fl = jnp.dot(hb, wf_ref[...], preferred_element_type=F32) + bf_ref[...]
    lf_ref[...] = jnp.minimum(fl, 0.0) - jnp.log(1.0 + jnp.exp(-jnp.abs(fl)))


def _l0_in(x, mods, g_pre, w_qkv, w_f, b_f, rope_c, rope_sa, rope_sb, tm):
    b, s, d = x.shape
    aw = w_qkv.shape[1] // 3
    d6 = mods.shape[-1]
    tok = lambda bi, i: (bi, i, 0)
    kern = functools.partial(_l0_in_kernel, d=d, aw=aw,
                             n_rope_chunks=N_MOBA_HEADS * HEAD_DIM // LANES)
    return pl.pallas_call(
        kern,
        grid=(b, s // tm),
        in_specs=[
            pl.BlockSpec((None, tm, d), tok),
            pl.BlockSpec((None, None, 1, d6), lambda bi, i: (0, bi, 0, 0)),
            _const_spec((1, d)),
            _const_spec(w_qkv.shape),
            _const_spec(w_f.shape),
            _const_spec((1, LANES)),
            pl.BlockSpec((tm, LANES), lambda bi, i: (i, 0)),
            pl.BlockSpec((tm, LANES), lambda bi, i: (i, 0)),
            pl.BlockSpec((tm, LANES), lambda bi, i: (i, 0)),
        ],
        out_specs=[
            pl.BlockSpec((None, tm, aw), tok),
            pl.BlockSpec((None, tm, aw), tok),
            pl.BlockSpec((None, tm, aw), tok),
            pl.BlockSpec((None, tm, LANES), tok),
        ],
        out_shape=[
            jax.ShapeDtypeStruct((b, s, aw), BF16),
            jax.ShapeDtypeStruct((b, s, aw), BF16),
            jax.ShapeDtypeStruct((b, s, aw), BF16),
            jax.ShapeDtypeStruct((b, s, LANES), F32),
        ],
        compiler_params=_cparams(("parallel", "parallel"), VMEM_LIMIT),
        name="l0_in",
    )(x, mods, g_pre, w_qkv, w_f, b_f, rope_c, rope_sa, rope_sb)


def _fox_prep_kernel(lf_ref, qa_ref, ka_ref, carry_ref, *, n_heads):
    t = lf_ref.shape[0]

    @pl.when(pl.program_id(1) == 0)
    def _():
        carry_ref[...] = jnp.zeros_like(carry_ref)

    row = lax.broadcasted_iota(jnp.int32, (t, t), 0)
    col = lax.broadcasted_iota(jnp.int32, (t, t), 1)
    tri = (col <= row).astype(F32)
    cum = jnp.dot(tri, lf_ref[...], preferred_element_type=F32, precision=HIGHEST) + carry_ref[...]
    carry_ref[...] = cum[t - 1:t, :]
    lane = lax.broadcasted_iota(jnp.int32, (t, LANES), 1)
    for h in range(n_heads):
        c = jnp.broadcast_to(cum[:, h:h + 1], (t, LANES)) * LOG2E
        hi = c.astype(BF16).astype(F32)
        r1 = c - hi
        mid = r1.astype(BF16).astype(F32)
        lo = r1 - mid
        qa = jnp.where(lane == 0, hi, jnp.where(lane == 1, mid, jnp.where(lane == 2, lo,
                       jnp.where(lane < 6, 1.0, 0.0))))
        ka = jnp.where(lane < 3, 1.0, jnp.where(lane == 3, -hi, jnp.where(lane == 4, -mid,
                       jnp.where(lane == 5, -lo, 0.0))))
        qa_ref[h] = qa.astype(BF16)
        ka_ref[h] = ka.astype(BF16)


def _fox_prep(logf, n_heads):
    b, s, _ = logf.shape
    t = ATT_BLOCK
    aux = jax.ShapeDtypeStruct((b, n_heads, s, LANES), BF16)
    aux_spec = pl.BlockSpec((None, n_heads, t, LANES), lambda bi, i: (bi, 0, i, 0))
    return pl.pallas_call(
        functools.partial(_fox_prep_kernel, n_heads=n_heads),
        grid=(b, s // t),
        in_specs=[pl.BlockSpec((None, t, LANES), lambda bi, i: (bi, i, 0))],
        out_specs=[aux_spec, aux_spec],
        out_shape=[aux, aux],
        scratch_shapes=[pltpu.VMEM((1, LANES), F32)],
        compiler_params=_cparams(("parallel", "arbitrary")),
        name="fox_prep",
    )(logf)


def _moba_gate_kernel(q_ref, k_ref, qa_ref, *, n_blk):
    s = k_ref.shape[0]
    t = MOBA_BLOCK
    half = HEAD_DIM
    rn = lax.broadcasted_iota(jnp.int32, (n_blk, s), 0)
    cs = lax.broadcasted_iota(jnp.int32, (n_blk, s), 1)
    avg = jnp.where((cs >= rn * t) & (cs < (rn + 1) * t), 1.0 / t, 0.0).astype(BF16)
    km = jnp.dot(avg, k_ref[...], preferred_element_type=F32)
    lane_k = lax.broadcasted_iota(jnp.int32, (n_blk, LANES), 1)
    km2 = jnp.concatenate([jnp.where(lane_k < half, km, 0.0), jnp.where(lane_k >= half, km, 0.0)], axis=0)
    pad = jnp.zeros((LANES - 2 * n_blk, LANES), F32)
    km2 = jnp.concatenate([km2, pad], axis=0)
    hi = km2.astype(BF16)
    r1 = km2 - hi.astype(F32)
    mid = r1.astype(BF16)
    lo = (r1 - mid.astype(F32)).astype(BF16)
    rhs = jnp.concatenate([hi, mid, lo], axis=0)
    lane = lax.broadcasted_iota(jnp.int32, (t, LANES), 1)

    def body(n, carry):
        rows = pl.ds(pl.multiple_of(n * t, t), t)
        g3 = lax.dot_general(q_ref[rows, :], rhs, (((1,), (1,)), ((), ())), preferred_element_type=F32)
        g = g3[:, 0:LANES] + g3[:, LANES:2 * LANES] + g3[:, 2 * LANES:3 * LANES]
        for hh in range(2):
            local = lane - hh * n_blk
            in_head = (local >= 0) & (local < n_blk)
            gv = jnp.where(in_head & (local < n), g, -jnp.inf)
            keep = local == n
            for _ in range(MOBA_TOPK):
                mx = jnp.max(gv, axis=-1, keepdims=True)
                cand = jnp.where((gv == mx) & (mx > -jnp.inf), lane, LANES)
                pick = lane == jnp.min(cand, axis=-1, keepdims=True)
                keep = keep | pick
                gv = jnp.where(pick, -jnp.inf, gv)
            qa_ref[hh, rows, :] = jnp.where(in_head & jnp.logical_not(keep), NEG, 0.0).astype(BF16)
        return carry

    lax.fori_loop(0, n_blk, body, 0)


def _moba_gate(q, k, n_heads):
    b, s, _ = q.shape
    n_blk = s // MOBA_BLOCK
    assert 2 * n_blk <= LANES
    return pl.pallas_call(
        functools.partial(_moba_gate_kernel, n_blk=n_blk),
        grid=(b, n_heads // 2),
        in_specs=[
            pl.BlockSpec((None, s, LANES), lambda bi, p: (bi, 0, p)),
            pl.BlockSpec((None, s, LANES), lambda bi, p: (bi, 0, p)),
        ],
        out_specs=pl.BlockSpec((None, 2, s, LANES), lambda bi, p: (bi, p, 0, 0)),
        out_shape=jax.ShapeDtypeStruct((b, n_heads, s, LANES), BF16),
        compiler_params=_cparams(("parallel", "parallel")),
        name="moba_gate",
    )(q, k)


def _attn_kernel(q_ref, k_ref, v_ref, qa_ref, ka_ref, o_ref):
    t = ATT_BLOCK
    i = pl.program_id(2)
    lane = lax.broadcasted_iota(jnp.int32, (t, LANES), 1)
    row = lax.broadcasted_iota(jnp.int32, (t, t), 0)
    col = lax.broadcasted_iota(jnp.int32, (t, t), 1)
    q = q_ref[...]
    nt = (((1,), (1,)), ((), ()))
    qas = []
    for hh in range(2):
        in_head = (lane < HEAD_DIM) if hh == 0 else (lane >= HEAD_DIM)
        qas.append(jnp.concatenate([jnp.where(in_head, q, jnp.zeros_like(q)), qa_ref[hh]], axis=1))

    def update(state, j0, nk, diag):
        rows = pl.ds(pl.multiple_of(j0 * t, t), nk * t)
        k = k_ref[rows, :]
        v = v_ref[rows, :]
        lane_v = lax.broadcasted_iota(jnp.int32, (nk * t, LANES), 1)
        new = []
        for hh in range(2):
            kk = jnp.concatenate([k, ka_ref[hh, rows, :]], axis=1)
            sc = lax.dot_general(qas[hh], kk, nt, preferred_element_type=F32)
            if diag:
                sc = jnp.where(col <= row, sc, NEG)
            mx = jnp.max(sc, axis=-1, keepdims=True)
            m_new = mx if state is None else jnp.maximum(state[hh][0], mx)
            p = jnp.exp2(sc - m_new).astype(BF16)
            in_head_v = (lane_v < HEAD_DIM) if hh == 0 else (lane_v >= HEAD_DIM)
            vh = jnp.where(in_head_v, v, jnp.ones_like(v))
            pv = jnp.dot(p, vh, preferred_element_type=F32)
            if state is None:
                acc = pv
            else:
                acc = jnp.exp2(state[hh][0] - m_new) * state[hh][1] + pv
            new.append((m_new, acc))
        return tuple(new)

    state = update(None, i, 1, True)
    n2 = jnp.right_shift(i, 1)
    state = lax.fori_loop(0, n2, lambda jj, st: update(st, 2 * jj, 2, False), state)
    state = lax.fori_loop(0, i - 2 * n2, lambda _, st: update(st, i - 1, 1, False), state)
    outs = [acc / pltpu.roll(acc, HEAD_DIM, 1) for _, acc in state]
    o_ref[...] = jnp.where(lane < HEAD_DIM, outs[0], outs[1]).astype(o_ref.dtype)


def _attention(q, k, v, qaux, kaux, n_heads, head_off):
    b, s, _ = q.shape
    t = ATT_BLOCK
    po = head_off // 2
    kb, kh = kaux.shape[0], kaux.shape[1]
    ka_map = (lambda bi, p, i: (bi, p, 0, 0)) if kb == b and kh == n_heads else (lambda bi, p, i: (0, 0, 0, 0))
    return pl.pallas_call(
        _attn_kernel,
        grid=(b, n_heads // 2, s // t),
        in_specs=[
            pl.BlockSpec((None, t, LANES), lambda bi, p, i: (bi, i, p + po)),
            pl.BlockSpec((None, s, LANES), lambda bi, p, i: (bi, 0, p + po)),
            pl.BlockSpec((None, s, LANES), lambda bi, p, i: (bi, 0, p + po)),
            pl.BlockSpec((None, 2, t, LANES), lambda bi, p, i: (bi, p, i, 0)),
            pl.BlockSpec((None, 2, s, LANES), ka_map),
        ],
        out_specs=pl.BlockSpec((None, t, LANES), lambda bi, p, i: (bi, i, p)),
        out_shape=jax.ShapeDtypeStruct((b, s, n_heads * HEAD_DIM), BF16),
        compiler_params=_cparams(("parallel", "parallel", "arbitrary")),
        name="attn",
    )(q, k, v, qaux, kaux)


def _l0_out_kernel(om_ref, of_ref, w_ref, x_ref, mod_ref, g1_ref, g2_ref, x1_ref, h2_ref, *, d):
    m = mod_ref[...]
    hw = om_ref.shape[1]
    y = (jnp.dot(om_ref[...], w_ref[0:hw, :], preferred_element_type=F32)
         + jnp.dot(of_ref[...], w_ref[hw:, :], preferred_element_type=F32))
    x1 = x_ref[...] + m[:, 2 * d:3 * d] * _rms(y, g1_ref[...])
    x1_ref[...] = x1
    h2_ref[...] = (_rms(x1, g2_ref[...]) * (1.0 + m[:, 4 * d:5 * d]) + m[:, 3 * d:4 * d]).astype(BF16)


def _l0_out(o_m, o_f, w_out, x, mods, g_post, g_pre2, tm):
    b, s, d = x.shape
    d6 = mods.shape[-1]
    tok = lambda bi, i: (bi, i, 0)
    return pl.pallas_call(
        functools.partial(_l0_out_kernel, d=d),
        grid=(b, s // tm),
        in_specs=[
            pl.BlockSpec((None, tm, o_m.shape[2]), tok),
            pl.BlockSpec((None, tm, o_f.shape[2]), tok),
            _const_spec(w_out.shape),
            pl.BlockSpec((None, tm, d), tok),
            pl.BlockSpec((None, None, 1, d6), lambda bi, i: (0, bi, 0, 0)),
            _const_spec((1, d)),
            _const_spec((1, d)),
        ],
        out_specs=[pl.BlockSpec((None, tm, d), tok), pl.BlockSpec((None, tm, d), tok)],
        out_shape=[jax.ShapeDtypeStruct((b, s, d), F32), jax.ShapeDtypeStruct((b, s, d), BF16)],
        compiler_params=_cparams(("parallel", "parallel"), VMEM_LIMIT),
        name="l0_out",
    )(o_m, o_f, w_out, x, mods, g_post, g_pre2)


def _ffn_kernel(h_ref, x_ref, wg_ref, wu_ref, wd_ref, mod0_ref, mod1_ref, g1_ref, g2_ref,
                x2_ref, h3_ref, *, d):
    h = h_ref[...]
    a = jnp.dot(h, wg_ref[...], preferred_element_type=F32)
    u = jnp.dot(h, wu_ref[...], preferred_element_type=F32)
    act = (a * jax.nn.sigmoid(a) * u).astype(BF16)
    y = jnp.dot(act, wd_ref[...], preferred_element_type=F32)
    m0 = mod0_ref[...]
    m1 = mod1_ref[...]
    x2 = x_ref[...] + m0[:, 5 * d:6 * d] * _rms(y, g1_ref[...])
    x2_ref[...] = x2
    h3_ref[...] = (_rms(x2, g2_ref[...]) * (1.0 + m1[:, d:2 * d]) + m1[:, 0:d]).astype(BF16)


def _ffn(h2, x1, w_gate, w_up, w_down, mods, g_post, g_pre_next, tm):
    b, s, d = x1.shape
    d6 = mods.shape[-1]
    tok = lambda bi, i: (bi, i, 0)
    single = dict(pipeline_mode=pl.Buffered(1))
    return pl.pallas_call(
        functools.partial(_ffn_kernel, d=d),
        grid=(b, s // tm),
        in_specs=[
            pl.BlockSpec((None, tm, d), tok),
            pl.BlockSpec((None, tm, d), tok),
            pl.BlockSpec(w_gate.shape, lambda bi, i: (0, 0), **single),
            pl.BlockSpec(w_up.shape, lambda bi, i: (0, 0), **single),
            pl.BlockSpec(w_down.shape, lambda bi, i: (0, 0), **single),
            pl.BlockSpec((None, None, 1, d6), lambda bi, i: (0, bi, 0, 0)),
            pl.BlockSpec((None, None, 1, d6), lambda bi, i: (1, bi, 0, 0)),
            _const_spec((1, d)),
            _const_spec((1, d)),
        ],
        out_specs=[pl.BlockSpec((None, tm, d), tok), pl.BlockSpec((None, tm, d), tok)],
        out_shape=[jax.ShapeDtypeStruct((b, s, d), F32), jax.ShapeDtypeStruct((b, s, d), BF16)],
        compiler_params=_cparams(("parallel", "parallel"), VMEM_LIMIT),
        name="ffn",
    )(h2, x1, w_gate, w_up, w_down, mods, mods, g_post, g_pre_next)


def _lru_kernel(h_ref, x_ref, win_ref, cw_ref, cb_ref, wax_ref, ba_ref, bx_ref, lam_ref, wout_ref,
                mod_ref, g1_ref, g2_ref, wr_ref, br_ref,
                x3_ref, h4_ref, lg_ref, ubuf_ref, hc_ref, *, d, dr, n_rnn_blocks):
    tm = h_ref.shape[0]
    tail = 8

    @pl.when(pl.program_id(1) == 0)
    def _():
        ubuf_ref[0:tail, :] = jnp.zeros((tail, dr), F32)
        hc_ref[...] = jnp.zeros_like(hc_ref)

    proj = jnp.dot(h_ref[...], win_ref[...], preferred_element_type=F32)
    gate_branch = proj[:, :dr]
    ubuf_ref[tail:tail + tm, :] = proj[:, dr:]
    conv = jnp.broadcast_to(cb_ref[...], (tm, dr))
    for j in range(CONV_WIDTH):
        off = tail - (CONV_WIDTH - 1) + j
        conv = conv + ubuf_ref[off:off + tm, :] * cw_ref[j:j + 1, :]
    ubuf_ref[0:tail, :] = ubuf_ref[tm:tm + tail, :]

    w = dr // n_rnn_blocks
    rs, is_ = [], []
    for n in range(n_rnn_blocks):
        cbk = conv[:, n * w:(n + 1) * w].astype(BF16)
        ra = jnp.dot(cbk, wax_ref[n], preferred_element_type=F32)
        rs.append(ra[:, :w])
        is_.append(ra[:, w:])
    r = jax.nn.sigmoid(jnp.concatenate(rs, axis=1) + ba_ref[...])
    ig = jax.nn.sigmoid(jnp.concatenate(is_, axis=1) + bx_ref[...])
    nl = -lam_ref[...]
    softplus = jnp.maximum(nl, 0.0) + jnp.log(1.0 + jnp.exp(-jnp.abs(nl)))
    log_a = (-RG_C * r) * softplus
    a = jnp.exp(log_a)
    xin = jnp.sqrt(1.0 - jnp.exp(2.0 * log_a)) * (ig * conv)

    rowi = lax.broadcasted_iota(jnp.int32, (tm, dr), 0)
    sa, sx = a, xin
    dist = 1
    while dist < tm:
        keep = rowi >= dist
        xs = jnp.where(keep, pltpu.roll(sx, dist, 0), 0.0)
        as_ = jnp.where(keep, pltpu.roll(sa, dist, 0), 1.0)
        sx = sx + sa * xs
        sa = sa * as_
        dist *= 2
    hs = sx + sa * hc_ref[...]
    hc_ref[...] = hs[tm - 1:tm, :]

    y = (jax.nn.gelu(gate_branch, approximate=True) * hs).astype(BF16)
    out = jnp.dot(y, wout_ref[...], preferred_element_type=F32)
    m = mod_ref[...]
    x3 = x_ref[...] + m[:, 2 * d:3 * d] * _rms(out, g1_ref[...])
    x3_ref[...] = x3
    h4 = _rms(x3, g2_ref[...]) * (1.0 + m[:, 4 * d:5 * d]) + m[:, 3 * d:4 * d]
    h4_ref[...] = h4
    lg_ref[...] = jnp.dot(h4, wr_ref[...], preferred_element_type=F32, precision=HIGHEST) + br_ref[...]


def _lru(h3, x2, w_in, conv_w, conv_b, wax, b_a, b_x, lam, w_out, mods, g_post, g_pre2, w_r, b_r, tm):
    b, s, d = x2.shape
    dr = w_out.shape[0]
    d6 = mods.shape[-1]
    nb = wax.shape[0]
    tok = lambda bi, i: (bi, i, 0)
    return pl.pallas_call(
        functools.partial(_lru_kernel, d=d, dr=dr, n_rnn_blocks=nb),
        grid=(b, s // tm),
        in_specs=[
            pl.BlockSpec((None, tm, d), tok),
            pl.BlockSpec((None, tm, d), tok),
            _const_spec(w_in.shape),
            _const_spec(conv_w.shape),
            _const_spec((1, dr)),
            _const_spec(wax.shape),
            _const_spec((1, dr)),
            _const_spec((1, dr)),
            _const_spec((1, dr)),
            _const_spec(w_out.shape),
            pl.BlockSpec((None, None, 1, d6), lambda bi, i: (1, bi, 0, 0)),
            _const_spec((1, d)),
            _const_spec((1, d)),
            _const_spec(w_r.shape),
            _const_spec((1, LANES)),
        ],
        out_specs=[pl.BlockSpec((None, tm, d), tok), pl.BlockSpec((None, tm, d), tok),
                   pl.BlockSpec((None, tm, LANES), tok)],
        out_shape=[jax.ShapeDtypeStruct((b, s, d), F32), jax.ShapeDtypeStruct((b, s, d), F32),
                   jax.ShapeDtypeStruct((b, s, LANES), F32)],
        scratch_shapes=[pltpu.VMEM((tm + 8, dr), F32), pltpu.VMEM((1, dr), F32)],
        compiler_params=_cparams(("parallel", "arbitrary"), VMEM_LIMIT),
        name="lru",
    )(h3, x2, w_in, conv_w, conv_b, wax, b_a, b_x, lam, w_out, mods, g_post, g_pre2, w_r, b_r)


def _moe_kernel(bexp_ref, nused_ref, tok_ref, h_hbm, wg_ref, wu_ref, wd_ref, y_ref, xbuf, sem, *, n_blocks):
    del bexp_ref
    mb = y_ref.shape[0]
    i = pl.program_id(0)
    nused = nused_ref[0]

    def row_copy(blk, slot, r):
        tok = tok_ref[blk * mb + r]
        return pltpu.make_async_copy(h_hbm.at[pl.ds(tok, 1), :], xbuf.at[slot, pl.ds(r, 1), :], sem.at[slot])

    def start_gather(blk, slot):
        def body(r, c):
            row_copy(blk, slot, r).start()
            return c
        lax.fori_loop(0, mb, body, 0, unroll=8)

    def wait_gather(slot):
        pltpu.make_async_copy(h_hbm.at[pl.ds(0, mb), :], xbuf.at[slot], sem.at[slot]).wait()

    slot = lax.rem(i, 2)

    @pl.when((i == 0) & (nused > 0))
    def _():
        start_gather(0, 0)

    @pl.when(i + 1 < nused)
    def _():
        start_gather(i + 1, 1 - slot)

    @pl.when(i < nused)
    def _():
        wait_gather(slot)
        x = xbuf[slot].astype(BF16)
        dff = wg_ref.shape[1]
        acc = jnp.zeros(y_ref.shape, F32)
        for c in range(dff // MOE_FF_CHUNK):
            cs = slice(c * MOE_FF_CHUNK, (c + 1) * MOE_FF_CHUNK)
            a = jnp.dot(x, wg_ref[:, cs], preferred_element_type=F32)
            u = jnp.dot(x, wu_ref[:, cs], preferred_element_type=F32)
            act = (a * jax.nn.sigmoid(a) * u).astype(BF16)
            acc = acc + jnp.dot(act, wd_ref[cs, :], preferred_element_type=F32)
        y_ref[...] = acc

    @pl.when(i >= nused)
    def _():
        y_ref[...] = jnp.zeros_like(y_ref)


def _moe(block_exp, nused, slot_tok, h4, w_gate, w_up, w_down):
    n_tok, d = h4.shape
    mb = MOE_BLOCK_ROWS
    cap = slot_tok.shape[0]
    n_blocks = cap // mb
    e, _, dff = w_gate.shape
    single = dict(pipeline_mode=pl.Buffered(1))
    grid_spec = pltpu.PrefetchScalarGridSpec(
        num_scalar_prefetch=3,
        grid=(n_blocks,),
        in_specs=[
            pl.BlockSpec(memory_space=pl.ANY),
            pl.BlockSpec((None, d, dff), lambda i, be, nu, st: (be[i], 0, 0), **single),
            pl.BlockSpec((None, d, dff), lambda i, be, nu, st: (be[i], 0, 0), **single),
            pl.BlockSpec((None, dff, d), lambda i, be, nu, st: (be[i], 0, 0), **single),
        ],
        out_specs=pl.BlockSpec((mb, d), lambda i, be, nu, st: (i, 0)),
        scratch_shapes=[pltpu.VMEM((2, mb, d), F32), pltpu.SemaphoreType.DMA((2,))],
    )
    return pl.pallas_call(
        functools.partial(_moe_kernel, n_blocks=n_blocks),
        grid_spec=grid_spec,
        out_shape=jax.ShapeDtypeStruct((cap, d), F32),
        compiler_params=_cparams(("arbitrary",), VMEM_LIMIT),
        name="moe",
    )(block_exp, nused, slot_tok, h4, w_gate, w_up, w_down)


def _moe_out_kernel(pos_ref, y_hbm, gate_ref, x_ref, mod_ref, g_ref, o_ref, gbuf, sem, *, d):
    tm = x_ref.shape[0]
    i = pl.program_id(0)
    n = pl.num_programs(0)

    def row_copy(blk, slot, r, kk):
        p = pos_ref[(blk * tm + r) * TOP_K + kk]
        return pltpu.make_async_copy(y_hbm.at[pl.ds(p, 1), :], gbuf.at[slot, kk, pl.ds(r, 1), :], sem.at[slot])

    def start_gather(blk, slot):
        def body(r, c):
            for kk in range(TOP_K):
                row_copy(blk, slot, r, kk).start()
            return c
        lax.fori_loop(0, tm, body, 0, unroll=8)

    def wait_gather(slot):
        for kk in range(TOP_K):
            pltpu.make_async_copy(y_hbm.at[pl.ds(0, tm), :], gbuf.at[slot, kk], sem.at[slot]).wait()

    slot = lax.rem(i, 2)

    @pl.when(i == 0)
    def _():
        start_gather(0, 0)

    @pl.when(i + 1 < n)
    def _():
        start_gather(i + 1, 1 - slot)

    wait_gather(slot)
    g = gate_ref[...]
    y = gbuf[slot, 0] * g[:, 0:1]
    for kk in range(1, TOP_K):
        y = y + gbuf[slot, kk] * g[:, kk:kk + 1]
    m = mod_ref[...]
    o_ref[...] = x_ref[...] + m[:, 5 * d:6 * d] * _rms(y, g_ref[...])


def _moe_out(pos, y_slots, gates, x3, mods, g_post, s, tm):
    n_tok, d = x3.shape
    d6 = mods.shape[-1]
    per_b = s // tm
    grid_spec = pltpu.PrefetchScalarGridSpec(
        num_scalar_prefetch=1,
        grid=(n_tok // tm,),
        in_specs=[
            pl.BlockSpec(memory_space=pl.ANY),
            pl.BlockSpec((tm, TOP_K), lambda i, p: (i, 0)),
            pl.BlockSpec((tm, d), lambda i, p: (i, 0)),
            pl.BlockSpec((None, None, 1, d6), lambda i, p: (1, i // per_b, 0, 0)),
            pl.BlockSpec((1, d), lambda i, p: (0, 0)),
        ],
        out_specs=pl.BlockSpec((tm, d), lambda i, p: (i, 0)),
        scratch_shapes=[pltpu.VMEM((2, TOP_K, tm, d), F32), pltpu.SemaphoreType.DMA((2,))],
    )
    return pl.pallas_call(
        functools.partial(_moe_out_kernel, d=d),
        grid_spec=grid_spec,
        out_shape=jax.ShapeDtypeStruct((n_tok, d), F32),
        compiler_params=_cparams(("arbitrary",), VMEM_LIMIT),
        name="moe_out",
    )(pos, y_slots, gates, x3, mods, g_post)


def _route(logits, n_experts, mb):
    n_tok = logits.shape[0]
    top_logit, top_idx = lax.top_k(logits, TOP_K)
    gates = jax.nn.softmax(top_logit, axis=-1)
    exp_flat = top_idx.reshape(-1).astype(jnp.int32)
    n_asg = n_tok * TOP_K
    onehot = (exp_flat[:, None] == jnp.arange(n_experts, dtype=jnp.int32)[None, :]).astype(jnp.int32)
    csum = jnp.cumsum(onehot, axis=0)
    counts = csum[-1]
    rank = jnp.sum((csum - onehot) * onehot, axis=1)
    padded = ((counts + mb - 1) // mb) * mb
    pends = jnp.cumsum(padded)
    pstarts = pends - padded
    pos = (pstarts[exp_flat] + rank).astype(jnp.int32)
    cap = (-(-n_asg // mb) + n_experts) * mb
    tok_flat = jnp.arange(n_asg, dtype=jnp.int32) // TOP_K
    slot_tok = jnp.zeros((cap,), jnp.int32).at[pos].set(tok_flat)
    n_blocks = cap // mb
    block_starts = jnp.arange(n_blocks, dtype=jnp.int32) * mb
    block_exp = jnp.minimum(jnp.searchsorted(pends, block_starts, side='right'), n_experts - 1).astype(jnp.int32)
    nused = (pends[-1] // mb).astype(jnp.int32).reshape(1)
    return gates, pos, slot_tok, block_exp, nused


def _rope_tables(s):
    half = ROPE_DIMS // 2
    inv_freq = jnp.power(ROPE_THETA, -jnp.arange(half, dtype=F32) / half)
    ang = jnp.arange(s, dtype=F32)[:, None] * inv_freq[None, :]
    cos, sin = jnp.cos(ang), jnp.sin(ang)
    lane = jnp.arange(LANES) % HEAD_DIM
    idx = lane % half
    is_x1 = lane < half
    is_x2 = (lane >= half) & (lane < ROPE_DIMS)
    c = jnp.where((is_x1 | is_x2)[None, :], cos[:, idx], 1.0)
    sa = jnp.where(is_x2[None, :], sin[:, idx], 0.0)
    sb = jnp.where(is_x1[None, :], -sin[:, idx], 0.0)
    return c.astype(F32), sa.astype(F32), sb.astype(F32)


def kernel(x, c, w_ada, b_ada, norm_g, attn_w_in, fox_b_f, attn_w_out, ffn_w_gate, ffn_w_up, ffn_w_down,
           lru_w_in, lru_conv_w, lru_conv_b, lru_w_a, lru_b_a, lru_w_x, lru_b_x, lru_lambda, lru_w_out,
           moe_w_router, moe_b_router, moe_w_gate, moe_w_up, moe_w_down):
    b, s, d = x.shape
    aw = attn_w_out.shape[1]
    n_experts = moe_w_router.shape[2]
    assert s % ATT_BLOCK == 0 and ATT_BLOCK == MOBA_BLOCK
    tm = min(512, s)

    c_pad = jnp.zeros((8, d), F32).at[:b].set(c)
    mods = _ada(c_pad, w_ada, b_ada)[:, :b].reshape(w_ada.shape[0], b, 1, 6 * d)

    w_in = attn_w_in[0]
    w_qkv = w_in[:, :3 * aw].astype(BF16)
    w_f = jnp.zeros((d, LANES), F32).at[:, :N_FOX_HEADS].set(w_in[:, 3 * aw:]).astype(BF16)
    b_f = jnp.zeros((1, LANES), F32).at[0, :N_FOX_HEADS].set(fox_b_f[0])
    rope_c, rope_sa, rope_sb = _rope_tables(s)
    q, k, v, logf = _l0_in(x, mods, norm_g[0, 0][None], w_qkv, w_f, b_f, rope_c, rope_sa, rope_sb, tm)

    qaux_m = _moba_gate(q, k, N_MOBA_HEADS)
    blk_of_pos = jnp.arange(s, dtype=jnp.int32) // MOBA_BLOCK
    lane = jnp.arange(LANES, dtype=jnp.int32)
    n_blk = s // MOBA_BLOCK
    kaux_row = ((lane[None, :] == blk_of_pos[:, None]) | (lane[None, :] == blk_of_pos[:, None] + n_blk))
    kaux_m = jnp.broadcast_to(kaux_row.astype(BF16)[None, None], (1, 2, s, LANES))
    o_m = _attention(q, k, v, qaux_m, kaux_m, N_MOBA_HEADS, 0)

    qaux_f, kaux_f = _fox_prep(logf, N_FOX_HEADS)
    o_f = _attention(q, k, v, qaux_f, kaux_f, N_FOX_HEADS, N_MOBA_HEADS)

    x1, h2 = _l0_out(o_m, o_f, attn_w_out[0].astype(BF16), x, mods, norm_g[0, 1][None], norm_g[0, 2][None], tm)

    x2, h3 = _ffn(h2, x1, ffn_w_gate[0].astype(BF16), ffn_w_up[0].astype(BF16), ffn_w_down[0].astype(BF16),
                  mods, norm_g[0, 3][None], norm_g[1, 0][None], min(256, s))

    wax = jnp.concatenate([lru_w_a[0], lru_w_x[0]], axis=-1).astype(BF16)
    w_r = jnp.zeros((d, LANES), F32).at[:, :n_experts].set(moe_w_router[0])
    b_r = jnp.full((1, LANES), NEG, F32).at[0, :n_experts].set(moe_b_router[0])
    x3, h4, logits = _lru(h3, x2, lru_w_in[0].astype(BF16), lru_conv_w[0], lru_conv_b[0][None], wax,
                          lru_b_a[0][None], lru_b_x[0][None], lru_lambda[0][None], lru_w_out[0].astype(BF16),
                          mods, norm_g[1, 1][None], norm_g[1, 2][None], w_r, b_r, min(256, s))

    n_tok = b * s
    gates, pos, slot_tok, block_exp, nused = _route(logits.reshape(n_tok, LANES)[:, :n_experts],
                                                    n_experts, MOE_BLOCK_ROWS)
    y_slots = _moe(block_exp, nused, slot_tok, h4.reshape(n_tok, d),
                   moe_w_gate[0].astype(BF16), moe_w_up[0].astype(BF16), moe_w_down[0].astype(BF16))
    out = _moe_out(pos, y_slots, gates, x3.reshape(n_tok, d), mods, norm_g[1, 3][None], s, min(256, s))
    return out.reshape(b, s, d)
```

```python
import functools

import jax
import jax.numpy as jnp
from jax import lax
from jax.experimental import pallas as pl
from jax.experimental.pallas import tpu as pltpu

F32 = jnp.float32
BF16 = jnp.bfloat16
HIGHEST = lax.Precision.HIGHEST

NORM_EPS = 1e-6
HEAD_DIM = 64
N_MOBA_HEADS = 8
N_FOX_HEADS = 8
ROPE_DIMS = 16
ROPE_THETA = 500000.0
MOBA_BLOCK = 256
MOBA_TOPK = 3
CONV_WIDTH = 4
RG_C = 8.0
TOP_K = 2

LANES = 128
SUBLANES = 8
ATT_BLOCK = 256
ATT_STEP = 512
ATT_SUB = 256
ATT_LOOKAHEAD = 2
NEG = -1e30
LOG2E = 1.4426950408889634
MOE_BLOCK_ROWS = 512
MOE_FF_CHUNK = 512
VMEM_LIMIT = 56 * 1024 * 1024


def _cparams(sem, vmem=None):
    return pltpu.CompilerParams(dimension_semantics=sem, vmem_limit_bytes=vmem)


def _rms(x, g):
    return x * lax.rsqrt(jnp.mean(x * x, axis=-1, keepdims=True) + NORM_EPS) * g


def _const_spec(shape):
    n = len(shape)
    return pl.BlockSpec(shape, lambda *_: (0,) * n)


def _ada_kernel(c_ref, w_ref, b_ref, o_ref):
    c = c_ref[...]
    cond = c * jax.nn.sigmoid(c)
    o_ref[...] = jnp.dot(cond, w_ref[...], preferred_element_type=F32, precision=HIGHEST) + b_ref[...]


def _ada(c_pad, w_ada, b_ada):
    depth, d, d6 = w_ada.shape
    rows = c_pad.shape[0]
    nj = d6 // d
    return pl.pallas_call(
        _ada_kernel,
        grid=(depth, nj),
        in_specs=[
            pl.BlockSpec((rows, d), lambda l, j: (0, 0)),
            pl.BlockSpec((None, d, d), lambda l, j: (l, 0, j)),
            pl.BlockSpec((None, 1, d), lambda l, j: (l, 0, j)),
        ],
        out_specs=pl.BlockSpec((None, rows, d), lambda l, j: (l, 0, j)),
        out_shape=jax.ShapeDtypeStruct((depth, rows, d6), F32),
        compiler_params=_cparams(("parallel", "parallel")),
        name="ada",
    )(c_pad, w_ada, b_ada.reshape(depth, 1, d6))


def _l0_in_kernel(x_ref, mod_ref, g_ref, w_ref, wf_ref, bf_ref, rc_ref, rsa_ref, rsb_ref,
                  q_ref, k_ref, v_ref, lf_ref, *, d, aw, n_rope_chunks):
    m = mod_ref[...]
    h = _rms(x_ref[...], g_ref[...]) * (1.0 + m[:, d:2 * d]) + m[:, 0:d]
    hb = h.astype(BF16)
    proj = jnp.dot(hb, w_ref[...], preferred_element_type=F32)
    rc, rsa, rsb = rc_ref[...], rsa_ref[...], rsb_ref[...]

    def rope(t):
        return t * rc + pltpu.roll(t, ROPE_DIMS // 2, 1) * rsa + pltpu.roll(t, LANES - ROPE_DIMS // 2, 1) * rsb

    scale = HEAD_DIM ** -0.5 * LOG2E
    for c in range(aw // LANES):
        sl = slice(c * LANES, (c + 1) * LANES)
        qc = proj[:, sl] * scale
        kc = proj[:, aw + c * LANES:aw + (c + 1) * LANES]
        if c < n_rope_chunks:
            qc, kc = rope(qc), rope(kc)
        q_ref[:, sl] = qc.astype(BF16)
        k_ref[:, sl] = kc.astype(BF16)
    v_ref[...] = proj[:, 2 * aw:3 * aw].astype(BF16)
    fl = jnp.dot(hb, wf_ref[...], preferred_element_type=F32) + bf_ref[...]
    lf_ref[...] = jnp.minimum(fl, 0.0) - jnp.log(1.0 + jnp.exp(-jnp.abs(fl)))


def _l0_in(x, mods, g_pre, w_qkv, w_f, b_f, rope_c, rope_sa, rope_sb, tm):
    b, s, d = x.shape
    aw = w_qkv.shape[1] // 3
    d6 = mods.shape[-1]
    tok = lambda bi, i: (bi, i, 0)
    kern = functools.partial(_l0_in_kernel, d=d, aw=aw,
                             n_rope_chunks=N_MOBA_HEADS * HEAD_DIM // LANES)
    return pl.pallas_call(
        kern,
        grid=(b, s // tm),
        in_specs=[
            pl.BlockSpec((None, tm, d), tok),
            pl.BlockSpec((None, None, 1, d6), lambda bi, i: (0, bi, 0, 0)),
            _const_spec((1, d)),
            _const_spec(w_qkv.shape),
            _const_spec(w_f.shape),
            _const_spec((1, LANES)),
            pl.BlockSpec((tm, LANES), lambda bi, i: (i, 0)),
            pl.BlockSpec((tm, LANES), lambda bi, i: (i, 0)),
            pl.BlockSpec((tm, LANES), lambda bi, i: (i, 0)),
        ],
        out_specs=[
            pl.BlockSpec((None, tm, aw), tok),
            pl.BlockSpec((None, tm, aw), tok),
            pl.BlockSpec((None, tm, aw), tok),
            pl.BlockSpec((None, tm, LANES), tok),
        ],
        out_shape=[
            jax.ShapeDtypeStruct((b, s, aw), BF16),
            jax.ShapeDtypeStruct((b, s, aw), BF16),
            jax.ShapeDtypeStruct((b, s, aw), BF16),
            jax.ShapeDtypeStruct((b, s, LANES), F32),
        ],
        compiler_params=_cparams(("parallel", "parallel"), VMEM_LIMIT),
        name="l0_in",
    )(x, mods, g_pre, w_qkv, w_f, b_f, rope_c, rope_sa, rope_sb)


def _fox_prep_kernel(lf_ref, qa_ref, ka_ref, carry_ref, *, n_heads):
    t = lf_ref.shape[0]

    @pl.when(pl.program_id(1) == 0)
    def _():
        carry_ref[...] = jnp.zeros_like(carry_ref)

    row = lax.broadcasted_iota(jnp.int32, (t, t), 0)
    col = lax.broadcasted_iota(jnp.int32, (t, t), 1)
    tri = (col <= row).astype(F32)
    cum = jnp.dot(tri, lf_ref[...], preferred_element_type=F32, precision=HIGHEST) + carry_ref[...]
    carry_ref[...] = cum[t - 1:t, :]
    lane = lax.broadcasted_iota(jnp.int32, (t, LANES), 1)
    for h in range(n_heads):
        c = jnp.broadcast_to(cum[:, h:h + 1], (t, LANES)) * LOG2E
        hi = c.astype(BF16).astype(F32)
        r1 = c - hi
        mid = r1.astype(BF16).astype(F32)
        lo = r1 - mid
        qa = jnp.where(lane == 0, hi, jnp.where(lane == 1, mid, jnp.where(lane == 2, lo,
                       jnp.where(lane < 6, 1.0, 0.0))))
        ka = jnp.where(lane < 3, 1.0, jnp.where(lane == 3, -hi, jnp.where(lane == 4, -mid,
                       jnp.where(lane == 5, -lo, 0.0))))
        qa_ref[h] = qa.astype(BF16)
        ka_ref[h] = ka.astype(BF16)


def _fox_prep(logf, n_heads):
    b, s, _ = logf.shape
    t = ATT_BLOCK
    aux = jax.ShapeDtypeStruct((b, n_heads, s, LANES), BF16)
    aux_spec = pl.BlockSpec((None, n_heads, t, LANES), lambda bi, i: (bi, 0, i, 0))
    return pl.pallas_call(
        functools.partial(_fox_prep_kernel, n_heads=n_heads),
        grid=(b, s // t),
        in_specs=[pl.BlockSpec((None, t, LANES), lambda bi, i: (bi, i, 0))],
        out_specs=[aux_spec, aux_spec],
        out_shape=[aux, aux],
        scratch_shapes=[pltpu.VMEM((1, LANES), F32)],
        compiler_params=_cparams(("parallel", "arbitrary")),
        name="fox_prep",
    )(logf)


def _moba_gate_kernel(q_ref, k_ref, bt_ref, *, n_blk):
    s = k_ref.shape[0]
    t = MOBA_BLOCK
    rowi = lax.broadcasted_iota(jnp.int32, (n_blk, s), 0)
    cs = lax.broadcasted_iota(jnp.int32, (n_blk, s), 1)
    own = (cs >= rowi * t) & (cs < (rowi + 1) * t)
    km = jnp.dot(jnp.where(own, 1.0 / t, 0.0).astype(BF16), k_ref[...], preferred_element_type=F32)
    lane_k = lax.broadcasted_iota(jnp.int32, (n_blk, LANES), 1)
    km2 = jnp.concatenate([jnp.where(lane_k < HEAD_DIM, km, 0.0), jnp.where(lane_k >= HEAD_DIM, km, 0.0)], axis=0)
    hi = km2.astype(BF16)
    r1 = km2 - hi.astype(F32)
    mid = r1.astype(BF16)
    lo = (r1 - mid.astype(F32)).astype(BF16)
    g3 = lax.dot_general(jnp.concatenate([hi, mid, lo], axis=0), q_ref[...], (((1,), (1,)), ((), ())),
                         preferred_element_type=F32)
    g = g3[0:2 * n_blk] + g3[2 * n_blk:4 * n_blk] + g3[4 * n_blk:6 * n_blk]
    fully_past = (rowi + 1) * t <= cs
    for hh in range(2):
        gv = jnp.where(fully_past, g[hh * n_blk:(hh + 1) * n_blk], -jnp.inf)
        keep = own
        for _ in range(MOBA_TOPK):
            mx = jnp.max(gv, axis=0, keepdims=True)
            cand = jnp.where((gv == mx) & (mx > -jnp.inf), rowi, n_blk)
            pick = rowi == jnp.min(cand, axis=0, keepdims=True)
            keep = keep | pick
            gv = jnp.where(pick, -jnp.inf, gv)
        bt_ref[hh] = jnp.where(keep, 0.0, NEG)


def _moba_gate(q, k, n_heads):
    b, s, _ = q.shape
    n_blk = s // MOBA_BLOCK
    assert n_blk <= LANES
    return pl.pallas_call(
        functools.partial(_moba_gate_kernel, n_blk=n_blk),
        grid=(b, n_heads // 2),
        in_specs=[
            pl.BlockSpec((None, s, LANES), lambda bi, p: (bi, 0, p)),
            pl.BlockSpec((None, s, LANES), lambda bi, p: (bi, 0, p)),
        ],
        out_specs=pl.BlockSpec((None, 2, n_blk, s), lambda bi, p: (bi, p, 0, 0)),
        out_shape=jax.ShapeDtypeStruct((b, n_heads, n_blk, s), F32),
        compiler_params=_cparams(("parallel", "parallel")),
        name="moba_gate",
    )(q, k)


def _attn_kernel(q_ref, k_ref, v_ref, qa_ref, ka_ref, o_ref):
    t = q_ref.shape[0]
    n_sub = t // ATT_SUB
    g = pl.program_id(2)
    lane = lax.broadcasted_iota(jnp.int32, (ATT_SUB, LANES), 1)
    lane_k = lax.broadcasted_iota(jnp.int32, (t, LANES), 1)
    row = lax.broadcasted_iota(jnp.int32, (ATT_SUB, t), 0)
    col = lax.broadcasted_iota(jnp.int32, (ATT_SUB, t), 1)
    nt = (((1,), (1,)), ((), ()))
    qas = []
    for hh in range(2):
        in_head = (lane < HEAD_DIM) if hh == 0 else (lane >= HEAD_DIM)
        for r in range(n_sub):
            rs = slice(r * ATT_SUB, (r + 1) * ATT_SUB)
            q = q_ref[rs, :]
            qas.append(jnp.concatenate([jnp.where(in_head, q, jnp.zeros_like(q)), qa_ref[hh, rs, :]], axis=1))

    def update(state, j, diag):
        rows = pl.ds(pl.multiple_of(j * t, t), t)
        k = k_ref[rows, :]
        v = v_ref[rows, :]
        kks = [jnp.concatenate([k, ka_ref[hh, rows, :]], axis=1) for hh in range(2)]
        vhs = [jnp.where((lane_k < HEAD_DIM) if hh == 0 else (lane_k >= HEAD_DIM), v, jnp.ones_like(v))
               for hh in range(2)]
        n_chain = 2 * n_sub

        def qk(c):
            return lax.dot_general(qas[c], kks[c // n_sub], nt, preferred_element_type=F32)

        scs = [qk(c) for c in range(min(ATT_LOOKAHEAD, n_chain))]
        new = []
        for c in range(n_chain):
            if c + ATT_LOOKAHEAD < n_chain:
                scs.append(qk(c + ATT_LOOKAHEAD))
            m_old, acc = state[c]
            sc = scs[c]
            if diag:
                sc = jnp.where(col <= row + (c % n_sub) * ATT_SUB, sc, NEG)
            m_new = jnp.maximum(m_old, jnp.max(sc, axis=-1, keepdims=True))
            p = jnp.exp2(sc - m_new).astype(BF16)
            acc = jnp.exp2(m_old - m_new) * acc + jnp.dot(p, vhs[c // n_sub], preferred_element_type=F32)
            new.append((m_new, acc))
        return tuple(new)

    init = tuple((jnp.full((ATT_SUB, 1), NEG, F32), jnp.zeros((ATT_SUB, LANES), F32)) for _ in range(2 * n_sub))
    state = lax.fori_loop(0, g, lambda j, st: update(st, j, False), init)
    state = update(state, g, True)
    for r in range(n_sub):
        outs = [state[hh * n_sub + r][1] for hh in range(2)]
        outs = [acc / pltpu.roll(acc, HEAD_DIM, 1) for acc in outs]
        o_ref[r * ATT_SUB:(r + 1) * ATT_SUB, :] = jnp.where(lane < HEAD_DIM, outs[0], outs[1]).astype(o_ref.dtype)


def _attention(q, k, v, qaux, kaux, n_heads, head_off):
    b, s, _ = q.shape
    t = ATT_STEP
    po = head_off // 2
    kb, kh = kaux.shape[0], kaux.shape[1]
    ka_map = (lambda bi, p, i: (bi, p, 0, 0)) if kb == b and kh == n_heads else (lambda bi, p, i: (0, 0, 0, 0))
    return pl.pallas_call(
        _attn_kernel,
        grid=(b, n_heads // 2, s // t),
        in_specs=[
            pl.BlockSpec((None, t, LANES), lambda bi, p, i: (bi, i, p + po)),
            pl.BlockSpec((None, s, LANES), lambda bi, p, i: (bi, 0, p + po)),
            pl.BlockSpec((None, s, LANES), lambda bi, p, i: (bi, 0, p + po)),
            pl.BlockSpec((None, 2, t, LANES), lambda bi, p, i: (bi, p, i, 0)),
            pl.BlockSpec((None, 2, s, LANES), ka_map),
        ],
        out_specs=pl.BlockSpec((None, t, LANES), lambda bi, p, i: (bi, i, p)),
        out_shape=jax.ShapeDtypeStruct((b, s, n_heads * HEAD_DIM), BF16),
        compiler_params=_cparams(("parallel", "parallel", "arbitrary")),
        name="attn",
    )(q, k, v, qaux, kaux)


def _l0_out_kernel(om_ref, of_ref, w_ref, x_ref, mod_ref, g1_ref, g2_ref, x1_ref, h2_ref, *, d):
    m = mod_ref[...]
    hw = om_ref.shape[1]
    y = (jnp.dot(om_ref[...], w_ref[0:hw, :], preferred_element_type=F32)
         + jnp.dot(of_ref[...], w_ref[hw:, :], preferred_element_type=F32))
    x1 = x_ref[...] + m[:, 2 * d:3 * d] * _rms(y, g1_ref[...])
    x1_ref[...] = x1
    h2_ref[...] = (_rms(x1, g2_ref[...]) * (1.0 + m[:, 4 * d:5 * d]) + m[:, 3 * d:4 * d]).astype(BF16)


def _l0_out(o_m, o_f, w_out, x, mods, g_post, g_pre2, tm):
    b, s, d = x.shape
    d6 = mods.shape[-1]
    tok = lambda bi, i: (bi, i, 0)
    return pl.pallas_call(
        functools.partial(_l0_out_kernel, d=d),
        grid=(b, s // tm),
        in_specs=[
            pl.BlockSpec((None, tm, o_m.shape[2]), tok),
            pl.BlockSpec((None, tm, o_f.shape[2]), tok),
            _const_spec(w_out.shape),
            pl.BlockSpec((None, tm, d), tok),
            pl.BlockSpec((None, None, 1, d6), lambda bi, i: (0, bi, 0, 0)),
            _const_spec((1, d)),
            _const_spec((1, d)),
        ],
        out_specs=[pl.BlockSpec((None, tm, d), tok), pl.BlockSpec((None, tm, d), tok)],
        out_shape=[jax.ShapeDtypeStruct((b, s, d), F32), jax.ShapeDtypeStruct((b, s, d), BF16)],
        compiler_params=_cparams(("parallel", "parallel"), VMEM_LIMIT),
        name="l0_out",
    )(o_m, o_f, w_out, x, mods, g_post, g_pre2)


def _ffn_kernel(h_ref, x_ref, wg_ref, wu_ref, wd_ref, mod0_ref, mod1_ref, g1_ref, g2_ref,
                x2_ref, h3_ref, *, d):
    h = h_ref[...]
    a = jnp.dot(h, wg_ref[...], preferred_element_type=F32)
    u = jnp.dot(h, wu_ref[...], preferred_element_type=F32)
    act = (a * jax.nn.sigmoid(a) * u).astype(BF16)
    y = jnp.dot(act, wd_ref[...], preferred_element_type=F32)
    m0 = mod0_ref[...]
    m1 = mod1_ref[...]
    x2 = x_ref[...] + m0[:, 5 * d:6 * d] * _rms(y, g1_ref[...])
    x2_ref[...] = x2
    h3_ref[...] = (_rms(x2, g2_ref[...]) * (1.0 + m1[:, d:2 * d]) + m1[:, 0:d]).astype(BF16)


def _ffn(h2, x1, w_gate, w_up, w_down, mods, g_post, g_pre_next, tm):
    b, s, d = x1.shape
    d6 = mods.shape[-1]
    tok = lambda bi, i: (bi, i, 0)
    single = dict(pipeline_mode=pl.Buffered(1))
    return pl.pallas_call(
        functools.partial(_ffn_kernel, d=d),
        grid=(b, s // tm),
        in_specs=[
            pl.BlockSpec((None, tm, d), tok),
            pl.BlockSpec((None, tm, d), tok),
            pl.BlockSpec(w_gate.shape, lambda bi, i: (0, 0), **single),
            pl.BlockSpec(w_up.shape, lambda bi, i: (0, 0), **single),
            pl.BlockSpec(w_down.shape, lambda bi, i: (0, 0), **single),
            pl.BlockSpec((None, None, 1, d6), lambda bi, i: (0, bi, 0, 0)),
            pl.BlockSpec((None, None, 1, d6), lambda bi, i: (1, bi, 0, 0)),
            _const_spec((1, d)),
            _const_spec((1, d)),
        ],
        out_specs=[pl.BlockSpec((None, tm, d), tok), pl.BlockSpec((None, tm, d), tok)],
        out_shape=[jax.ShapeDtypeStruct((b, s, d), F32), jax.ShapeDtypeStruct((b, s, d), BF16)],
        compiler_params=_cparams(("parallel", "parallel"), VMEM_LIMIT),
        name="ffn",
    )(h2, x1, w_gate, w_up, w_down, mods, mods, g_post, g_pre_next)


def _lru_kernel(h_ref, x_ref, win_ref, cw_ref, cb_ref, wax_ref, ba_ref, bx_ref, lam_ref, wout_ref,
                mod_ref, g1_ref, g2_ref, wr_ref, br_ref,
                x3_ref, h4_ref, lg_ref, utail_ref, hc_ref, *, d, dr, n_rnn_blocks):
    tm = h_ref.shape[0]
    sub = utail_ref.shape[0]

    @pl.when(pl.program_id(1) == 0)
    def _():
        utail_ref[...] = jnp.zeros_like(utail_ref)
        hc_ref[...] = jnp.zeros_like(hc_ref)

    proj = jnp.dot(h_ref[...], win_ref[...], preferred_element_type=F32)
    gate_branch = proj[:, :dr]
    u = proj[:, dr:]
    prev = utail_ref[...]
    row_sub = lax.broadcasted_iota(jnp.int32, (sub, dr), 0)
    conv = cb_ref[...] + u * cw_ref[CONV_WIDTH - 1:CONV_WIDTH, :]
    for back in range(1, CONV_WIDTH):
        sh = pltpu.roll(u, back, 0)
        head = jnp.where(row_sub >= back, sh[0:sub], pltpu.roll(prev, back, 0))
        sh = jnp.concatenate([head, sh[sub:]], axis=0)
        conv = conv + sh * cw_ref[CONV_WIDTH - 1 - back:CONV_WIDTH - back, :]
    utail_ref[...] = u[tm - sub:tm]

    w = dr // n_rnn_blocks
    rs, is_ = [], []
    for n in range(n_rnn_blocks):
        cbk = conv[:, n * w:(n + 1) * w].astype(BF16)
        ra = jnp.dot(cbk, wax_ref[n], preferred_element_type=F32)
        rs.append(ra[:, :w])
        is_.append(ra[:, w:])
    r = jax.nn.sigmoid(jnp.concatenate(rs, axis=1) + ba_ref[...])
    ig = jax.nn.sigmoid(jnp.concatenate(is_, axis=1) + bx_ref[...])
    nl = -lam_ref[...]
    softplus = jnp.maximum(nl, 0.0) + jnp.log(1.0 + jnp.exp(-jnp.abs(nl)))
    log_a = (-RG_C * r) * softplus
    a = jnp.exp(log_a)
    var = 1.0 - jnp.exp(2.0 * log_a)
    xin = jnp.where(var > 0.0, var * lax.rsqrt(var), 0.0) * (ig * conv)

    rowg = lax.broadcasted_iota(jnp.int32, (tm, dr), 0) & (sub - 1)
    sa, sx = a, xin
    dist = 1
    while dist < sub:
        keep = rowg >= dist
        xs = jnp.where(keep, pltpu.roll(sx, dist, 0), 0.0)
        as_ = jnp.where(keep, pltpu.roll(sa, dist, 0), 1.0)
        sx = sx + sa * xs
        sa = sa * as_
        dist *= 2
    carry = hc_ref[...]
    groups = []
    for g in range(tm // sub):
        hg = sx[g * sub:(g + 1) * sub] + sa[g * sub:(g + 1) * sub] * carry
        carry = hg[sub - 1:sub]
        groups.append(hg)
    hs = jnp.concatenate(groups, axis=0)
    hc_ref[...] = carry

    y = (jax.nn.gelu(gate_branch, approximate=True) * hs).astype(BF16)
    out = jnp.dot(y, wout_ref[...], preferred_element_type=F32)
    m = mod_ref[...]
    x3 = x_ref[...] + m[:, 2 * d:3 * d] * _rms(out, g1_ref[...])
    x3_ref[...] = x3
    h4 = _rms(x3, g2_ref[...]) * (1.0 + m[:, 4 * d:5 * d]) + m[:, 3 * d:4 * d]
    h4_ref[...] = h4
    h_hi = h4.astype(BF16)
    h_lo = (h4 - h_hi.astype(F32)).astype(BF16)
    lg_ref[...] = jnp.dot(jnp.concatenate([h_hi, h_hi, h_lo], axis=1), wr_ref[...],
                          preferred_element_type=F32) + br_ref[...]


def _lru(h3, x2, w_in, conv_w, conv_b, wax, b_a, b_x, lam, w_out, mods, g_post, g_pre2, w_r, b_r, tm):
    b, s, d = x2.shape
    dr = w_out.shape[0]
    d6 = mods.shape[-1]
    nb = wax.shape[0]
    tok = lambda bi, i: (bi, i, 0)
    return pl.pallas_call(
        functools.partial(_lru_kernel, d=d, dr=dr, n_rnn_blocks=nb),
        grid=(b, s // tm),
        in_specs=[
            pl.BlockSpec((None, tm, d), tok),
            pl.BlockSpec((None, tm, d), tok),
            _const_spec(w_in.shape),
            _const_spec(conv_w.shape),
            _const_spec((1, dr)),
            _const_spec(wax.shape),
            _const_spec((1, dr)),
            _const_spec((1, dr)),
            _const_spec((1, dr)),
            _const_spec(w_out.shape),
            pl.BlockSpec((None, None, 1, d6), lambda bi, i: (1, bi, 0, 0)),
            _const_spec((1, d)),
            _const_spec((1, d)),
            _const_spec(w_r.shape),
            _const_spec((1, LANES)),
        ],
        out_specs=[pl.BlockSpec((None, tm, d), tok), pl.BlockSpec((None, tm, d), tok),
                   pl.BlockSpec((None, tm, LANES), tok)],
        out_shape=[jax.ShapeDtypeStruct((b, s, d), F32), jax.ShapeDtypeStruct((b, s, d), F32),
                   jax.ShapeDtypeStruct((b, s, LANES), F32)],
        scratch_shapes=[pltpu.VMEM((SUBLANES, dr), F32), pltpu.VMEM((1, dr), F32)],
        compiler_params=_cparams(("parallel", "arbitrary"), VMEM_LIMIT),
        name="lru",
    )(h3, x2, w_in, conv_w, conv_b, wax, b_a, b_x, lam, w_out, mods, g_post, g_pre2, w_r, b_r)


def _moe_kernel(bexp_ref, nused_ref, tok_ref, h_hbm, wg_ref, wu_ref, wd_ref, y_ref, xbuf, sem, *, n_blocks):
    del bexp_ref
    mb = y_ref.shape[0]
    i = pl.program_id(0)
    nused = nused_ref[0]

    def row_copy(blk, slot, r):
        tok = tok_ref[blk * mb + r]
        return pltpu.make_async_copy(h_hbm.at[pl.ds(tok, 1), :], xbuf.at[slot, pl.ds(r, 1), :], sem.at[slot])

    def start_gather(blk, slot):
        def body(r, c):
            row_copy(blk, slot, r).start()
            return c
        lax.fori_loop(0, mb, body, 0, unroll=8)

    def wait_gather(slot):
        pltpu.make_async_copy(h_hbm.at[pl.ds(0, mb), :], xbuf.at[slot], sem.at[slot]).wait()

    slot = lax.rem(i, 2)

    @pl.when((i == 0) & (nused > 0))
    def _():
        start_gather(0, 0)

    @pl.when(i + 1 < nused)
    def _():
        start_gather(i + 1, 1 - slot)

    @pl.when(i < nused)
    def _():
        wait_gather(slot)
        x = xbuf[slot].astype(BF16)
        dff = wg_ref.shape[1]
        acc = jnp.zeros(y_ref.shape, F32)
        for c in range(dff // MOE_FF_CHUNK):
            cs = slice(c * MOE_FF_CHUNK, (c + 1) * MOE_FF_CHUNK)
            a = jnp.dot(x, wg_ref[:, cs], preferred_element_type=F32)
            u = jnp.dot(x, wu_ref[:, cs], preferred_element_type=F32)
            act = (a * jax.nn.sigmoid(a) * u).astype(BF16)
            acc = acc + jnp.dot(act, wd_ref[cs, :], preferred_element_type=F32)
        y_ref[...] = acc

    @pl.when(i >= nused)
    def _():
        y_ref[...] = jnp.zeros_like(y_ref)


def _moe(block_exp, nused, slot_tok, h4, w_gate, w_up, w_down):
    n_tok, d = h4.shape
    mb = MOE_BLOCK_ROWS
    cap = slot_tok.shape[0]
    n_blocks = cap // mb
    e, _, dff = w_gate.shape
    single = dict(pipeline_mode=pl.Buffered(1))
    grid_spec = pltpu.PrefetchScalarGridSpec(
        num_scalar_prefetch=3,
        grid=(n_blocks,),
        in_specs=[
            pl.BlockSpec(memory_space=pl.ANY),
            pl.BlockSpec((None, d, dff), lambda i, be, nu, st: (be[i], 0, 0), **single),
            pl.BlockSpec((None, d, dff), lambda i, be, nu, st: (be[i], 0, 0), **single),
            pl.BlockSpec((None, dff, d), lambda i, be, nu, st: (be[i], 0, 0), **single),
        ],
        out_specs=pl.BlockSpec((mb, d), lambda i, be, nu, st: (i, 0)),
        scratch_shapes=[pltpu.VMEM((2, mb, d), F32), pltpu.SemaphoreType.DMA((2,))],
    )
    return pl.pallas_call(
        functools.partial(_moe_kernel, n_blocks=n_blocks),
        grid_spec=grid_spec,
        out_shape=jax.ShapeDtypeStruct((cap, d), F32),
        compiler_params=_cparams(("arbitrary",), VMEM_LIMIT),
        name="moe",
    )(block_exp, nused, slot_tok, h4, w_gate, w_up, w_down)


def _moe_out_kernel(pos_ref, y_hbm, gate_ref, x_ref, mod_ref, g_ref, o_ref, gbuf, sem, *, d):
    tm = x_ref.shape[0]
    i = pl.program_id(0)
    n = pl.num_programs(0)

    def row_copy(blk, slot, r, kk):
        p = pos_ref[(blk * tm + r) * TOP_K + kk]
        return pltpu.make_async_copy(y_hbm.at[pl.ds(p, 1), :], gbuf.at[slot, kk, pl.ds(r, 1), :], sem.at[slot])

    def start_gather(blk, slot):
        def body(r, c):
            for kk in range(TOP_K):
                row_copy(blk, slot, r, kk).start()
            return c
        lax.fori_loop(0, tm, body, 0, unroll=8)

    def wait_gather(slot):
        for kk in range(TOP_K):
            pltpu.make_async_copy(y_hbm.at[pl.ds(0, tm), :], gbuf.at[slot, kk], sem.at[slot]).wait()

    slot = lax.rem(i, 2)

    @pl.when(i == 0)
    def _():
        start_gather(0, 0)

    @pl.when(i + 1 < n)
    def _():
        start_gather(i + 1, 1 - slot)

    wait_gather(slot)
    g = gate_ref[...]
    y = gbuf[slot, 0] * g[:, 0:1]
    for kk in range(1, TOP_K):
        y = y + gbuf[slot, kk] * g[:, kk:kk + 1]
    m = mod_ref[...]
    o_ref[...] = x_ref[...] + m[:, 5 * d:6 * d] * _rms(y, g_ref[...])


def _moe_out(pos, y_slots, gates, x3, mods, g_post, s, tm):
    n_tok, d = x3.shape
    d6 = mods.shape[-1]
    per_b = s // tm
    grid_spec = pltpu.PrefetchScalarGridSpec(
        num_scalar_prefetch=1,
        grid=(n_tok // tm,),
        in_specs=[
            pl.BlockSpec(memory_space=pl.ANY),
            pl.BlockSpec((tm, TOP_K), lambda i, p: (i, 0)),
            pl.BlockSpec((tm, d), lambda i, p: (i, 0)),
            pl.BlockSpec((None, None, 1, d6), lambda i, p: (1, i // per_b, 0, 0)),
            pl.BlockSpec((1, d), lambda i, p: (0, 0)),
        ],
        out_specs=pl.BlockSpec((tm, d), lambda i, p: (i, 0)),
        scratch_shapes=[pltpu.VMEM((2, TOP_K, tm, d), F32), pltpu.SemaphoreType.DMA((2,))],
    )
    return pl.pallas_call(
        functools.partial(_moe_out_kernel, d=d),
        grid_spec=grid_spec,
        out_shape=jax.ShapeDtypeStruct((n_tok, d), F32),
        compiler_params=_cparams(("arbitrary",), VMEM_LIMIT),
        name="moe_out",
    )(pos, y_slots, gates, x3, mods, g_post)


def _route(logits, n_experts, mb):
    n_tok = logits.shape[0]
    top_logit, top_idx = lax.top_k(logits, TOP_K)
    gates = jax.nn.softmax(top_logit, axis=-1)
    exp_flat = top_idx.reshape(-1).astype(jnp.int32)
    n_asg = n_tok * TOP_K
    onehot = (exp_flat[:, None] == jnp.arange(n_experts, dtype=jnp.int32)[None, :]).astype(jnp.int32)
    csum = jnp.cumsum(onehot, axis=0)
    counts = csum[-1]
    rank = jnp.sum((csum - onehot) * onehot, axis=1)
    padded = ((counts + mb - 1) // mb) * mb
    pends = jnp.cumsum(padded)
    pstarts = pends - padded
    pos = (pstarts[exp_flat] + rank).astype(jnp.int32)
    cap = (-(-n_asg // mb) + n_experts) * mb
    tok_flat = jnp.arange(n_asg, dtype=jnp.int32) // TOP_K
    slot_tok = jnp.zeros((cap,), jnp.int32).at[pos].set(tok_flat)
    n_blocks = cap // mb
    block_starts = jnp.arange(n_blocks, dtype=jnp.int32) * mb
    block_exp = jnp.minimum(jnp.searchsorted(pends, block_starts, side='right'), n_experts - 1).astype(jnp.int32)
    nused = (pends[-1] // mb).astype(jnp.int32).reshape(1)
    return gates, pos, slot_tok, block_exp, nused


def _rope_tables(s):
    half = ROPE_DIMS // 2
    inv_freq = jnp.power(ROPE_THETA, -jnp.arange(half, dtype=F32) / half)
    ang = jnp.arange(s, dtype=F32)[:, None] * inv_freq[None, :]
    cos, sin = jnp.cos(ang), jnp.sin(ang)
    lane = jnp.arange(LANES) % HEAD_DIM
    idx = lane % half
    is_x1 = lane < half
    is_x2 = (lane >= half) & (lane < ROPE_DIMS)
    c = jnp.where((is_x1 | is_x2)[None, :], cos[:, idx], 1.0)
    sa = jnp.where(is_x2[None, :], sin[:, idx], 0.0)
    sb = jnp.where(is_x1[None, :], -sin[:, idx], 0.0)
    return c.astype(F32), sa.astype(F32), sb.astype(F32)


def kernel(x, c, w_ada, b_ada, norm_g, attn_w_in, fox_b_f, attn_w_out, ffn_w_gate, ffn_w_up, ffn_w_down,
           lru_w_in, lru_conv_w, lru_conv_b, lru_w_a, lru_b_a, lru_w_x, lru_b_x, lru_lambda, lru_w_out,
           moe_w_router, moe_b_router, moe_w_gate, moe_w_up, moe_w_down):
    b, s, d = x.shape
    aw = attn_w_out.shape[1]
    n_experts = moe_w_router.shape[2]
    assert s % ATT_STEP == 0 and ATT_STEP % MOBA_BLOCK == 0 and ATT_STEP % ATT_SUB == 0
    tm = min(512, s)

    c_pad = jnp.zeros((8, d), F32).at[:b].set(c)
    mods = _ada(c_pad, w_ada, b_ada)[:, :b].reshape(w_ada.shape[0], b, 1, 6 * d)

    w_in = attn_w_in[0]
    w_qkv = w_in[:, :3 * aw].astype(BF16)
    w_f = jnp.zeros((d, LANES), F32).at[:, :N_FOX_HEADS].set(w_in[:, 3 * aw:]).astype(BF16)
    b_f = jnp.zeros((1, LANES), F32).at[0, :N_FOX_HEADS].set(fox_b_f[0])
    rope_c, rope_sa, rope_sb = _rope_tables(s)
    q, k, v, logf = _l0_in(x, mods, norm_g[0, 0][None], w_qkv, w_f, b_f, rope_c, rope_sa, rope_sb, tm)

    n_blk = s // MOBA_BLOCK
    bias_t = _moba_gate(q, k, N_MOBA_HEADS)
    qaux_m = jnp.pad(jnp.swapaxes(bias_t, 2, 3).astype(BF16), ((0, 0), (0, 0), (0, 0), (0, LANES - n_blk)))
    blk_of_pos = jnp.arange(s, dtype=jnp.int32) // MOBA_BLOCK
    kaux_row = jnp.arange(LANES, dtype=jnp.int32)[None, :] == blk_of_pos[:, None]
    kaux_m = jnp.broadcast_to(kaux_row.astype(BF16)[None, None], (1, 2, s, LANES))
    o_m = _attention(q, k, v, qaux_m, kaux_m, N_MOBA_HEADS, 0)

    qaux_f, kaux_f = _fox_prep(logf, N_FOX_HEADS)
    o_f = _attention(q, k, v, qaux_f, kaux_f, N_FOX_HEADS, N_MOBA_HEADS)

    x1, h2 = _l0_out(o_m, o_f, attn_w_out[0].astype(BF16), x, mods, norm_g[0, 1][None], norm_g[0, 2][None], tm)

    x2, h3 = _ffn(h2, x1, ffn_w_gate[0].astype(BF16), ffn_w_up[0].astype(BF16), ffn_w_down[0].astype(BF16),
                  mods, norm_g[0, 3][None], norm_g[1, 0][None], min(256, s))

    wax = jnp.concatenate([lru_w_a[0], lru_w_x[0]], axis=-1).astype(BF16)
    w_r = jnp.zeros((d, LANES), F32).at[:, :n_experts].set(moe_w_router[0])
    w_r_hi = w_r.astype(BF16)
    w_r_lo = (w_r - w_r_hi.astype(F32)).astype(BF16)
    w_r = jnp.concatenate([w_r_hi, w_r_lo, w_r_hi], axis=0)
    b_r = jnp.zeros((1, LANES), F32).at[0, :n_experts].set(moe_b_router[0])
    x3, h4, logits = _lru(h3, x2, lru_w_in[0].astype(BF16), lru_conv_w[0], lru_conv_b[0][None], wax,
                          lru_b_a[0][None], lru_b_x[0][None], lru_lambda[0][None], lru_w_out[0].astype(BF16),
                          mods, norm_g[1, 1][None], norm_g[1, 2][None], w_r, b_r, min(256, s))

    n_tok = b * s
    gates, pos, slot_tok, block_exp, nused = _route(logits.reshape(n_tok, LANES)[:, :n_experts],
                                                    n_experts, MOE_BLOCK_ROWS)
    y_slots = _moe(block_exp, nused, slot_tok, h4.reshape(n_tok, d),
                   moe_w_gate[0].astype(BF16), moe_w_up[0].astype(BF16), moe_w_down[0].astype(BF16))
    out = _moe_out(pos, y_slots, gates, x3.reshape(n_tok, d), mods, norm_g[1, 3][None], s, min(256, s))
    return out.reshape(b, s, d)
```

```python
import functools

import jax
import jax.numpy as jnp
from jax import lax
from jax.experimental import pallas as pl
from jax.experimental.pallas import tpu as pltpu

F32 = jnp.float32
BF16 = jnp.bfloat16
HIGHEST = lax.Precision.HIGHEST

NORM_EPS = 1e-6
HEAD_DIM = 64
N_MOBA_HEADS = 8
N_FOX_HEADS = 8
ROPE_DIMS = 16
ROPE_THETA = 500000.0
MOBA_BLOCK = 256
MOBA_TOPK = 3
CONV_WIDTH = 4
RG_C = 8.0
TOP_K = 2

LANES = 128
SUBLANES = 8
ATT_BLOCK = 256
ATT_STEP = 512
ATT_SUB = 256
ATT_LOOKAHEAD = 2
NEG = -1e30
LOG2E = 1.4426950408889634
MOE_BLOCK_ROWS = 512
MOE_FF_CHUNK = 512
VMEM_LIMIT = 56 * 1024 * 1024


def _cparams(sem, vmem=None):
    return pltpu.CompilerParams(dimension_semantics=sem, vmem_limit_bytes=vmem)


def _rms(x, g):
    return x * lax.rsqrt(jnp.mean(x * x, axis=-1, keepdims=True) + NORM_EPS) * g


def _const_spec(shape):
    n = len(shape)
    return pl.BlockSpec(shape, lambda *_: (0,) * n)


def _ada_kernel(c_ref, w_ref, b_ref, o_ref):
    c = c_ref[...]
    cond = c * jax.nn.sigmoid(c)
    o_ref[...] = jnp.dot(cond, w_ref[...], preferred_element_type=F32, precision=HIGHEST) + b_ref[...]


def _ada(c_pad, w_ada, b_ada):
    depth, d, d6 = w_ada.shape
    rows = c_pad.shape[0]
    nj = d6 // d
    return pl.pallas_call(
        _ada_kernel,
        grid=(depth, nj),
        in_specs=[
            pl.BlockSpec((rows, d), lambda l, j: (0, 0)),
            pl.BlockSpec((None, d, d), lambda l, j: (l, 0, j)),
            pl.BlockSpec((None, 1, d), lambda l, j: (l, 0, j)),
        ],
        out_specs=pl.BlockSpec((None, rows, d), lambda l, j: (l, 0, j)),
        out_shape=jax.ShapeDtypeStruct((depth, rows, d6), F32),
        compiler_params=_cparams(("parallel", "parallel")),
        name="ada",
    )(c_pad, w_ada, b_ada.reshape(depth, 1, d6))


def _l0_in_kernel(x_ref, mod_ref, g_ref, w_ref, wf_ref, bf_ref, rc_ref, rsa_ref, rsb_ref,
                  q_ref, k_ref, v_ref, lf_ref, *, d, aw, n_rope_chunks):
    m = mod_ref[...]
    h = _rms(x_ref[...], g_ref[...]) * (1.0 + m[:, d:2 * d]) + m[:, 0:d]
    hb = h.astype(BF16)
    proj = jnp.dot(hb, w_ref[...], preferred_element_type=F32)
    rc, rsa, rsb = rc_ref[...], rsa_ref[...], rsb_ref[...]

    def rope(t):
        return t * rc + pltpu.roll(t, ROPE_DIMS // 2, 1) * rsa + pltpu.roll(t, LANES - ROPE_DIMS // 2, 1) * rsb

    scale = HEAD_DIM ** -0.5 * LOG2E
    for c in range(aw // LANES):
        sl = slice(c * LANES, (c + 1) * LANES)
        qc = proj[:, sl] * scale
        kc = proj[:, aw + c * LANES:aw + (c + 1) * LANES]
        if c < n_rope_chunks:
            qc, kc = rope(qc), rope(kc)
        q_ref[:, sl] = qc.astype(BF16)
        k_ref[:, sl] = kc.astype(BF16)
    v_ref[...] = proj[:, 2 * aw:3 * aw].astype(BF16)
    fl = jnp.dot(hb, wf_ref[...], preferred_element_type=F32) + bf_ref[...]
    lf_ref[...] = jnp.minimum(fl, 0.0) - jnp.log(1.0 + jnp.exp(-jnp.abs(fl)))


def _l0_in(x, mods, g_pre, w_qkv, w_f, b_f, rope_c, rope_sa, rope_sb, tm):
    b, s, d = x.shape
    aw = w_qkv.shape[1] // 3
    d6 = mods.shape[-1]
    tok = lambda bi, i: (bi, i, 0)
    kern = functools.partial(_l0_in_kernel, d=d, aw=aw,
                             n_rope_chunks=N_MOBA_HEADS * HEAD_DIM // LANES)
    return pl.pallas_call(
        kern,
        grid=(b, s // tm),
        in_specs=[
            pl.BlockSpec((None, tm, d), tok),
            pl.BlockSpec((None, None, 1, d6), lambda bi, i: (0, bi, 0, 0)),
            _const_spec((1, d)),
            _const_spec(w_qkv.shape),
            _const_spec(w_f.shape),
            _const_spec((1, LANES)),
            pl.BlockSpec((tm, LANES), lambda bi, i: (i, 0)),
            pl.BlockSpec((tm, LANES), lambda bi, i: (i, 0)),
            pl.BlockSpec((tm, LANES), lambda bi, i: (i, 0)),
        ],
        out_specs=[
            pl.BlockSpec((None, tm, aw), tok),
            pl.BlockSpec((None, tm, aw), tok),
            pl.BlockSpec((None, tm, aw), tok),
            pl.BlockSpec((None, tm, LANES), tok),
        ],
        out_shape=[
            jax.ShapeDtypeStruct((b, s, aw), BF16),
            jax.ShapeDtypeStruct((b, s, aw), BF16),
            jax.ShapeDtypeStruct((b, s, aw), BF16),
            jax.ShapeDtypeStruct((b, s, LANES), F32),
        ],
        compiler_params=_cparams(("parallel", "parallel"), VMEM_LIMIT),
        name="l0_in",
    )(x, mods, g_pre, w_qkv, w_f, b_f, rope_c, rope_sa, rope_sb)


def _fox_prep_kernel(lf_ref, qa_ref, ka_ref, carry_ref, *, n_heads):
    t = lf_ref.shape[0]

    @pl.when(pl.program_id(1) == 0)
    def _():
        carry_ref[...] = jnp.zeros_like(carry_ref)

    row = lax.broadcasted_iota(jnp.int32, (t, t), 0)
    col = lax.broadcasted_iota(jnp.int32, (t, t), 1)
    tri = (col <= row).astype(F32)
    cum = jnp.dot(tri, lf_ref[...], preferred_element_type=F32, precision=HIGHEST) + carry_ref[...]
    carry_ref[...] = cum[t - 1:t, :]
    lane = lax.broadcasted_iota(jnp.int32, (t, LANES), 1)
    for h in range(n_heads):
        c = jnp.broadcast_to(cum[:, h:h + 1], (t, LANES)) * LOG2E
        hi = c.astype(BF16).astype(F32)
        r1 = c - hi
        mid = r1.astype(BF16).astype(F32)
        lo = r1 - mid
        qa = jnp.where(lane == 0, hi, jnp.where(lane == 1, mid, jnp.where(lane == 2, lo,
                       jnp.where(lane < 6, 1.0, 0.0))))
        ka = jnp.where(lane < 3, 1.0, jnp.where(lane == 3, -hi, jnp.where(lane == 4, -mid,
                       jnp.where(lane == 5, -lo, 0.0))))
        qa_ref[h] = qa.astype(BF16)
        ka_ref[h] = ka.astype(BF16)


def _fox_prep(logf, n_heads):
    b, s, _ = logf.shape
    t = ATT_BLOCK
    aux = jax.ShapeDtypeStruct((b, n_heads, s, LANES), BF16)
    aux_spec = pl.BlockSpec((None, n_heads, t, LANES), lambda bi, i: (bi, 0, i, 0))
    return pl.pallas_call(
        functools.partial(_fox_prep_kernel, n_heads=n_heads),
        grid=(b, s // t),
        in_specs=[pl.BlockSpec((None, t, LANES), lambda bi, i: (bi, i, 0))],
        out_specs=[aux_spec, aux_spec],
        out_shape=[aux, aux],
        scratch_shapes=[pltpu.VMEM((1, LANES), F32)],
        compiler_params=_cparams(("parallel", "arbitrary")),
        name="fox_prep",
    )(logf)


def _moba_gate_kernel(q_ref, k_ref, bt_ref, *, n_blk):
    s = k_ref.shape[0]
    t = MOBA_BLOCK
    rowi = lax.broadcasted_iota(jnp.int32, (n_blk, s), 0)
    cs = lax.broadcasted_iota(jnp.int32, (n_blk, s), 1)
    own = (cs >= rowi * t) & (cs < (rowi + 1) * t)
    km = jnp.dot(jnp.where(own, 1.0 / t, 0.0).astype(BF16), k_ref[...], preferred_element_type=F32)
    lane_k = lax.broadcasted_iota(jnp.int32, (n_blk, LANES), 1)
    km2 = jnp.concatenate([jnp.where(lane_k < HEAD_DIM, km, 0.0), jnp.where(lane_k >= HEAD_DIM, km, 0.0)], axis=0)
    hi = km2.astype(BF16)
    r1 = km2 - hi.astype(F32)
    mid = r1.astype(BF16)
    lo = (r1 - mid.astype(F32)).astype(BF16)
    g3 = lax.dot_general(jnp.concatenate([hi, mid, lo], axis=0), q_ref[...], (((1,), (1,)), ((), ())),
                         preferred_element_type=F32)
    g = g3[0:2 * n_blk] + g3[2 * n_blk:4 * n_blk] + g3[4 * n_blk:6 * n_blk]
    fully_past = (rowi + 1) * t <= cs
    for hh in range(2):
        gv = jnp.where(fully_past, g[hh * n_blk:(hh + 1) * n_blk], -jnp.inf)
        keep = own
        for _ in range(MOBA_TOPK):
            mx = jnp.max(gv, axis=0, keepdims=True)
            cand = jnp.where((gv == mx) & (mx > -jnp.inf), rowi, n_blk)
            pick = rowi == jnp.min(cand, axis=0, keepdims=True)
            keep = keep | pick
            gv = jnp.where(pick, -jnp.inf, gv)
        bt_ref[hh] = jnp.where(keep, 0.0, NEG)


def _moba_gate(q, k, n_heads):
    b, s, _ = q.shape
    n_blk = s // MOBA_BLOCK
    assert n_blk <= LANES
    return pl.pallas_call(
        functools.partial(_moba_gate_kernel, n_blk=n_blk),
        grid=(b, n_heads // 2),
        in_specs=[
            pl.BlockSpec((None, s, LANES), lambda bi, p: (bi, 0, p)),
            pl.BlockSpec((None, s, LANES), lambda bi, p: (bi, 0, p)),
        ],
        out_specs=pl.BlockSpec((None, 2, n_blk, s), lambda bi, p: (bi, p, 0, 0)),
        out_shape=jax.ShapeDtypeStruct((b, n_heads, n_blk, s), F32),
        compiler_params=_cparams(("parallel", "parallel")),
        name="moba_gate",
    )(q, k)


def _attn_kernel(q_ref, k_ref, v_ref, qa_ref, ka_ref, o_ref):
    t = q_ref.shape[0]
    n_sub = t // ATT_SUB
    g = pl.program_id(2)
    lane = lax.broadcasted_iota(jnp.int32, (ATT_SUB, LANES), 1)
    lane_k = lax.broadcasted_iota(jnp.int32, (t, LANES), 1)
    row = lax.broadcasted_iota(jnp.int32, (ATT_SUB, t), 0)
    col = lax.broadcasted_iota(jnp.int32, (ATT_SUB, t), 1)
    nt = (((1,), (1,)), ((), ()))
    qas = []
    for hh in range(2):
        in_head = (lane < HEAD_DIM) if hh == 0 else (lane >= HEAD_DIM)
        for r in range(n_sub):
            rs = slice(r * ATT_SUB, (r + 1) * ATT_SUB)
            q = q_ref[rs, :]
            qas.append(jnp.concatenate([jnp.where(in_head, q, jnp.zeros_like(q)), qa_ref[hh, rs, :]], axis=1))

    def update(state, j, diag):
        rows = pl.ds(pl.multiple_of(j * t, t), t)
        k = k_ref[rows, :]
        v = v_ref[rows, :]
        kks = [jnp.concatenate([k, ka_ref[hh, rows, :]], axis=1) for hh in range(2)]
        vhs = [jnp.where((lane_k < HEAD_DIM) if hh == 0 else (lane_k >= HEAD_DIM), v, jnp.ones_like(v))
               for hh in range(2)]
        n_chain = 2 * n_sub

        def qk(c):
            return lax.dot_general(qas[c], kks[c // n_sub], nt, preferred_element_type=F32)

        scs = [qk(c) for c in range(min(ATT_LOOKAHEAD, n_chain))]
        new = []
        for c in range(n_chain):
            if c + ATT_LOOKAHEAD < n_chain:
                scs.append(qk(c + ATT_LOOKAHEAD))
            m_old, acc = state[c]
            sc = scs[c]
            if diag:
                sc = jnp.where(col <= row + (c % n_sub) * ATT_SUB, sc, NEG)
            m_new = jnp.maximum(m_old, jnp.max(sc, axis=-1, keepdims=True))
            p = jnp.exp2(sc - m_new).astype(BF16)
            acc = jnp.exp2(m_old - m_new) * acc + jnp.dot(p, vhs[c // n_sub], preferred_element_type=F32)
            new.append((m_new, acc))
        return tuple(new)

    init = tuple((jnp.full((ATT_SUB, 1), NEG, F32), jnp.zeros((ATT_SUB, LANES), F32)) for _ in range(2 * n_sub))
    state = lax.fori_loop(0, g, lambda j, st: update(st, j, False), init)
    state = update(state, g, True)
    for r in range(n_sub):
        outs = [state[hh * n_sub + r][1] for hh in range(2)]
        outs = [acc / pltpu.roll(acc, HEAD_DIM, 1) for acc in outs]
        o_ref[r * ATT_SUB:(r + 1) * ATT_SUB, :] = jnp.where(lane < HEAD_DIM, outs[0], outs[1]).astype(o_ref.dtype)


def _attention(q, k, v, qaux, kaux, n_heads, head_off):
    b, s, _ = q.shape
    t = ATT_STEP
    po = head_off // 2
    kb, kh = kaux.shape[0], kaux.shape[1]
    ka_map = (lambda bi, p, i: (bi, p, 0, 0)) if kb == b and kh == n_heads else (lambda bi, p, i: (0, 0, 0, 0))
    return pl.pallas_call(
        _attn_kernel,
        grid=(b, n_heads // 2, s // t),
        in_specs=[
            pl.BlockSpec((None, t, LANES), lambda bi, p, i: (bi, i, p + po)),
            pl.BlockSpec((None, s, LANES), lambda bi, p, i: (bi, 0, p + po)),
            pl.BlockSpec((None, s, LANES), lambda bi, p, i: (bi, 0, p + po)),
            pl.BlockSpec((None, 2, t, LANES), lambda bi, p, i: (bi, p, i, 0)),
            pl.BlockSpec((None, 2, s, LANES), ka_map),
        ],
        out_specs=pl.BlockSpec((None, t, LANES), lambda bi, p, i: (bi, i, p)),
        out_shape=jax.ShapeDtypeStruct((b, s, n_heads * HEAD_DIM), BF16),
        compiler_params=_cparams(("parallel", "parallel", "arbitrary")),
        name="attn",
    )(q, k, v, qaux, kaux)


def _l0_out_kernel(om_ref, of_ref, w_ref, x_ref, mod_ref, g1_ref, g2_ref, x1_ref, h2_ref, *, d):
    m = mod_ref[...]
    hw = om_ref.shape[1]
    y = (jnp.dot(om_ref[...], w_ref[0:hw, :], preferred_element_type=F32)
         + jnp.dot(of_ref[...], w_ref[hw:, :], preferred_element_type=F32))
    x1 = x_ref[...] + m[:, 2 * d:3 * d] * _rms(y, g1_ref[...])
    x1_ref[...] = x1
    h2_ref[...] = (_rms(x1, g2_ref[...]) * (1.0 + m[:, 4 * d:5 * d]) + m[:, 3 * d:4 * d]).astype(BF16)


def _l0_out(o_m, o_f, w_out, x, mods, g_post, g_pre2, tm):
    b, s, d = x.shape
    d6 = mods.shape[-1]
    tok = lambda bi, i: (bi, i, 0)
    return pl.pallas_call(
        functools.partial(_l0_out_kernel, d=d),
        grid=(b, s // tm),
        in_specs=[
            pl.BlockSpec((None, tm, o_m.shape[2]), tok),
            pl.BlockSpec((None, tm, o_f.shape[2]), tok),
            _const_spec(w_out.shape),
            pl.BlockSpec((None, tm, d), tok),
            pl.BlockSpec((None, None, 1, d6), lambda bi, i: (0, bi, 0, 0)),
            _const_spec((1, d)),
            _const_spec((1, d)),
        ],
        out_specs=[pl.BlockSpec((None, tm, d), tok), pl.BlockSpec((None, tm, d), tok)],
        out_shape=[jax.ShapeDtypeStruct((b, s, d), F32), jax.ShapeDtypeStruct((b, s, d), BF16)],
        compiler_params=_cparams(("parallel", "parallel"), VMEM_LIMIT),
        name="l0_out",
    )(o_m, o_f, w_out, x, mods, g_post, g_pre2)


def _ffn_kernel(h_ref, x_ref, wg_ref, wu_ref, wd_ref, mod0_ref, mod1_ref, g1_ref, g2_ref,
                x2_ref, h3_ref, *, d):
    h = h_ref[...]
    a = jnp.dot(h, wg_ref[...], preferred_element_type=F32)
    u = jnp.dot(h, wu_ref[...], preferred_element_type=F32)
    act = (a * jax.nn.sigmoid(a) * u).astype(BF16)
    y = jnp.dot(act, wd_ref[...], preferred_element_type=F32)
    m0 = mod0_ref[...]
    m1 = mod1_ref[...]
    x2 = x_ref[...] + m0[:, 5 * d:6 * d] * _rms(y, g1_ref[...])
    x2_ref[...] = x2
    h3_ref[...] = (_rms(x2, g2_ref[...]) * (1.0 + m1[:, d:2 * d]) + m1[:, 0:d]).astype(BF16)


def _ffn(h2, x1, w_gate, w_up, w_down, mods, g_post, g_pre_next, tm):
    b, s, d = x1.shape
    d6 = mods.shape[-1]
    tok = lambda bi, i: (bi, i, 0)
    single = dict(pipeline_mode=pl.Buffered(1))
    return pl.pallas_call(
        functools.partial(_ffn_kernel, d=d),
        grid=(b, s // tm),
        in_specs=[
            pl.BlockSpec((None, tm, d), tok),
            pl.BlockSpec((None, tm, d), tok),
            pl.BlockSpec(w_gate.shape, lambda bi, i: (0, 0), **single),
            pl.BlockSpec(w_up.shape, lambda bi, i: (0, 0), **single),
            pl.BlockSpec(w_down.shape, lambda bi, i: (0, 0), **single),
            pl.BlockSpec((None, None, 1, d6), lambda bi, i: (0, bi, 0, 0)),
            pl.BlockSpec((None, None, 1, d6), lambda bi, i: (1, bi, 0, 0)),
            _const_spec((1, d)),
            _const_spec((1, d)),
        ],
        out_specs=[pl.BlockSpec((None, tm, d), tok), pl.BlockSpec((None, tm, d), tok)],
        out_shape=[jax.ShapeDtypeStruct((b, s, d), F32), jax.ShapeDtypeStruct((b, s, d), BF16)],
        compiler_params=_cparams(("parallel", "parallel"), VMEM_LIMIT),
        name="ffn",
    )(h2, x1, w_gate, w_up, w_down, mods, mods, g_post, g_pre_next)


def _lru_kernel(h_ref, x_ref, win_ref, cw_ref, cb_ref, wax_ref, ba_ref, bx_ref, lam_ref, wout_ref,
                mod_ref, g1_ref, g2_ref, wr_ref, br_ref,
                x3_ref, h4_ref, lg_ref, utail_ref, hc_ref, *, d, dr, n_rnn_blocks):
    tm = h_ref.shape[0]
    sub = utail_ref.shape[0]

    @pl.when(pl.program_id(1) == 0)
    def _():
        utail_ref[...] = jnp.zeros_like(utail_ref)
        hc_ref[...] = jnp.zeros_like(hc_ref)

    proj = jnp.dot(h_ref[...], win_ref[...], preferred_element_type=F32)
    gate_branch = proj[:, :dr]
    u = proj[:, dr:]
    prev = utail_ref[...]
    row_sub = lax.broadcasted_iota(jnp.int32, (sub, dr), 0)
    conv = cb_ref[...] + u * cw_ref[CONV_WIDTH - 1:CONV_WIDTH, :]
    for back in range(1, CONV_WIDTH):
        sh = pltpu.roll(u, back, 0)
        head = jnp.where(row_sub >= back, sh[0:sub], pltpu.roll(prev, back, 0))
        sh = jnp.concatenate([head, sh[sub:]], axis=0)
        conv = conv + sh * cw_ref[CONV_WIDTH - 1 - back:CONV_WIDTH - back, :]
    utail_ref[...] = u[tm - sub:tm]

    w = dr // n_rnn_blocks
    rs, is_ = [], []
    for n in range(n_rnn_blocks):
        cbk = conv[:, n * w:(n + 1) * w].astype(BF16)
        ra = jnp.dot(cbk, wax_ref[n], preferred_element_type=F32)
        rs.append(ra[:, :w])
        is_.append(ra[:, w:])
    r = jax.nn.sigmoid(jnp.concatenate(rs, axis=1) + ba_ref[...])
    ig = jax.nn.sigmoid(jnp.concatenate(is_, axis=1) + bx_ref[...])
    nl = -lam_ref[...]
    softplus = jnp.maximum(nl, 0.0) + jnp.log(1.0 + jnp.exp(-jnp.abs(nl)))
    log_a = (-RG_C * r) * softplus
    a = jnp.exp(log_a)
    var = 1.0 - jnp.exp(2.0 * log_a)
    xin = jnp.where(var > 0.0, var * lax.rsqrt(var), 0.0) * (ig * conv)

    rowg = lax.broadcasted_iota(jnp.int32, (tm, dr), 0) & (sub - 1)
    sa, sx = a, xin
    dist = 1
    while dist < sub:
        keep = rowg >= dist
        xs = jnp.where(keep, pltpu.roll(sx, dist, 0), 0.0)
        as_ = jnp.where(keep, pltpu.roll(sa, dist, 0), 1.0)
        sx = sx + sa * xs
        sa = sa * as_
        dist *= 2
    carry = hc_ref[...]
    groups = []
    for g in range(tm // sub):
        hg = sx[g * sub:(g + 1) * sub] + sa[g * sub:(g + 1) * sub] * carry
        carry = hg[sub - 1:sub]
        groups.append(hg)
    hs = jnp.concatenate(groups, axis=0)
    hc_ref[...] = carry

    y = (jax.nn.gelu(gate_branch, approximate=True) * hs).astype(BF16)
    out = jnp.dot(y, wout_ref[...], preferred_element_type=F32)
    m = mod_ref[...]
    x3 = x_ref[...] + m[:, 2 * d:3 * d] * _rms(out, g1_ref[...])
    x3_ref[...] = x3
    h4 = _rms(x3, g2_ref[...]) * (1.0 + m[:, 4 * d:5 * d]) + m[:, 3 * d:4 * d]
    h4_ref[...] = h4
    h_hi = h4.astype(BF16)
    h_lo = (h4 - h_hi.astype(F32)).astype(BF16)
    lg_ref[...] = jnp.dot(jnp.concatenate([h_hi, h_hi, h_lo], axis=1), wr_ref[...],
                          preferred_element_type=F32) + br_ref[...]


def _lru(h3, x2, w_in, conv_w, conv_b, wax, b_a, b_x, lam, w_out, mods, g_post, g_pre2, w_r, b_r, tm):
    b, s, d = x2.shape
    dr = w_out.shape[0]
    d6 = mods.shape[-1]
    nb = wax.shape[0]
    tok = lambda bi, i: (bi, i, 0)
    return pl.pallas_call(
        functools.partial(_lru_kernel, d=d, dr=dr, n_rnn_blocks=nb),
        grid=(b, s // tm),
        in_specs=[
            pl.BlockSpec((None, tm, d), tok),
            pl.BlockSpec((None, tm, d), tok),
            _const_spec(w_in.shape),
            _const_spec(conv_w.shape),
            _const_spec((1, dr)),
            _const_spec(wax.shape),
            _const_spec((1, dr)),
            _const_spec((1, dr)),
            _const_spec((1, dr)),
            _const_spec(w_out.shape),
            pl.BlockSpec((None, None, 1, d6), lambda bi, i: (1, bi, 0, 0)),
            _const_spec((1, d)),
            _const_spec((1, d)),
            _const_spec(w_r.shape),
            _const_spec((1, LANES)),
        ],
        out_specs=[pl.BlockSpec((None, tm, d), tok), pl.BlockSpec((None, tm, d), tok),
                   pl.BlockSpec((None, tm, LANES), tok)],
        out_shape=[jax.ShapeDtypeStruct((b, s, d), F32), jax.ShapeDtypeStruct((b, s, d), F32),
                   jax.ShapeDtypeStruct((b, s, LANES), F32)],
        scratch_shapes=[pltpu.VMEM((SUBLANES, dr), F32), pltpu.VMEM((1, dr), F32)],
        compiler_params=_cparams(("parallel", "arbitrary"), VMEM_LIMIT),
        name="lru",
    )(h3, x2, w_in, conv_w, conv_b, wax, b_a, b_x, lam, w_out, mods, g_post, g_pre2, w_r, b_r)


def _moe_kernel(bexp_ref, nused_ref, tok_ref, dst_ref, h_hbm, wg_ref, wu_ref, wd_ref, o_hbm,
                xbuf, ybuf, gsem, ssem, *, n_blocks, dummy_base):
    del bexp_ref
    mb = xbuf.shape[1]
    i = pl.program_id(0)
    nused = nused_ref[0]
    slot = lax.rem(i, 2)
    other = 1 - slot

    def gather_copy(blk, buf, r):
        tok = tok_ref[blk * mb + r]
        return pltpu.make_async_copy(h_hbm.at[pl.ds(tok, 1), :], xbuf.at[buf, pl.ds(r, 1), :], gsem.at[buf])

    def scatter_copy(dst, buf, r):
        return pltpu.make_async_copy(ybuf.at[buf, pl.ds(r, 1), :], o_hbm.at[pl.ds(dst, 1), :], ssem.at[buf])

    def wait_gather(buf):
        pltpu.make_async_copy(h_hbm.at[pl.ds(0, mb), :], xbuf.at[buf], gsem.at[buf]).wait()

    def wait_scatter(buf):
        pltpu.make_async_copy(ybuf.at[buf], o_hbm.at[pl.ds(0, mb), :], ssem.at[buf]).wait()

    @pl.when(i == 0)
    def _():
        def body(r, c):
            gather_copy(0, 0, r).start()
            return c
        lax.fori_loop(0, mb, body, 0, unroll=8)
        ybuf[1] = jnp.zeros(ybuf.shape[1:], F32)

    @pl.when((i >= 1) & (i <= nused))
    def _():
        wait_scatter(slot)

    @pl.when(i < nused)
    def _():
        wait_gather(slot)
        x = xbuf[slot].astype(BF16)
        dff = wg_ref.shape[1]
        acc = jnp.zeros((mb, wd_ref.shape[1]), F32)
        n_chunks = dff // MOE_FF_CHUNK
        per = -(-mb // n_chunks)
        nxt = jnp.minimum(i + 1, n_blocks - 1)
        prv = jnp.maximum(i - 1, 0)
        for c in range(n_chunks):
            for r in range(c * per, min((c + 1) * per, mb)):
                gather_copy(nxt, other, r).start()
                dst = jnp.where(i == 0, dummy_base + r, dst_ref[prv * mb + r])
                scatter_copy(dst, other, r).start(priority=1)
            cs = slice(c * MOE_FF_CHUNK, (c + 1) * MOE_FF_CHUNK)
            a = jnp.dot(x, wg_ref[:, cs], preferred_element_type=F32)
            u = jnp.dot(x, wu_ref[:, cs], preferred_element_type=F32)
            act = (a * jax.nn.sigmoid(a) * u).astype(BF16)
            acc = acc + jnp.dot(act, wd_ref[cs, :], preferred_element_type=F32)
        ybuf[slot] = acc

    @pl.when(i == nused)
    def _():
        wait_gather(slot)

        def body(r, c):
            scatter_copy(dst_ref[(i - 1) * mb + r], other, r).start()
            return c
        lax.fori_loop(0, mb, body, 0, unroll=8)
        wait_scatter(other)


def _moe(block_exp, nused, slot_tok, slot_dst, h4, w_gate, w_up, w_down, n_out_rows):
    n_tok, d = h4.shape
    mb = MOE_BLOCK_ROWS
    cap = slot_tok.shape[0]
    n_blocks = cap // mb
    e, _, dff = w_gate.shape
    single = dict(pipeline_mode=pl.Buffered(1))
    wmap = lambda i, be, nu, st, sd: (be[jnp.minimum(i, n_blocks - 1)], 0, 0)
    grid_spec = pltpu.PrefetchScalarGridSpec(
        num_scalar_prefetch=4,
        grid=(n_blocks + 1,),
        in_specs=[
            pl.BlockSpec(memory_space=pl.ANY),
            pl.BlockSpec((None, d, dff), wmap, **single),
            pl.BlockSpec((None, d, dff), wmap, **single),
            pl.BlockSpec((None, dff, d), wmap, **single),
        ],
        out_specs=pl.BlockSpec(memory_space=pl.ANY),
        scratch_shapes=[pltpu.VMEM((2, mb, d), F32), pltpu.VMEM((2, mb, d), F32),
                        pltpu.SemaphoreType.DMA((2,)), pltpu.SemaphoreType.DMA((2,))],
    )
    return pl.pallas_call(
        functools.partial(_moe_kernel, n_blocks=n_blocks, dummy_base=n_out_rows),
        grid_spec=grid_spec,
        out_shape=jax.ShapeDtypeStruct((n_out_rows + mb, d), F32),
        compiler_params=_cparams(("arbitrary",), VMEM_LIMIT),
        name="moe",
    )(block_exp, nused, slot_tok, slot_dst, h4, w_gate, w_up, w_down)


def _moe_out_kernel(*refs, d):
    y_refs, (gate_ref, x_ref, mod_ref, g_ref, o_ref) = refs[:TOP_K], refs[TOP_K:]
    g = gate_ref[...]
    y = y_refs[0][...] * g[:, 0:1]
    for kk in range(1, TOP_K):
        y = y + y_refs[kk][...] * g[:, kk:kk + 1]
    m = mod_ref[...]
    o_ref[...] = x_ref[...] + m[:, 5 * d:6 * d] * _rms(y, g_ref[...])


def _moe_out(y_rows, gates, x3, mods, g_post, s, tm):
    n_tok, d = x3.shape
    d6 = mods.shape[-1]
    per_b = s // tm
    nt = n_tok // tm
    y_specs = [pl.BlockSpec((tm, d), functools.partial(lambda i, kk: (kk * nt + i, 0), kk=kk))
               for kk in range(TOP_K)]
    return pl.pallas_call(
        functools.partial(_moe_out_kernel, d=d),
        grid=(nt,),
        in_specs=y_specs + [
            pl.BlockSpec((tm, TOP_K), lambda i: (i, 0)),
            pl.BlockSpec((tm, d), lambda i: (i, 0)),
            pl.BlockSpec((None, None, 1, d6), lambda i: (1, i // per_b, 0, 0)),
            pl.BlockSpec((1, d), lambda i: (0, 0)),
        ],
        out_specs=pl.BlockSpec((tm, d), lambda i: (i, 0)),
        out_shape=jax.ShapeDtypeStruct((n_tok, d), F32),
        compiler_params=_cparams(("parallel",), VMEM_LIMIT),
        name="moe_out",
    )(*([y_rows] * TOP_K), gates, x3, mods, g_post)


def _route(logits, n_experts, mb):
    n_tok = logits.shape[0]
    top_logit, top_idx = lax.top_k(logits, TOP_K)
    gates = jax.nn.softmax(top_logit, axis=-1)
    exp_flat = top_idx.reshape(-1).astype(jnp.int32)
    n_asg = n_tok * TOP_K
    onehot = (exp_flat[:, None] == jnp.arange(n_experts, dtype=jnp.int32)[None, :]).astype(jnp.int32)
    csum = jnp.cumsum(onehot, axis=0)
    counts = csum[-1]
    rank = jnp.sum((csum - onehot) * onehot, axis=1)
    padded = ((counts + mb - 1) // mb) * mb
    pends = jnp.cumsum(padded)
    pstarts = pends - padded
    pos = (pstarts[exp_flat] + rank).astype(jnp.int32)
    cap = (-(-n_asg // mb) + n_experts) * mb
    slot_flat = jnp.full((cap,), -1, jnp.int32).at[pos].set(jnp.arange(n_asg, dtype=jnp.int32))
    valid = slot_flat >= 0
    slot_tok = jnp.where(valid, slot_flat // TOP_K, 0)
    slot_dst = jnp.where(valid, (slot_flat % TOP_K) * n_tok + slot_flat // TOP_K,
                         n_asg + jnp.arange(cap, dtype=jnp.int32) % mb)
    n_blocks = cap // mb
    block_starts = jnp.arange(n_blocks, dtype=jnp.int32) * mb
    block_exp = jnp.minimum(jnp.searchsorted(pends, block_starts, side='right'), n_experts - 1).astype(jnp.int32)
    nused = (pends[-1] // mb).astype(jnp.int32).reshape(1)
    return gates, slot_tok, slot_dst, block_exp, nused, n_asg


def _rope_tables(s):
    half = ROPE_DIMS // 2
    inv_freq = jnp.power(ROPE_THETA, -jnp.arange(half, dtype=F32) / half)
    ang = jnp.arange(s, dtype=F32)[:, None] * inv_freq[None, :]
    cos, sin = jnp.cos(ang), jnp.sin(ang)
    lane = jnp.arange(LANES) % HEAD_DIM
    idx = lane % half
    is_x1 = lane < half
    is_x2 = (lane >= half) & (lane < ROPE_DIMS)
    c = jnp.where((is_x1 | is_x2)[None, :], cos[:, idx], 1.0)
    sa = jnp.where(is_x2[None, :], sin[:, idx], 0.0)
    sb = jnp.where(is_x1[None, :], -sin[:, idx], 0.0)
    return c.astype(F32), sa.astype(F32), sb.astype(F32)


def kernel(x, c, w_ada, b_ada, norm_g, attn_w_in, fox_b_f, attn_w_out, ffn_w_gate, ffn_w_up, ffn_w_down,
           lru_w_in, lru_conv_w, lru_conv_b, lru_w_a, lru_b_a, lru_w_x, lru_b_x, lru_lambda, lru_w_out,
           moe_w_router, moe_b_router, moe_w_gate, moe_w_up, moe_w_down):
    b, s, d = x.shape
    aw = attn_w_out.shape[1]
    n_experts = moe_w_router.shape[2]
    assert s % ATT_STEP == 0 and ATT_STEP % MOBA_BLOCK == 0 and ATT_STEP % ATT_SUB == 0
    tm = min(512, s)

    c_pad = jnp.zeros((8, d), F32).at[:b].set(c)
    mods = _ada(c_pad, w_ada, b_ada)[:, :b].reshape(w_ada.shape[0], b, 1, 6 * d)

    w_in = attn_w_in[0]
    w_qkv = w_in[:, :3 * aw].astype(BF16)
    w_f = jnp.zeros((d, LANES), F32).at[:, :N_FOX_HEADS].set(w_in[:, 3 * aw:]).astype(BF16)
    b_f = jnp.zeros((1, LANES), F32).at[0, :N_FOX_HEADS].set(fox_b_f[0])
    rope_c, rope_sa, rope_sb = _rope_tables(s)
    q, k, v, logf = _l0_in(x, mods, norm_g[0, 0][None], w_qkv, w_f, b_f, rope_c, rope_sa, rope_sb, tm)

    n_blk = s // MOBA_BLOCK
    bias_t = _moba_gate(q, k, N_MOBA_HEADS)
    qaux_m = jnp.pad(jnp.swapaxes(bias_t, 2, 3).astype(BF16), ((0, 0), (0, 0), (0, 0), (0, LANES - n_blk)))
    blk_of_pos = jnp.arange(s, dtype=jnp.int32) // MOBA_BLOCK
    kaux_row = jnp.arange(LANES, dtype=jnp.int32)[None, :] == blk_of_pos[:, None]
    kaux_m = jnp.broadcast_to(kaux_row.astype(BF16)[None, None], (1, 2, s, LANES))
    o_m = _attention(q, k, v, qaux_m, kaux_m, N_MOBA_HEADS, 0)

    qaux_f, kaux_f = _fox_prep(logf, N_FOX_HEADS)
    o_f = _attention(q, k, v, qaux_f, kaux_f, N_FOX_HEADS, N_MOBA_HEADS)

    x1, h2 = _l0_out(o_m, o_f, attn_w_out[0].astype(BF16), x, mods, norm_g[0, 1][None], norm_g[0, 2][None], tm)

    x2, h3 = _ffn(h2, x1, ffn_w_gate[0].astype(BF16), ffn_w_up[0].astype(BF16), ffn_w_down[0].astype(BF16),
                  mods, norm_g[0, 3][None], norm_g[1, 0][None], min(256, s))

    wax = jnp.concatenate([lru_w_a[0], lru_w_x[0]], axis=-1).astype(BF16)
    w_r = jnp.zeros((d, LANES), F32).at[:, :n_experts].set(moe_w_router[0])
    w_r_hi = w_r.astype(BF16)
    w_r_lo = (w_r - w_r_hi.astype(F32)).astype(BF16)
    w_r = jnp.concatenate([w_r_hi, w_r_lo, w_r_hi], axis=0)
    b_r = jnp.zeros((1, LANES), F32).at[0, :n_experts].set(moe_b_router[0])
    x3, h4, logits = _lru(h3, x2, lru_w_in[0].astype(BF16), lru_conv_w[0], lru_conv_b[0][None], wax,
                          lru_b_a[0][None], lru_b_x[0][None], lru_lambda[0][None], lru_w_out[0].astype(BF16),
                          mods, norm_g[1, 1][None], norm_g[1, 2][None], w_r, b_r, min(256, s))

    n_tok = b * s
    gates, slot_tok, slot_dst, block_exp, nused, n_rows = _route(
        logits.reshape(n_tok, LANES)[:, :n_experts], n_experts, MOE_BLOCK_ROWS)
    y_rows = _moe(block_exp, nused, slot_tok, slot_dst, h4.reshape(n_tok, d),
                  moe_w_gate[0].astype(BF16), moe_w_up[0].astype(BF16), moe_w_down[0].astype(BF16), n_rows)
    out = _moe_out(y_rows, gates, x3.reshape(n_tok, d), mods, norm_g[1, 3][None], s, min(256, s))
    return out.reshape(b, s, d)
```

```python
import functools

import jax
import jax.numpy as jnp
from jax import lax
from jax.experimental import pallas as pl
from jax.experimental.pallas import tpu as pltpu

F32 = jnp.float32
BF16 = jnp.bfloat16
HIGHEST = lax.Precision.HIGHEST

NORM_EPS = 1e-6
HEAD_DIM = 64
N_MOBA_HEADS = 8
N_FOX_HEADS = 8
ROPE_DIMS = 16
ROPE_THETA = 500000.0
MOBA_BLOCK = 256
MOBA_TOPK = 3
CONV_WIDTH = 4
RG_C = 8.0
TOP_K = 2

LANES = 128
SUBLANES = 8
FOX_PREP_ROWS = 1024
ATT_QUERIES = 1024
ATT_KEYS = 512
ATT_SUB = 256
ATT_LOOKAHEAD = 2
NEG = -1e30
LOG2E = 1.4426950408889634
MOE_BLOCK_ROWS = 512
MOE_FF_CHUNK = 512
VMEM_LIMIT = 56 * 1024 * 1024


def _cparams(sem, vmem=None):
    return pltpu.CompilerParams(dimension_semantics=sem, vmem_limit_bytes=vmem)


def _rms(x, g):
    return x * lax.rsqrt(jnp.mean(x * x, axis=-1, keepdims=True) + NORM_EPS) * g


def _const_spec(shape):
    n = len(shape)
    return pl.BlockSpec(shape, lambda *_: (0,) * n)


def _ada_kernel(c_ref, w_ref, b_ref, o_ref):
    c = c_ref[...]
    cond = c * jax.nn.sigmoid(c)
    o_ref[...] = jnp.dot(cond, w_ref[...], preferred_element_type=F32, precision=HIGHEST) + b_ref[...]


def _ada(c_pad, w_ada, b_ada):
    depth, d, d6 = w_ada.shape
    rows = c_pad.shape[0]
    nj = d6 // d
    return pl.pallas_call(
        _ada_kernel,
        grid=(depth, nj),
        in_specs=[
            pl.BlockSpec((rows, d), lambda l, j: (0, 0)),
            pl.BlockSpec((None, d, d), lambda l, j: (l, 0, j)),
            pl.BlockSpec((None, 1, d), lambda l, j: (l, 0, j)),
        ],
        out_specs=pl.BlockSpec((None, rows, d), lambda l, j: (l, 0, j)),
        out_shape=jax.ShapeDtypeStruct((depth, rows, d6), F32),
        compiler_params=_cparams(("parallel", "parallel")),
        name="ada",
    )(c_pad, w_ada, b_ada.reshape(depth, 1, d6))


def _l0_in_kernel(x_ref, mod_ref, g_ref, w_ref, wf_ref, bf_ref, rc_ref, rsa_ref, rsb_ref,
                  q_ref, k_ref, v_ref, lf_ref, *, d, aw, n_rope_chunks):
    m = mod_ref[...]
    h = _rms(x_ref[...], g_ref[...]) * (1.0 + m[:, d:2 * d]) + m[:, 0:d]
    hb = h.astype(BF16)
    proj = jnp.dot(hb, w_ref[...], preferred_element_type=F32)
    rc, rsa, rsb = rc_ref[...], rsa_ref[...], rsb_ref[...]

    def rope(t):
        return t * rc + pltpu.roll(t, ROPE_DIMS // 2, 1) * rsa + pltpu.roll(t, LANES - ROPE_DIMS // 2, 1) * rsb

    scale = HEAD_DIM ** -0.5 * LOG2E
    for c in range(aw // LANES):
        sl = slice(c * LANES, (c + 1) * LANES)
        qc = proj[:, sl] * scale
        kc = proj[:, aw + c * LANES:aw + (c + 1) * LANES]
        if c < n_rope_chunks:
            qc, kc = rope(qc), rope(kc)
        q_ref[:, sl] = qc.astype(BF16)
        k_ref[:, sl] = kc.astype(BF16)
    v_ref[...] = proj[:, 2 * aw:3 * aw].astype(BF16)
    fl = jnp.dot(hb, wf_ref[...], preferred_element_type=F32) + bf_ref[...]
    lf_ref[...] = jnp.minimum(fl, 0.0) - jnp.log(1.0 + jnp.exp(-jnp.abs(fl)))


def _l0_in(x, mods, g_pre, w_qkv, w_f, b_f, rope_c, rope_sa, rope_sb, tm):
    b, s, d = x.shape
    aw = w_qkv.shape[1] // 3
    d6 = mods.shape[-1]
    tok = lambda bi, i: (bi, i, 0)
    kern = functools.partial(_l0_in_kernel, d=d, aw=aw,
                             n_rope_chunks=N_MOBA_HEADS * HEAD_DIM // LANES)
    return pl.pallas_call(
        kern,
        grid=(b, s // tm),
        in_specs=[
            pl.BlockSpec((None, tm, d), tok),
            pl.BlockSpec((None, None, 1, d6), lambda bi, i: (0, bi, 0, 0)),
            _const_spec((1, d)),
            _const_spec(w_qkv.shape),
            _const_spec(w_f.shape),
            _const_spec((1, LANES)),
            pl.BlockSpec((tm, LANES), lambda bi, i: (i, 0)),
            pl.BlockSpec((tm, LANES), lambda bi, i: (i, 0)),
            pl.BlockSpec((tm, LANES), lambda bi, i: (i, 0)),
        ],
        out_specs=[
            pl.BlockSpec((None, tm, aw), tok),
            pl.BlockSpec((None, tm, aw), tok),
            pl.BlockSpec((None, tm, aw), tok),
            pl.BlockSpec((None, tm, LANES), tok),
        ],
        out_shape=[
            jax.ShapeDtypeStruct((b, s, aw), BF16),
            jax.ShapeDtypeStruct((b, s, aw), BF16),
            jax.ShapeDtypeStruct((b, s, aw), BF16),
            jax.ShapeDtypeStruct((b, s, LANES), F32),
        ],
        compiler_params=_cparams(("parallel", "parallel"), VMEM_LIMIT),
        name="l0_in",
    )(x, mods, g_pre, w_qkv, w_f, b_f, rope_c, rope_sa, rope_sb)


def _fox_prep_kernel(lf_ref, qa_ref, ka_ref, carry_ref, *, n_heads):
    t = lf_ref.shape[0]

    @pl.when(pl.program_id(1) == 0)
    def _():
        carry_ref[...] = jnp.zeros_like(carry_ref)

    row = lax.broadcasted_iota(jnp.int32, (t, t), 0)
    col = lax.broadcasted_iota(jnp.int32, (t, t), 1)
    tri = (col <= row).astype(F32)
    cum = jnp.dot(tri, lf_ref[...], preferred_element_type=F32, precision=HIGHEST) + carry_ref[...]
    carry_ref[...] = cum[t - 1:t, :]
    lane = lax.broadcasted_iota(jnp.int32, (t, LANES), 1)
    for h in range(n_heads):
        c = jnp.broadcast_to(cum[:, h:h + 1], (t, LANES)) * LOG2E
        hi = c.astype(BF16).astype(F32)
        r1 = c - hi
        mid = r1.astype(BF16).astype(F32)
        lo = r1 - mid
        qa = jnp.where(lane == 0, hi, jnp.where(lane == 1, mid, jnp.where(lane == 2, lo,
                       jnp.where(lane < 6, 1.0, 0.0))))
        ka = jnp.where(lane < 3, 1.0, jnp.where(lane == 3, -hi, jnp.where(lane == 4, -mid,
                       jnp.where(lane == 5, -lo, 0.0))))
        qa_ref[h] = qa.astype(BF16)
        ka_ref[h] = ka.astype(BF16)


def _fox_prep(logf, n_heads):
    b, s, _ = logf.shape
    t = min(FOX_PREP_ROWS, s)
    aux = jax.ShapeDtypeStruct((b, n_heads, s, LANES), BF16)
    aux_spec = pl.BlockSpec((None, n_heads, t, LANES), lambda bi, i: (bi, 0, i, 0))
    return pl.pallas_call(
        functools.partial(_fox_prep_kernel, n_heads=n_heads),
        grid=(b, s // t),
        in_specs=[pl.BlockSpec((None, t, LANES), lambda bi, i: (bi, i, 0))],
        out_specs=[aux_spec, aux_spec],
        out_shape=[aux, aux],
        scratch_shapes=[pltpu.VMEM((1, LANES), F32)],
        compiler_params=_cparams(("parallel", "arbitrary")),
        name="fox_prep",
    )(logf)


def _moba_gate_kernel(q_ref, k_ref, bt_ref, *, n_blk):
    s = k_ref.shape[0]
    t = MOBA_BLOCK
    rowi = lax.broadcasted_iota(jnp.int32, (n_blk, s), 0)
    cs = lax.broadcasted_iota(jnp.int32, (n_blk, s), 1)
    own = (cs >= rowi * t) & (cs < (rowi + 1) * t)
    km = jnp.dot(jnp.where(own, 1.0 / t, 0.0).astype(BF16), k_ref[...], preferred_element_type=F32)
    lane_k = lax.broadcasted_iota(jnp.int32, (n_blk, LANES), 1)
    km2 = jnp.concatenate([jnp.where(lane_k < HEAD_DIM, km, 0.0), jnp.where(lane_k >= HEAD_DIM, km, 0.0)], axis=0)
    hi = km2.astype(BF16)
    r1 = km2 - hi.astype(F32)
    mid = r1.astype(BF16)
    lo = (r1 - mid.astype(F32)).astype(BF16)
    g3 = lax.dot_general(jnp.concatenate([hi, mid, lo], axis=0), q_ref[...], (((1,), (1,)), ((), ())),
                         preferred_element_type=F32)
    g = g3[0:2 * n_blk] + g3[2 * n_blk:4 * n_blk] + g3[4 * n_blk:6 * n_blk]
    fully_past = (rowi + 1) * t <= cs
    for hh in range(2):
        gv = jnp.where(fully_past, g[hh * n_blk:(hh + 1) * n_blk], -jnp.inf)
        keep = own
        for _ in range(MOBA_TOPK):
            mx = jnp.max(gv, axis=0, keepdims=True)
            cand = jnp.where((gv == mx) & (mx > -jnp.inf), rowi, n_blk)
            pick = rowi == jnp.min(cand, axis=0, keepdims=True)
            keep = keep | pick
            gv = jnp.where(pick, -jnp.inf, gv)
        bt_ref[hh] = jnp.where(keep, 0.0, NEG)


def _moba_gate(q, k, n_heads):
    b, s, _ = q.shape
    n_blk = s // MOBA_BLOCK
    assert n_blk <= LANES
    return pl.pallas_call(
        functools.partial(_moba_gate_kernel, n_blk=n_blk),
        grid=(b, n_heads // 2),
        in_specs=[
            pl.BlockSpec((None, s, LANES), lambda bi, p: (bi, 0, p)),
            pl.BlockSpec((None, s, LANES), lambda bi, p: (bi, 0, p)),
        ],
        out_specs=pl.BlockSpec((None, 2, n_blk, s), lambda bi, p: (bi, p, 0, 0)),
        out_shape=jax.ShapeDtypeStruct((b, n_heads, n_blk, s), F32),
        compiler_params=_cparams(("parallel", "parallel")),
        name="moba_gate",
    )(q, k)


def _attn_kernel(q_ref, k_ref, v_ref, qa_ref, ka_ref, o_ref):
    tq = q_ref.shape[0]
    t = ATT_KEYS
    n_sub = tq // ATT_SUB
    n_diag = tq // t
    g = pl.program_id(2)
    lane = lax.broadcasted_iota(jnp.int32, (ATT_SUB, LANES), 1)
    lane_k = lax.broadcasted_iota(jnp.int32, (t, LANES), 1)
    row = lax.broadcasted_iota(jnp.int32, (ATT_SUB, t), 0)
    col = lax.broadcasted_iota(jnp.int32, (ATT_SUB, t), 1)
    nt = (((1,), (1,)), ((), ()))
    qas = []
    for hh in range(2):
        in_head = (lane < HEAD_DIM) if hh == 0 else (lane >= HEAD_DIM)
        for r in range(n_sub):
            rs = slice(r * ATT_SUB, (r + 1) * ATT_SUB)
            q = q_ref[rs, :]
            qas.append(jnp.concatenate([jnp.where(in_head, q, jnp.zeros_like(q)), qa_ref[hh, rs, :]], axis=1))

    def update(state, j, d):
        rows = pl.ds(pl.multiple_of(j * t, t), t)
        k = k_ref[rows, :]
        v = v_ref[rows, :]
        kks = [jnp.concatenate([k, ka_ref[hh, rows, :]], axis=1) for hh in range(2)]
        vhs = [jnp.where((lane_k < HEAD_DIM) if hh == 0 else (lane_k >= HEAD_DIM), v, jnp.ones_like(v))
               for hh in range(2)]

        def q_off(c):
            return (c % n_sub) * ATT_SUB

        active = [c for c in range(2 * n_sub) if d is None or q_off(c) + ATT_SUB > d * t]

        def qk(c):
            return lax.dot_general(qas[c], kks[c // n_sub], nt, preferred_element_type=F32)

        scs = {c: qk(c) for c in active[:ATT_LOOKAHEAD]}
        new = list(state)
        for n, c in enumerate(active):
            if n + ATT_LOOKAHEAD < len(active):
                nxt = active[n + ATT_LOOKAHEAD]
                scs[nxt] = qk(nxt)
            m_old, acc = state[c]
            sc = scs.pop(c)
            if d is not None and q_off(c) < (d + 1) * t:
                sc = jnp.where(col + d * t <= row + q_off(c), sc, NEG)
            m_new = jnp.maximum(m_old, jnp.max(sc, axis=-1, keepdims=True))
            p = jnp.exp2(sc - m_new).astype(BF16)
            acc = jnp.exp2(m_old - m_new) * acc + jnp.dot(p, vhs[c // n_sub], preferred_element_type=F32)
            new[c] = (m_new, acc)
        return tuple(new)

    init = tuple((jnp.full((ATT_SUB, 1), NEG, F32), jnp.zeros((ATT_SUB, LANES), F32)) for _ in range(2 * n_sub))
    state = lax.fori_loop(0, g * n_diag, lambda j, st: update(st, j, None), init)
    for d in range(n_diag):
        state = update(state, g * n_diag + d, d)
    for r in range(n_sub):
        outs = [state[hh * n_sub + r][1] for hh in range(2)]
        outs = [acc / pltpu.roll(acc, HEAD_DIM, 1) for acc in outs]
        o_ref[r * ATT_SUB:(r + 1) * ATT_SUB, :] = jnp.where(lane < HEAD_DIM, outs[0], outs[1]).astype(o_ref.dtype)


def _attention(q, k, v, qaux, kaux, n_heads, head_off):
    b, s, _ = q.shape
    t = ATT_QUERIES
    po = head_off // 2
    kb, kh = kaux.shape[0], kaux.shape[1]
    ka_map = (lambda bi, p, i: (bi, p, 0, 0)) if kb == b and kh == n_heads else (lambda bi, p, i: (0, 0, 0, 0))
    return pl.pallas_call(
        _attn_kernel,
        grid=(b, n_heads // 2, s // t),
        in_specs=[
            pl.BlockSpec((None, t, LANES), lambda bi, p, i: (bi, i, p + po)),
            pl.BlockSpec((None, s, LANES), lambda bi, p, i: (bi, 0, p + po)),
            pl.BlockSpec((None, s, LANES), lambda bi, p, i: (bi, 0, p + po)),
            pl.BlockSpec((None, 2, t, LANES), lambda bi, p, i: (bi, p, i, 0)),
            pl.BlockSpec((None, 2, s, LANES), ka_map),
        ],
        out_specs=pl.BlockSpec((None, t, LANES), lambda bi, p, i: (bi, i, p)),
        out_shape=jax.ShapeDtypeStruct((b, s, n_heads * HEAD_DIM), BF16),
        compiler_params=_cparams(("parallel", "parallel", "arbitrary")),
        name="attn",
    )(q, k, v, qaux, kaux)


def _l0_out_kernel(om_ref, of_ref, w_ref, x_ref, mod_ref, g1_ref, g2_ref, x1_ref, h2_ref, *, d):
    m = mod_ref[...]
    hw = om_ref.shape[1]
    y = (jnp.dot(om_ref[...], w_ref[0:hw, :], preferred_element_type=F32)
         + jnp.dot(of_ref[...], w_ref[hw:, :], preferred_element_type=F32))
    x1 = x_ref[...] + m[:, 2 * d:3 * d] * _rms(y, g1_ref[...])
    x1_ref[...] = x1
    h2_ref[...] = (_rms(x1, g2_ref[...]) * (1.0 + m[:, 4 * d:5 * d]) + m[:, 3 * d:4 * d]).astype(BF16)


def _l0_out(o_m, o_f, w_out, x, mods, g_post, g_pre2, tm):
    b, s, d = x.shape
    d6 = mods.shape[-1]
    tok = lambda bi, i: (bi, i, 0)
    return pl.pallas_call(
        functools.partial(_l0_out_kernel, d=d),
        grid=(b, s // tm),
        in_specs=[
            pl.BlockSpec((None, tm, o_m.shape[2]), tok),
            pl.BlockSpec((None, tm, o_f.shape[2]), tok),
            _const_spec(w_out.shape),
            pl.BlockSpec((None, tm, d), tok),
            pl.BlockSpec((None, None, 1, d6), lambda bi, i: (0, bi, 0, 0)),
            _const_spec((1, d)),
            _const_spec((1, d)),
        ],
        out_specs=[pl.BlockSpec((None, tm, d), tok), pl.BlockSpec((None, tm, d), tok)],
        out_shape=[jax.ShapeDtypeStruct((b, s, d), F32), jax.ShapeDtypeStruct((b, s, d), BF16)],
        compiler_params=_cparams(("parallel", "parallel"), VMEM_LIMIT),
        name="l0_out",
    )(o_m, o_f, w_out, x, mods, g_post, g_pre2)


def _ffn_kernel(h_ref, x_ref, wg_ref, wu_ref, wd_ref, mod0_ref, mod1_ref, g1_ref, g2_ref,
                x2_ref, h3_ref, *, d):
    h = h_ref[...]
    a = jnp.dot(h, wg_ref[...], preferred_element_type=F32)
    u = jnp.dot(h, wu_ref[...], preferred_element_type=F32)
    act = (a * jax.nn.sigmoid(a) * u).astype(BF16)
    y = jnp.dot(act, wd_ref[...], preferred_element_type=F32)
    m0 = mod0_ref[...]
    m1 = mod1_ref[...]
    x2 = x_ref[...] + m0[:, 5 * d:6 * d] * _rms(y, g1_ref[...])
    x2_ref[...] = x2
    h3_ref[...] = (_rms(x2, g2_ref[...]) * (1.0 + m1[:, d:2 * d]) + m1[:, 0:d]).astype(BF16)


def _ffn(h2, x1, w_gate, w_up, w_down, mods, g_post, g_pre_next, tm):
    b, s, d = x1.shape
    d6 = mods.shape[-1]
    tok = lambda bi, i: (bi, i, 0)
    single = dict(pipeline_mode=pl.Buffered(1))
    return pl.pallas_call(
        functools.partial(_ffn_kernel, d=d),
        grid=(b, s // tm),
        in_specs=[
            pl.BlockSpec((None, tm, d), tok),
            pl.BlockSpec((None, tm, d), tok),
            pl.BlockSpec(w_gate.shape, lambda bi, i: (0, 0), **single),
            pl.BlockSpec(w_up.shape, lambda bi, i: (0, 0), **single),
            pl.BlockSpec(w_down.shape, lambda bi, i: (0, 0), **single),
            pl.BlockSpec((None, None, 1, d6), lambda bi, i: (0, bi, 0, 0)),
            pl.BlockSpec((None, None, 1, d6), lambda bi, i: (1, bi, 0, 0)),
            _const_spec((1, d)),
            _const_spec((1, d)),
        ],
        out_specs=[pl.BlockSpec((None, tm, d), tok), pl.BlockSpec((None, tm, d), tok)],
        out_shape=[jax.ShapeDtypeStruct((b, s, d), F32), jax.ShapeDtypeStruct((b, s, d), BF16)],
        compiler_params=_cparams(("parallel", "parallel"), VMEM_LIMIT),
        name="ffn",
    )(h2, x1, w_gate, w_up, w_down, mods, mods, g_post, g_pre_next)


def _lru_kernel(h_ref, x_ref, win_ref, cw_ref, cb_ref, wax_ref, ba_ref, bx_ref, lam_ref, wout_ref,
                mod_ref, g1_ref, g2_ref, wr_ref, br_ref,
                x3_ref, h4_ref, lg_ref, utail_ref, hc_ref, *, d, dr, n_rnn_blocks):
    tm = h_ref.shape[0]
    sub = utail_ref.shape[0]

    @pl.when(pl.program_id(1) == 0)
    def _():
        utail_ref[...] = jnp.zeros_like(utail_ref)
        hc_ref[...] = jnp.zeros_like(hc_ref)

    proj = jnp.dot(h_ref[...], win_ref[...], preferred_element_type=F32)
    gate_branch = proj[:, :dr]
    u = proj[:, dr:]
    row_sub = lax.broadcasted_iota(jnp.int32, (sub, dr), 0)
    n_groups = tm // sub
    taps = [cw_ref[j:j + 1, :] for j in range(CONV_WIDTH)]
    bias = cb_ref[...]
    prev = utail_ref[...]
    rolled_prev = [None] + [pltpu.roll(prev, back, 0) for back in range(1, CONV_WIDTH)]
    conv_groups = []
    for g in range(n_groups):
        ug = u[g * sub:(g + 1) * sub]
        cg = bias + ug * taps[CONV_WIDTH - 1]
        for back in range(1, CONV_WIDTH):
            rolled = pltpu.roll(ug, back, 0)
            cg = cg + jnp.where(row_sub >= back, rolled, rolled_prev[back]) * taps[CONV_WIDTH - 1 - back]
            rolled_prev[back] = rolled
        conv_groups.append(cg)
    conv = jnp.concatenate(conv_groups, axis=0)
    utail_ref[...] = u[tm - sub:tm]

    w = dr // n_rnn_blocks
    rs, is_ = [], []
    for n in range(n_rnn_blocks):
        cbk = conv[:, n * w:(n + 1) * w].astype(BF16)
        ra = jnp.dot(cbk, wax_ref[n], preferred_element_type=F32)
        rs.append(ra[:, :w])
        is_.append(ra[:, w:])
    r = jax.nn.sigmoid(jnp.concatenate(rs, axis=1) + ba_ref[...])
    ig = jax.nn.sigmoid(jnp.concatenate(is_, axis=1) + bx_ref[...])
    nl = -lam_ref[...]
    softplus = jnp.maximum(nl, 0.0) + jnp.log(1.0 + jnp.exp(-jnp.abs(nl)))
    log_a = (-RG_C * r) * softplus
    a = jnp.exp(log_a)
    xin = jnp.exp(0.5 * jnp.log(1.0 - a * a)) * (ig * conv)

    carry = hc_ref[...]
    groups = []
    for g in range(n_groups):
        sa, sx = a[g * sub:(g + 1) * sub], xin[g * sub:(g + 1) * sub]
        dist = 1
        while dist < sub:
            keep = row_sub >= dist
            xs = jnp.where(keep, pltpu.roll(sx, dist, 0), 0.0)
            as_ = jnp.where(keep, pltpu.roll(sa, dist, 0), 1.0)
            sx = sx + sa * xs
            sa = sa * as_
            dist *= 2
        hg = sx + sa * carry
        carry = hg[sub - 1:sub]
        groups.append(hg)
    hs = jnp.concatenate(groups, axis=0)
    hc_ref[...] = carry

    y = (jax.nn.gelu(gate_branch, approximate=True) * hs).astype(BF16)
    out = jnp.dot(y, wout_ref[...], preferred_element_type=F32)
    m = mod_ref[...]
    x3 = x_ref[...] + m[:, 2 * d:3 * d] * _rms(out, g1_ref[...])
    x3_ref[...] = x3
    h4 = _rms(x3, g2_ref[...]) * (1.0 + m[:, 4 * d:5 * d]) + m[:, 3 * d:4 * d]
    h4_ref[...] = h4
    h_hi = h4.astype(BF16)
    h_lo = (h4 - h_hi.astype(F32)).astype(BF16)
    lg_ref[...] = jnp.dot(jnp.concatenate([h_hi, h_hi, h_lo], axis=1), wr_ref[...],
                          preferred_element_type=F32) + br_ref[...]


def _lru(h3, x2, w_in, conv_w, conv_b, wax, b_a, b_x, lam, w_out, mods, g_post, g_pre2, w_r, b_r, tm):
    b, s, d = x2.shape
    dr = w_out.shape[0]
    d6 = mods.shape[-1]
    nb = wax.shape[0]
    tok = lambda bi, i: (bi, i, 0)
    return pl.pallas_call(
        functools.partial(_lru_kernel, d=d, dr=dr, n_rnn_blocks=nb),
        grid=(b, s // tm),
        in_specs=[
            pl.BlockSpec((None, tm, d), tok),
            pl.BlockSpec((None, tm, d), tok),
            _const_spec(w_in.shape),
            _const_spec(conv_w.shape),
            _const_spec((1, dr)),
            _const_spec(wax.shape),
            _const_spec((1, dr)),
            _const_spec((1, dr)),
            _const_spec((1, dr)),
            _const_spec(w_out.shape),
            pl.BlockSpec((None, None, 1, d6), lambda bi, i: (1, bi, 0, 0)),
            _const_spec((1, d)),
            _const_spec((1, d)),
            _const_spec(w_r.shape),
            _const_spec((1, LANES)),
        ],
        out_specs=[pl.BlockSpec((None, tm, d), tok), pl.BlockSpec((None, tm, d), tok),
                   pl.BlockSpec((None, tm, LANES), tok)],
        out_shape=[jax.ShapeDtypeStruct((b, s, d), F32), jax.ShapeDtypeStruct((b, s, d), F32),
                   jax.ShapeDtypeStruct((b, s, LANES), F32)],
        scratch_shapes=[pltpu.VMEM((SUBLANES, dr), F32), pltpu.VMEM((1, dr), F32)],
        compiler_params=_cparams(("parallel", "arbitrary"), VMEM_LIMIT),
        name="lru",
    )(h3, x2, w_in, conv_w, conv_b, wax, b_a, b_x, lam, w_out, mods, g_post, g_pre2, w_r, b_r)


def _moe_kernel(bexp_ref, nused_ref, tok_ref, dst_ref, h_hbm, wg_ref, wu_ref, wd_ref, o_hbm,
                xbuf, ybuf, gsem, ssem, *, n_blocks, dummy_base):
    del bexp_ref
    mb = xbuf.shape[1]
    i = pl.program_id(0)
    nused = nused_ref[0]
    slot = lax.rem(i, 2)
    other = 1 - slot

    def gather_copy(blk, buf, r):
        tok = tok_ref[blk * mb + r]
        return pltpu.make_async_copy(h_hbm.at[pl.ds(tok, 1), :], xbuf.at[buf, pl.ds(r, 1), :], gsem.at[buf])

    def scatter_copy(dst, buf, r):
        return pltpu.make_async_copy(ybuf.at[buf, pl.ds(r, 1), :], o_hbm.at[pl.ds(dst, 1), :], ssem.at[buf])

    def wait_gather(buf):
        pltpu.make_async_copy(h_hbm.at[pl.ds(0, mb), :], xbuf.at[buf], gsem.at[buf]).wait()

    def wait_scatter(buf):
        pltpu.make_async_copy(ybuf.at[buf], o_hbm.at[pl.ds(0, mb), :], ssem.at[buf]).wait()

    @pl.when(i == 0)
    def _():
        def body(r, c):
            gather_copy(0, 0, r).start()
            return c
        lax.fori_loop(0, mb, body, 0, unroll=8)
        ybuf[1] = jnp.zeros(ybuf.shape[1:], F32)

    @pl.when((i >= 1) & (i <= nused))
    def _():
        wait_scatter(slot)

    @pl.when(i < nused)
    def _():
        wait_gather(slot)
        x = xbuf[slot].astype(BF16)
        dff = wg_ref.shape[1]
        acc = jnp.zeros((mb, wd_ref.shape[1]), F32)
        n_chunks = dff // MOE_FF_CHUNK
        per = -(-mb // n_chunks)
        nxt = jnp.minimum(i + 1, n_blocks - 1)
        prv = jnp.maximum(i - 1, 0)
        for c in range(n_chunks):
            for r in range(c * per, min((c + 1) * per, mb)):
                gather_copy(nxt, other, r).start()
                dst = jnp.where(i == 0, dummy_base + r, dst_ref[prv * mb + r])
                scatter_copy(dst, other, r).start(priority=1)
            cs = slice(c * MOE_FF_CHUNK, (c + 1) * MOE_FF_CHUNK)
            a = jnp.dot(x, wg_ref[:, cs], preferred_element_type=F32)
            u = jnp.dot(x, wu_ref[:, cs], preferred_element_type=F32)
            act = (a * jax.nn.sigmoid(a) * u).astype(BF16)
            acc = acc + jnp.dot(act, wd_ref[cs, :], preferred_element_type=F32)
        ybuf[slot] = acc

    @pl.when(i == nused)
    def _():
        wait_gather(slot)

        def body(r, c):
            scatter_copy(dst_ref[(i - 1) * mb + r], other, r).start()
            return c
        lax.fori_loop(0, mb, body, 0, unroll=8)
        wait_scatter(other)


def _moe(block_exp, nused, slot_tok, slot_dst, h4, w_gate, w_up, w_down, n_out_rows):
    n_tok, d = h4.shape
    mb = MOE_BLOCK_ROWS
    cap = slot_tok.shape[0]
    n_blocks = cap // mb
    e, _, dff = w_gate.shape
    single = dict(pipeline_mode=pl.Buffered(1))
    wmap = lambda i, be, nu, st, sd: (be[jnp.minimum(i, n_blocks - 1)], 0, 0)
    grid_spec = pltpu.PrefetchScalarGridSpec(
        num_scalar_prefetch=4,
        grid=(n_blocks + 1,),
        in_specs=[
            pl.BlockSpec(memory_space=pl.ANY),
            pl.BlockSpec((None, d, dff), wmap, **single),
            pl.BlockSpec((None, d, dff), wmap, **single),
            pl.BlockSpec((None, dff, d), wmap, **single),
        ],
        out_specs=pl.BlockSpec(memory_space=pl.ANY),
        scratch_shapes=[pltpu.VMEM((2, mb, d), F32), pltpu.VMEM((2, mb, d), F32),
                        pltpu.SemaphoreType.DMA((2,)), pltpu.SemaphoreType.DMA((2,))],
    )
    return pl.pallas_call(
        functools.partial(_moe_kernel, n_blocks=n_blocks, dummy_base=n_out_rows),
        grid_spec=grid_spec,
        out_shape=jax.ShapeDtypeStruct((n_out_rows + mb, d), F32),
        compiler_params=_cparams(("arbitrary",), VMEM_LIMIT),
        name="moe",
    )(block_exp, nused, slot_tok, slot_dst, h4, w_gate, w_up, w_down)


def _moe_out_kernel(*refs, d):
    y_refs, (gate_ref, x_ref, mod_ref, g_ref, o_ref) = refs[:TOP_K], refs[TOP_K:]
    g = gate_ref[...]
    y = y_refs[0][...] * g[:, 0:1]
    for kk in range(1, TOP_K):
        y = y + y_refs[kk][...] * g[:, kk:kk + 1]
    m = mod_ref[...]
    o_ref[...] = x_ref[...] + m[:, 5 * d:6 * d] * _rms(y, g_ref[...])


def _moe_out(y_rows, gates, x3, mods, g_post, s, tm):
    n_tok, d = x3.shape
    d6 = mods.shape[-1]
    per_b = s // tm
    nt = n_tok // tm
    y_specs = [pl.BlockSpec((tm, d), functools.partial(lambda i, kk: (kk * nt + i, 0), kk=kk))
               for kk in range(TOP_K)]
    return pl.pallas_call(
        functools.partial(_moe_out_kernel, d=d),
        grid=(nt,),
        in_specs=y_specs + [
            pl.BlockSpec((tm, TOP_K), lambda i: (i, 0)),
            pl.BlockSpec((tm, d), lambda i: (i, 0)),
            pl.BlockSpec((None, None, 1, d6), lambda i: (1, i // per_b, 0, 0)),
            pl.BlockSpec((1, d), lambda i: (0, 0)),
        ],
        out_specs=pl.BlockSpec((tm, d), lambda i: (i, 0)),
        out_shape=jax.ShapeDtypeStruct((n_tok, d), F32),
        compiler_params=_cparams(("parallel",), VMEM_LIMIT),
        name="moe_out",
    )(*([y_rows] * TOP_K), gates, x3, mods, g_post)


def _route(logits, n_experts, mb):
    n_tok = logits.shape[0]
    top_logit, top_idx = lax.top_k(logits, TOP_K)
    gates = jax.nn.softmax(top_logit, axis=-1)
    exp_flat = top_idx.reshape(-1).astype(jnp.int32)
    n_asg = n_tok * TOP_K
    onehot = (exp_flat[:, None] == jnp.arange(n_experts, dtype=jnp.int32)[None, :]).astype(jnp.int32)
    csum = jnp.cumsum(onehot, axis=0)
    counts = csum[-1]
    rank = jnp.sum((csum - onehot) * onehot, axis=1)
    padded = ((counts + mb - 1) // mb) * mb
    pends = jnp.cumsum(padded)
    pstarts = pends - padded
    pos = (pstarts[exp_flat] + rank).astype(jnp.int32)
    cap = (-(-n_asg // mb) + n_experts) * mb
    slot_flat = jnp.full((cap,), -1, jnp.int32).at[pos].set(jnp.arange(n_asg, dtype=jnp.int32))
    valid = slot_flat >= 0
    slot_tok = jnp.where(valid, slot_flat // TOP_K, 0)
    slot_dst = jnp.where(valid, (slot_flat % TOP_K) * n_tok + slot_flat // TOP_K,
                         n_asg + jnp.arange(cap, dtype=jnp.int32) % mb)
    n_blocks = cap // mb
    block_starts = jnp.arange(n_blocks, dtype=jnp.int32) * mb
    block_exp = jnp.minimum(jnp.searchsorted(pends, block_starts, side='right'), n_experts - 1).astype(jnp.int32)
    nused = (pends[-1] // mb).astype(jnp.int32).reshape(1)
    return gates, slot_tok, slot_dst, block_exp, nused, n_asg


def _rope_tables(s):
    half = ROPE_DIMS // 2
    inv_freq = jnp.power(ROPE_THETA, -jnp.arange(half, dtype=F32) / half)
    ang = jnp.arange(s, dtype=F32)[:, None] * inv_freq[None, :]
    cos, sin = jnp.cos(ang), jnp.sin(ang)
    lane = jnp.arange(LANES) % HEAD_DIM
    idx = lane % half
    is_x1 = lane < half
    is_x2 = (lane >= half) & (lane < ROPE_DIMS)
    c = jnp.where((is_x1 | is_x2)[None, :], cos[:, idx], 1.0)
    sa = jnp.where(is_x2[None, :], sin[:, idx], 0.0)
    sb = jnp.where(is_x1[None, :], -sin[:, idx], 0.0)
    return c.astype(F32), sa.astype(F32), sb.astype(F32)


def kernel(x, c, w_ada, b_ada, norm_g, attn_w_in, fox_b_f, attn_w_out, ffn_w_gate, ffn_w_up, ffn_w_down,
           lru_w_in, lru_conv_w, lru_conv_b, lru_w_a, lru_b_a, lru_w_x, lru_b_x, lru_lambda, lru_w_out,
           moe_w_router, moe_b_router, moe_w_gate, moe_w_up, moe_w_down):
    b, s, d = x.shape
    aw = attn_w_out.shape[1]
    n_experts = moe_w_router.shape[2]
    assert s % ATT_QUERIES == 0 and ATT_QUERIES % ATT_KEYS == 0 and ATT_KEYS % MOBA_BLOCK == 0
    assert ATT_QUERIES % ATT_SUB == 0
    tm = min(512, s)

    c_pad = jnp.zeros((8, d), F32).at[:b].set(c)
    mods = _ada(c_pad, w_ada, b_ada)[:, :b].reshape(w_ada.shape[0], b, 1, 6 * d)

    w_in = attn_w_in[0]
    w_qkv = w_in[:, :3 * aw].astype(BF16)
    w_f = jnp.zeros((d, LANES), F32).at[:, :N_FOX_HEADS].set(w_in[:, 3 * aw:]).astype(BF16)
    b_f = jnp.zeros((1, LANES), F32).at[0, :N_FOX_HEADS].set(fox_b_f[0])
    rope_c, rope_sa, rope_sb = _rope_tables(s)
    q, k, v, logf = _l0_in(x, mods, norm_g[0, 0][None], w_qkv, w_f, b_f, rope_c, rope_sa, rope_sb, tm)

    n_blk = s // MOBA_BLOCK
    bias_t = _moba_gate(q, k, N_MOBA_HEADS)
    qaux_m = jnp.pad(jnp.swapaxes(bias_t, 2, 3).astype(BF16), ((0, 0), (0, 0), (0, 0), (0, LANES - n_blk)))
    blk_of_pos = jnp.arange(s, dtype=jnp.int32) // MOBA_BLOCK
    kaux_row = jnp.arange(LANES, dtype=jnp.int32)[None, :] == blk_of_pos[:, None]
    kaux_m = jnp.broadcast_to(kaux_row.astype(BF16)[None, None], (1, 2, s, LANES))
    o_m = _attention(q, k, v, qaux_m, kaux_m, N_MOBA_HEADS, 0)

    qaux_f, kaux_f = _fox_prep(logf, N_FOX_HEADS)
    o_f = _attention(q, k, v, qaux_f, kaux_f, N_FOX_HEADS, N_MOBA_HEADS)

    x1, h2 = _l0_out(o_m, o_f, attn_w_out[0].astype(BF16), x, mods, norm_g[0, 1][None], norm_g[0, 2][None], tm)

    x2, h3 = _ffn(h2, x1, ffn_w_gate[0].astype(BF16), ffn_w_up[0].astype(BF16), ffn_w_down[0].astype(BF16),
                  mods, norm_g[0, 3][None], norm_g[1, 0][None], min(256, s))

    wax = jnp.concatenate([lru_w_a[0], lru_w_x[0]], axis=-1).astype(BF16)
    w_r = jnp.zeros((d, LANES), F32).at[:, :n_experts].set(moe_w_router[0])
    w_r_hi = w_r.astype(BF16)
    w_r_lo = (w_r - w_r_hi.astype(F32)).astype(BF16)
    w_r = jnp.concatenate([w_r_hi, w_r_lo, w_r_hi], axis=0)
    b_r = jnp.zeros((1, LANES), F32).at[0, :n_experts].set(moe_b_router[0])
    x3, h4, logits = _lru(h3, x2, lru_w_in[0].astype(BF16), lru_conv_w[0], lru_conv_b[0][None], wax,
                          lru_b_a[0][None], lru_b_x[0][None], lru_lambda[0][None], lru_w_out[0].astype(BF16),
                          mods, norm_g[1, 1][None], norm_g[1, 2][None], w_r, b_r, min(256, s))

    n_tok = b * s
    gates, slot_tok, slot_dst, block_exp, nused, n_rows = _route(
        logits.reshape(n_tok, LANES)[:, :n_experts], n_experts, MOE_BLOCK_ROWS)
    y_rows = _moe(block_exp, nused, slot_tok, slot_dst, h4.reshape(n_tok, d),
                  moe_w_gate[0].astype(BF16), moe_w_up[0].astype(BF16), moe_w_down[0].astype(BF16), n_rows)
    out = _moe_out(y_rows, gates, x3.reshape(n_tok, d), mods, norm_g[1, 3][None], s, min(256, s))
    return out.reshape(b, s, d)
```

```python
import functools

import jax
import jax.numpy as jnp
from jax import lax
from jax.experimental import pallas as pl
from jax.experimental.pallas import tpu as pltpu

F32 = jnp.float32
BF16 = jnp.bfloat16
HIGHEST = lax.Precision.HIGHEST

NORM_EPS = 1e-6
HEAD_DIM = 64
N_MOBA_HEADS = 8
N_FOX_HEADS = 8
ROPE_DIMS = 16
ROPE_THETA = 500000.0
MOBA_BLOCK = 256
MOBA_TOPK = 3
CONV_WIDTH = 4
RG_C = 8.0
TOP_K = 2

LANES = 128
SUBLANES = 8
FOX_PREP_ROWS = 256
ATT_QUERIES = 1024
ATT_KEYS = 512
ATT_SUB = 256
ATT_LOOKAHEAD = 2
NEG = -1e30
LOG2E = 1.4426950408889634
MOE_BLOCK_ROWS = 512
MOE_FF_CHUNK = 512
VMEM_LIMIT = 56 * 1024 * 1024


def _cparams(sem, vmem=None):
    return pltpu.CompilerParams(dimension_semantics=sem, vmem_limit_bytes=vmem)


def _rms(x, g):
    return x * lax.rsqrt(jnp.mean(x * x, axis=-1, keepdims=True) + NORM_EPS) * g


def _const_spec(shape):
    n = len(shape)
    return pl.BlockSpec(shape, lambda *_: (0,) * n)


def _ada_kernel(c_ref, w_ref, b_ref, o_ref):
    c = c_ref[...]
    cond = c * jax.nn.sigmoid(c)
    o_ref[...] = jnp.dot(cond, w_ref[...], preferred_element_type=F32, precision=HIGHEST) + b_ref[...]


def _ada(c_pad, w_ada, b_ada):
    depth, d, d6 = w_ada.shape
    rows = c_pad.shape[0]
    nj = d6 // d
    return pl.pallas_call(
        _ada_kernel,
        grid=(depth, nj),
        in_specs=[
            pl.BlockSpec((rows, d), lambda l, j: (0, 0)),
            pl.BlockSpec((None, d, d), lambda l, j: (l, 0, j)),
            pl.BlockSpec((None, 1, d), lambda l, j: (l, 0, j)),
        ],
        out_specs=pl.BlockSpec((None, rows, d), lambda l, j: (l, 0, j)),
        out_shape=jax.ShapeDtypeStruct((depth, rows, d6), F32),
        compiler_params=_cparams(("parallel", "parallel")),
        name="ada",
    )(c_pad, w_ada, b_ada.reshape(depth, 1, d6))


def _l0_in_kernel(x_ref, mod_ref, g_ref, w_ref, wf_ref, bf_ref, rc_ref, rsa_ref, rsb_ref,
                  q_ref, k_ref, v_ref, lf_ref, *, d, aw, n_rope_chunks):
    m = mod_ref[...]
    h = _rms(x_ref[...], g_ref[...]) * (1.0 + m[:, d:2 * d]) + m[:, 0:d]
    hb = h.astype(BF16)
    proj = jnp.dot(hb, w_ref[...], preferred_element_type=F32)
    rc, rsa, rsb = rc_ref[...], rsa_ref[...], rsb_ref[...]

    def rope(t):
        return t * rc + pltpu.roll(t, ROPE_DIMS // 2, 1) * rsa + pltpu.roll(t, LANES - ROPE_DIMS // 2, 1) * rsb

    scale = HEAD_DIM ** -0.5 * LOG2E
    for c in range(aw // LANES):
        sl = slice(c * LANES, (c + 1) * LANES)
        qc = proj[:, sl] * scale
        kc = proj[:, aw + c * LANES:aw + (c + 1) * LANES]
        if c < n_rope_chunks:
            qc, kc = rope(qc), rope(kc)
        q_ref[:, sl] = qc.astype(BF16)
        k_ref[:, sl] = kc.astype(BF16)
    v_ref[...] = proj[:, 2 * aw:3 * aw].astype(BF16)
    fl = jnp.dot(hb, wf_ref[...], preferred_element_type=F32) + bf_ref[...]
    lf_ref[...] = jnp.minimum(fl, 0.0) - jnp.log(1.0 + jnp.exp(-jnp.abs(fl)))


def _l0_in(x, mods, g_pre, w_qkv, w_f, b_f, rope_c, rope_sa, rope_sb, tm):
    b, s, d = x.shape
    aw = w_qkv.shape[1] // 3
    d6 = mods.shape[-1]
    tok = lambda bi, i: (bi, i, 0)
    kern = functools.partial(_l0_in_kernel, d=d, aw=aw,
                             n_rope_chunks=N_MOBA_HEADS * HEAD_DIM // LANES)
    return pl.pallas_call(
        kern,
        grid=(b, s // tm),
        in_specs=[
            pl.BlockSpec((None, tm, d), tok),
            pl.BlockSpec((None, None, 1, d6), lambda bi, i: (0, bi, 0, 0)),
            _const_spec((1, d)),
            _const_spec(w_qkv.shape),
            _const_spec(w_f.shape),
            _const_spec((1, LANES)),
            pl.BlockSpec((tm, LANES), lambda bi, i: (i, 0)),
            pl.BlockSpec((tm, LANES), lambda bi, i: (i, 0)),
            pl.BlockSpec((tm, LANES), lambda bi, i: (i, 0)),
        ],
        out_specs=[
            pl.BlockSpec((None, tm, aw), tok),
            pl.BlockSpec((None, tm, aw), tok),
            pl.BlockSpec((None, tm, aw), tok),
            pl.BlockSpec((None, tm, LANES), tok),
        ],
        out_shape=[
            jax.ShapeDtypeStruct((b, s, aw), BF16),
            jax.ShapeDtypeStruct((b, s, aw), BF16),
            jax.ShapeDtypeStruct((b, s, aw), BF16),
            jax.ShapeDtypeStruct((b, s, LANES), F32),
        ],
        compiler_params=_cparams(("parallel", "parallel"), VMEM_LIMIT),
        name="l0_in",
    )(x, mods, g_pre, w_qkv, w_f, b_f, rope_c, rope_sa, rope_sb)


def _fox_prep_kernel(lf_ref, qa_ref, ka_ref, carry_ref, *, n_heads):
    t = lf_ref.shape[0]

    @pl.when(pl.program_id(1) == 0)
    def _():
        carry_ref[...] = jnp.zeros_like(carry_ref)

    row = lax.broadcasted_iota(jnp.int32, (t, t), 0)
    col = lax.broadcasted_iota(jnp.int32, (t, t), 1)
    tri = (col <= row).astype(F32)
    cum = jnp.dot(tri, lf_ref[...], preferred_element_type=F32, precision=HIGHEST) + carry_ref[...]
    carry_ref[...] = cum[t - 1:t, :]
    lane = lax.broadcasted_iota(jnp.int32, (t, LANES), 1)
    for h in range(n_heads):
        c = jnp.broadcast_to(cum[:, h:h + 1], (t, LANES)) * LOG2E
        hi = c.astype(BF16).astype(F32)
        r1 = c - hi
        mid = r1.astype(BF16).astype(F32)
        lo = r1 - mid
        qa = jnp.where(lane == 0, hi, jnp.where(lane == 1, mid, jnp.where(lane == 2, lo,
                       jnp.where(lane < 6, 1.0, 0.0))))
        ka = jnp.where(lane < 3, 1.0, jnp.where(lane == 3, -hi, jnp.where(lane == 4, -mid,
                       jnp.where(lane == 5, -lo, 0.0))))
        qa_ref[h] = qa.astype(BF16)
        ka_ref[h] = ka.astype(BF16)


def _fox_prep(logf, n_heads):
    b, s, _ = logf.shape
    t = min(FOX_PREP_ROWS, s)
    aux = jax.ShapeDtypeStruct((b, n_heads, s, LANES), BF16)
    aux_spec = pl.BlockSpec((None, n_heads, t, LANES), lambda bi, i: (bi, 0, i, 0))
    return pl.pallas_call(
        functools.partial(_fox_prep_kernel, n_heads=n_heads),
        grid=(b, s // t),
        in_specs=[pl.BlockSpec((None, t, LANES), lambda bi, i: (bi, i, 0))],
        out_specs=[aux_spec, aux_spec],
        out_shape=[aux, aux],
        scratch_shapes=[pltpu.VMEM((1, LANES), F32)],
        compiler_params=_cparams(("parallel", "arbitrary")),
        name="fox_prep",
    )(logf)


def _moba_gate_kernel(q_ref, k_ref, bt_ref, *, n_blk):
    s = k_ref.shape[0]
    t = MOBA_BLOCK
    rowi = lax.broadcasted_iota(jnp.int32, (n_blk, s), 0)
    cs = lax.broadcasted_iota(jnp.int32, (n_blk, s), 1)
    own = (cs >= rowi * t) & (cs < (rowi + 1) * t)
    km = jnp.dot(jnp.where(own, 1.0 / t, 0.0).astype(BF16), k_ref[...], preferred_element_type=F32)
    lane_k = lax.broadcasted_iota(jnp.int32, (n_blk, LANES), 1)
    km2 = jnp.concatenate([jnp.where(lane_k < HEAD_DIM, km, 0.0), jnp.where(lane_k >= HEAD_DIM, km, 0.0)], axis=0)
    hi = km2.astype(BF16)
    r1 = km2 - hi.astype(F32)
    mid = r1.astype(BF16)
    lo = (r1 - mid.astype(F32)).astype(BF16)
    g3 = lax.dot_general(jnp.concatenate([hi, mid, lo], axis=0), q_ref[...], (((1,), (1,)), ((), ())),
                         preferred_element_type=F32)
    g = g3[0:2 * n_blk] + g3[2 * n_blk:4 * n_blk] + g3[4 * n_blk:6 * n_blk]
    fully_past = (rowi + 1) * t <= cs
    for hh in range(2):
        gv = jnp.where(fully_past, g[hh * n_blk:(hh + 1) * n_blk], -jnp.inf)
        keep = own
        for _ in range(MOBA_TOPK):
            mx = jnp.max(gv, axis=0, keepdims=True)
            cand = jnp.where((gv == mx) & (mx > -jnp.inf), rowi, n_blk)
            pick = rowi == jnp.min(cand, axis=0, keepdims=True)
            keep = keep | pick
            gv = jnp.where(pick, -jnp.inf, gv)
        bt_ref[hh] = jnp.where(keep, 0.0, NEG)


def _moba_gate(q, k, n_heads):
    b, s, _ = q.shape
    n_blk = s // MOBA_BLOCK
    assert n_blk <= LANES
    return pl.pallas_call(
        functools.partial(_moba_gate_kernel, n_blk=n_blk),
        grid=(b, n_heads // 2),
        in_specs=[
            pl.BlockSpec((None, s, LANES), lambda bi, p: (bi, 0, p)),
            pl.BlockSpec((None, s, LANES), lambda bi, p: (bi, 0, p)),
        ],
        out_specs=pl.BlockSpec((None, 2, n_blk, s), lambda bi, p: (bi, p, 0, 0)),
        out_shape=jax.ShapeDtypeStruct((b, n_heads, n_blk, s), F32),
        compiler_params=_cparams(("parallel", "parallel")),
        name="moba_gate",
    )(q, k)


def _attn_kernel(q_ref, k_ref, v_ref, qa_ref, ka_ref, o_ref):
    tq = q_ref.shape[0]
    t = ATT_KEYS
    n_sub = tq // ATT_SUB
    n_diag = tq // t
    g = pl.program_id(2)
    lane = lax.broadcasted_iota(jnp.int32, (ATT_SUB, LANES), 1)
    lane_k = lax.broadcasted_iota(jnp.int32, (t, LANES), 1)
    row = lax.broadcasted_iota(jnp.int32, (ATT_SUB, t), 0)
    col = lax.broadcasted_iota(jnp.int32, (ATT_SUB, t), 1)
    nt = (((1,), (1,)), ((), ()))
    qas = []
    for hh in range(2):
        in_head = (lane < HEAD_DIM) if hh == 0 else (lane >= HEAD_DIM)
        for r in range(n_sub):
            rs = slice(r * ATT_SUB, (r + 1) * ATT_SUB)
            q = q_ref[rs, :]
            qas.append(jnp.concatenate([jnp.where(in_head, q, jnp.zeros_like(q)), qa_ref[hh, rs, :]], axis=1))

    def update(state, j, d):
        rows = pl.ds(pl.multiple_of(j * t, t), t)
        k = k_ref[rows, :]
        v = v_ref[rows, :]
        kks = [jnp.concatenate([k, ka_ref[hh, rows, :]], axis=1) for hh in range(2)]
        vhs = [jnp.where((lane_k < HEAD_DIM) if hh == 0 else (lane_k >= HEAD_DIM), v, jnp.ones_like(v))
               for hh in range(2)]

        def q_off(c):
            return (c % n_sub) * ATT_SUB

        active = [c for c in range(2 * n_sub) if d is None or q_off(c) + ATT_SUB > d * t]

        def qk(c):
            return lax.dot_general(qas[c], kks[c // n_sub], nt, preferred_element_type=F32)

        scs = {c: qk(c) for c in active[:ATT_LOOKAHEAD]}
        new = list(state)
        for n, c in enumerate(active):
            if n + ATT_LOOKAHEAD < len(active):
                nxt = active[n + ATT_LOOKAHEAD]
                scs[nxt] = qk(nxt)
            m_old, acc = state[c]
            sc = scs.pop(c)
            if d is not None and q_off(c) < (d + 1) * t:
                sc = jnp.where(col + d * t <= row + q_off(c), sc, NEG)
            m_new = jnp.maximum(m_old, jnp.max(sc, axis=-1, keepdims=True))
            p = jnp.exp2(sc - m_new).astype(BF16)
            acc = jnp.exp2(m_old - m_new) * acc + jnp.dot(p, vhs[c // n_sub], preferred_element_type=F32)
            new[c] = (m_new, acc)
        return tuple(new)

    init = tuple((jnp.full((ATT_SUB, 1), NEG, F32), jnp.zeros((ATT_SUB, LANES), F32)) for _ in range(2 * n_sub))
    state = lax.fori_loop(0, g * n_diag, lambda j, st: update(st, j, None), init)
    for d in range(n_diag):
        state = update(state, g * n_diag + d, d)
    for r in range(n_sub):
        outs = [state[hh * n_sub + r][1] for hh in range(2)]
        outs = [acc / pltpu.roll(acc, HEAD_DIM, 1) for acc in outs]
        o_ref[r * ATT_SUB:(r + 1) * ATT_SUB, :] = jnp.where(lane < HEAD_DIM, outs[0], outs[1]).astype(o_ref.dtype)


def _attention(q, k, v, qaux, kaux, n_heads, head_off):
    b, s, _ = q.shape
    t = ATT_QUERIES
    po = head_off // 2
    kb, kh = kaux.shape[0], kaux.shape[1]
    ka_map = (lambda bi, p, i: (bi, p, 0, 0)) if kb == b and kh == n_heads else (lambda bi, p, i: (0, 0, 0, 0))
    return pl.pallas_call(
        _attn_kernel,
        grid=(b, n_heads // 2, s // t),
        in_specs=[
            pl.BlockSpec((None, t, LANES), lambda bi, p, i: (bi, i, p + po)),
            pl.BlockSpec((None, s, LANES), lambda bi, p, i: (bi, 0, p + po)),
            pl.BlockSpec((None, s, LANES), lambda bi, p, i: (bi, 0, p + po)),
            pl.BlockSpec((None, 2, t, LANES), lambda bi, p, i: (bi, p, i, 0)),
            pl.BlockSpec((None, 2, s, LANES), ka_map),
        ],
        out_specs=pl.BlockSpec((None, t, LANES), lambda bi, p, i: (bi, i, p)),
        out_shape=jax.ShapeDtypeStruct((b, s, n_heads * HEAD_DIM), BF16),
        compiler_params=_cparams(("parallel", "parallel", "arbitrary")),
        name="attn",
    )(q, k, v, qaux, kaux)


def _l0_out_kernel(om_ref, of_ref, w_ref, x_ref, mod_ref, g1_ref, g2_ref, x1_ref, h2_ref, *, d):
    m = mod_ref[...]
    hw = om_ref.shape[1]
    y = (jnp.dot(om_ref[...], w_ref[0:hw, :], preferred_element_type=F32)
         + jnp.dot(of_ref[...], w_ref[hw:, :], preferred_element_type=F32))
    x1 = x_ref[...] + m[:, 2 * d:3 * d] * _rms(y, g1_ref[...])
    x1_ref[...] = x1
    h2_ref[...] = (_rms(x1, g2_ref[...]) * (1.0 + m[:, 4 * d:5 * d]) + m[:, 3 * d:4 * d]).astype(BF16)


def _l0_out(o_m, o_f, w_out, x, mods, g_post, g_pre2, tm):
    b, s, d = x.shape
    d6 = mods.shape[-1]
    tok = lambda bi, i: (bi, i, 0)
    return pl.pallas_call(
        functools.partial(_l0_out_kernel, d=d),
        grid=(b, s // tm),
        in_specs=[
            pl.BlockSpec((None, tm, o_m.shape[2]), tok),
            pl.BlockSpec((None, tm, o_f.shape[2]), tok),
            _const_spec(w_out.shape),
            pl.BlockSpec((None, tm, d), tok),
            pl.BlockSpec((None, None, 1, d6), lambda bi, i: (0, bi, 0, 0)),
            _const_spec((1, d)),
            _const_spec((1, d)),
        ],
        out_specs=[pl.BlockSpec((None, tm, d), tok), pl.BlockSpec((None, tm, d), tok)],
        out_shape=[jax.ShapeDtypeStruct((b, s, d), F32), jax.ShapeDtypeStruct((b, s, d), BF16)],
        compiler_params=_cparams(("parallel", "parallel"), VMEM_LIMIT),
        name="l0_out",
    )(o_m, o_f, w_out, x, mods, g_post, g_pre2)


def _ffn_kernel(h_ref, x_ref, wg_ref, wu_ref, wd_ref, mod0_ref, mod1_ref, g1_ref, g2_ref,
                x2_ref, h3_ref, *, d):
    h = h_ref[...]
    a = jnp.dot(h, wg_ref[...], preferred_element_type=F32)
    u = jnp.dot(h, wu_ref[...], preferred_element_type=F32)
    act = (a * jax.nn.sigmoid(a) * u).astype(BF16)
    y = jnp.dot(act, wd_ref[...], preferred_element_type=F32)
    m0 = mod0_ref[...]
    m1 = mod1_ref[...]
    x2 = x_ref[...] + m0[:, 5 * d:6 * d] * _rms(y, g1_ref[...])
    x2_ref[...] = x2
    h3_ref[...] = (_rms(x2, g2_ref[...]) * (1.0 + m1[:, d:2 * d]) + m1[:, 0:d]).astype(BF16)


def _ffn(h2, x1, w_gate, w_up, w_down, mods, g_post, g_pre_next, tm):
    b, s, d = x1.shape
    d6 = mods.shape[-1]
    tok = lambda bi, i: (bi, i, 0)
    single = dict(pipeline_mode=pl.Buffered(1))
    return pl.pallas_call(
        functools.partial(_ffn_kernel, d=d),
        grid=(b, s // tm),
        in_specs=[
            pl.BlockSpec((None, tm, d), tok),
            pl.BlockSpec((None, tm, d), tok),
            pl.BlockSpec(w_gate.shape, lambda bi, i: (0, 0), **single),
            pl.BlockSpec(w_up.shape, lambda bi, i: (0, 0), **single),
            pl.BlockSpec(w_down.shape, lambda bi, i: (0, 0), **single),
            pl.BlockSpec((None, None, 1, d6), lambda bi, i: (0, bi, 0, 0)),
            pl.BlockSpec((None, None, 1, d6), lambda bi, i: (1, bi, 0, 0)),
            _const_spec((1, d)),
            _const_spec((1, d)),
        ],
        out_specs=[pl.BlockSpec((None, tm, d), tok), pl.BlockSpec((None, tm, d), tok)],
        out_shape=[jax.ShapeDtypeStruct((b, s, d), F32), jax.ShapeDtypeStruct((b, s, d), BF16)],
        compiler_params=_cparams(("parallel", "parallel"), VMEM_LIMIT),
        name="ffn",
    )(h2, x1, w_gate, w_up, w_down, mods, mods, g_post, g_pre_next)


def _lru_kernel(h_ref, x_ref, win_ref, cw_ref, cb_ref, wax_ref, ba_ref, bx_ref, lam_ref, wout_ref,
                mod_ref, g1_ref, g2_ref, wr_ref, br_ref,
                x3_ref, h4_ref, lg_ref, utail_ref, hc_ref, *, d, dr, n_rnn_blocks):
    tm = h_ref.shape[0]
    sub = utail_ref.shape[0]

    @pl.when(pl.program_id(1) == 0)
    def _():
        utail_ref[...] = jnp.zeros_like(utail_ref)
        hc_ref[...] = jnp.zeros_like(hc_ref)

    proj = jnp.dot(h_ref[...], win_ref[...], preferred_element_type=F32)
    gate_branch = proj[:, :dr]
    u = proj[:, dr:]
    row_sub = lax.broadcasted_iota(jnp.int32, (sub, dr), 0)
    n_groups = tm // sub
    taps = [cw_ref[j:j + 1, :] for j in range(CONV_WIDTH)]
    bias = cb_ref[...]
    prev = utail_ref[...]
    rolled_prev = [None] + [pltpu.roll(prev, back, 0) for back in range(1, CONV_WIDTH)]
    conv_groups = []
    for g in range(n_groups):
        ug = u[g * sub:(g + 1) * sub]
        cg = bias + ug * taps[CONV_WIDTH - 1]
        for back in range(1, CONV_WIDTH):
            rolled = pltpu.roll(ug, back, 0)
            cg = cg + jnp.where(row_sub >= back, rolled, rolled_prev[back]) * taps[CONV_WIDTH - 1 - back]
            rolled_prev[back] = rolled
        conv_groups.append(cg)
    conv = jnp.concatenate(conv_groups, axis=0)
    utail_ref[...] = u[tm - sub:tm]

    w = dr // n_rnn_blocks
    rs, is_ = [], []
    for n in range(n_rnn_blocks):
        cbk = conv[:, n * w:(n + 1) * w].astype(BF16)
        ra = jnp.dot(cbk, wax_ref[n], preferred_element_type=F32)
        rs.append(ra[:, :w])
        is_.append(ra[:, w:])
    r = jax.nn.sigmoid(jnp.concatenate(rs, axis=1) + ba_ref[...])
    ig = jax.nn.sigmoid(jnp.concatenate(is_, axis=1) + bx_ref[...])
    nl = -lam_ref[...]
    softplus = jnp.maximum(nl, 0.0) + jnp.log(1.0 + jnp.exp(-jnp.abs(nl)))
    log_a = (-RG_C * r) * softplus
    a = jnp.exp(log_a)
    xin = jnp.exp(0.5 * jnp.log(1.0 - a * a)) * (ig * conv)

    carry = hc_ref[...]
    groups = []
    for g in range(n_groups):
        sa, sx = a[g * sub:(g + 1) * sub], xin[g * sub:(g + 1) * sub]
        dist = 1
        while dist < sub:
            keep = row_sub >= dist
            xs = jnp.where(keep, pltpu.roll(sx, dist, 0), 0.0)
            as_ = jnp.where(keep, pltpu.roll(sa, dist, 0), 1.0)
            sx = sx + sa * xs
            sa = sa * as_
            dist *= 2
        hg = sx + sa * carry
        carry = hg[sub - 1:sub]
        groups.append(hg)
    hs = jnp.concatenate(groups, axis=0)
    hc_ref[...] = carry

    y = (jax.nn.gelu(gate_branch, approximate=True) * hs).astype(BF16)
    out = jnp.dot(y, wout_ref[...], preferred_element_type=F32)
    m = mod_ref[...]
    x3 = x_ref[...] + m[:, 2 * d:3 * d] * _rms(out, g1_ref[...])
    x3_ref[...] = x3
    h4 = _rms(x3, g2_ref[...]) * (1.0 + m[:, 4 * d:5 * d]) + m[:, 3 * d:4 * d]
    h4_ref[...] = h4
    h_hi = h4.astype(BF16)
    h_lo = (h4 - h_hi.astype(F32)).astype(BF16)
    lg_ref[...] = jnp.dot(jnp.concatenate([h_hi, h_hi, h_lo], axis=1), wr_ref[...],
                          preferred_element_type=F32) + br_ref[...]


def _lru(h3, x2, w_in, conv_w, conv_b, wax, b_a, b_x, lam, w_out, mods, g_post, g_pre2, w_r, b_r, tm):
    b, s, d = x2.shape
    dr = w_out.shape[0]
    d6 = mods.shape[-1]
    nb = wax.shape[0]
    tok = lambda bi, i: (bi, i, 0)
    return pl.pallas_call(
        functools.partial(_lru_kernel, d=d, dr=dr, n_rnn_blocks=nb),
        grid=(b, s // tm),
        in_specs=[
            pl.BlockSpec((None, tm, d), tok),
            pl.BlockSpec((None, tm, d), tok),
            _const_spec(w_in.shape),
            _const_spec(conv_w.shape),
            _const_spec((1, dr)),
            _const_spec(wax.shape),
            _const_spec((1, dr)),
            _const_spec((1, dr)),
            _const_spec((1, dr)),
            _const_spec(w_out.shape),
            pl.BlockSpec((None, None, 1, d6), lambda bi, i: (1, bi, 0, 0)),
            _const_spec((1, d)),
            _const_spec((1, d)),
            _const_spec(w_r.shape),
            _const_spec((1, LANES)),
        ],
        out_specs=[pl.BlockSpec((None, tm, d), tok), pl.BlockSpec((None, tm, d), tok),
                   pl.BlockSpec((None, tm, LANES), tok)],
        out_shape=[jax.ShapeDtypeStruct((b, s, d), F32), jax.ShapeDtypeStruct((b, s, d), F32),
                   jax.ShapeDtypeStruct((b, s, LANES), F32)],
        scratch_shapes=[pltpu.VMEM((SUBLANES, dr), F32), pltpu.VMEM((1, dr), F32)],
        compiler_params=_cparams(("parallel", "arbitrary"), VMEM_LIMIT),
        name="lru",
    )(h3, x2, w_in, conv_w, conv_b, wax, b_a, b_x, lam, w_out, mods, g_post, g_pre2, w_r, b_r)


def _moe_kernel(bexp_ref, nused_ref, tok_ref, dst_ref, h_hbm, wg_hbm, wu_hbm, wd_hbm, o_hbm,
                xbuf, ybuf, wg_bf, wu_bf, wd_bf, stage_g, stage_u, stage_d, gsem, ssem, wsem,
                *, n_blocks, dummy_base):
    mb = xbuf.shape[1]
    i = pl.program_id(0)
    nused = nused_ref[0]
    slot = lax.rem(i, 2)
    other = 1 - slot

    def gather_copy(blk, buf, r):
        tok = tok_ref[blk * mb + r]
        return pltpu.make_async_copy(h_hbm.at[pl.ds(tok, 1), :], xbuf.at[buf, pl.ds(r, 1), :], gsem.at[buf])

    def scatter_copy(dst, buf, r):
        return pltpu.make_async_copy(ybuf.at[buf, pl.ds(r, 1), :], o_hbm.at[pl.ds(dst, 1), :], ssem.at[buf])

    def wait_gather(buf):
        pltpu.make_async_copy(h_hbm.at[pl.ds(0, mb), :], xbuf.at[buf], gsem.at[buf]).wait()

    def wait_scatter(buf):
        pltpu.make_async_copy(ybuf.at[buf], o_hbm.at[pl.ds(0, mb), :], ssem.at[buf]).wait()

    @pl.when(i == 0)
    def _():
        def body(r, c):
            gather_copy(0, 0, r).start()
            return c
        lax.fori_loop(0, mb, body, 0, unroll=8)
        ybuf[1] = jnp.zeros(ybuf.shape[1:], F32)

    @pl.when((i >= 1) & (i <= nused))
    def _():
        wait_scatter(slot)

    ck = MOE_FF_CHUNK
    n_chunks = wg_bf.shape[1] // ck
    e_cur = bexp_ref[jnp.minimum(i, n_blocks - 1)]
    e_prev = bexp_ref[jnp.maximum(i - 1, 0)]
    new_expert = (i == 0) | (e_cur != e_prev)

    def weight_copies(c, buf):
        cs = pl.ds(c * ck, ck)
        return (pltpu.make_async_copy(wg_hbm.at[e_cur, :, cs], stage_g.at[buf], wsem.at[buf]),
                pltpu.make_async_copy(wu_hbm.at[e_cur, :, cs], stage_u.at[buf], wsem.at[buf]),
                pltpu.make_async_copy(wd_hbm.at[e_cur, cs, :], stage_d.at[buf], wsem.at[buf]))

    def compute(load_weights):
        wait_gather(slot)
        x = xbuf[slot].astype(BF16)
        acc = jnp.zeros((mb, wd_bf.shape[1]), F32)
        per = -(-mb // n_chunks)
        nxt = jnp.minimum(i + 1, n_blocks - 1)
        prv = jnp.maximum(i - 1, 0)
        if load_weights:
            for c in range(min(2, n_chunks)):
                for cp in weight_copies(c, c % 2):
                    cp.start()
        for c in range(n_chunks):
            for r in range(c * per, min((c + 1) * per, mb)):
                gather_copy(nxt, other, r).start()
                dst = jnp.where(i == 0, dummy_base + r, dst_ref[prv * mb + r])
                scatter_copy(dst, other, r).start(priority=1)
            cs = slice(c * ck, (c + 1) * ck)
            if load_weights:
                for cp in weight_copies(c, c % 2):
                    cp.wait()
                wg_bf[:, cs] = stage_g[c % 2].astype(BF16)
                wu_bf[:, cs] = stage_u[c % 2].astype(BF16)
                wd_bf[cs, :] = stage_d[c % 2].astype(BF16)
                if c + 2 < n_chunks:
                    for cp in weight_copies(c + 2, c % 2):
                        cp.start()
            a = jnp.dot(x, wg_bf[:, cs], preferred_element_type=F32)
            u = jnp.dot(x, wu_bf[:, cs], preferred_element_type=F32)
            act = (a * jax.nn.sigmoid(a) * u).astype(BF16)
            acc = acc + jnp.dot(act, wd_bf[cs, :], preferred_element_type=F32)
        ybuf[slot] = acc

    @pl.when((i < nused) & new_expert)
    def _():
        compute(True)

    @pl.when((i < nused) & jnp.logical_not(new_expert))
    def _():
        compute(False)

    @pl.when(i == nused)
    def _():
        wait_gather(slot)

        def body(r, c):
            scatter_copy(dst_ref[(i - 1) * mb + r], other, r).start()
            return c
        lax.fori_loop(0, mb, body, 0, unroll=8)
        wait_scatter(other)


def _moe(block_exp, nused, slot_tok, slot_dst, h4, w_gate, w_up, w_down, n_out_rows):
    n_tok, d = h4.shape
    mb = MOE_BLOCK_ROWS
    cap = slot_tok.shape[0]
    n_blocks = cap // mb
    e, _, dff = w_gate.shape
    ck = MOE_FF_CHUNK
    assert dff % ck == 0
    grid_spec = pltpu.PrefetchScalarGridSpec(
        num_scalar_prefetch=4,
        grid=(n_blocks + 1,),
        in_specs=[pl.BlockSpec(memory_space=pl.ANY)] * 4,
        out_specs=pl.BlockSpec(memory_space=pl.ANY),
        scratch_shapes=[pltpu.VMEM((2, mb, d), F32), pltpu.VMEM((2, mb, d), F32),
                        pltpu.VMEM((d, dff), BF16), pltpu.VMEM((d, dff), BF16), pltpu.VMEM((dff, d), BF16),
                        pltpu.VMEM((2, d, ck), F32), pltpu.VMEM((2, d, ck), F32), pltpu.VMEM((2, ck, d), F32),
                        pltpu.SemaphoreType.DMA((2,)), pltpu.SemaphoreType.DMA((2,)),
                        pltpu.SemaphoreType.DMA((2,))],
    )
    return pl.pallas_call(
        functools.partial(_moe_kernel, n_blocks=n_blocks, dummy_base=n_out_rows),
        grid_spec=grid_spec,
        out_shape=jax.ShapeDtypeStruct((n_out_rows + mb, d), F32),
        compiler_params=_cparams(("arbitrary",), VMEM_LIMIT),
        name="moe",
    )(block_exp, nused, slot_tok, slot_dst, h4, w_gate, w_up, w_down)


def _moe_out_kernel(*refs, d):
    y_refs, (gate_ref, x_ref, mod_ref, g_ref, o_ref) = refs[:TOP_K], refs[TOP_K:]
    g = gate_ref[...]
    y = y_refs[0][...] * g[:, 0:1]
    for kk in range(1, TOP_K):
        y = y + y_refs[kk][...] * g[:, kk:kk + 1]
    m = mod_ref[...]
    o_ref[...] = x_ref[...] + m[:, 5 * d:6 * d] * _rms(y, g_ref[...])


def _moe_out(y_rows, gates, x3, mods, g_post, s, tm):
    n_tok, d = x3.shape
    d6 = mods.shape[-1]
    per_b = s // tm
    nt = n_tok // tm
    y_specs = [pl.BlockSpec((tm, d), functools.partial(lambda i, kk: (kk * nt + i, 0), kk=kk))
               for kk in range(TOP_K)]
    return pl.pallas_call(
        functools.partial(_moe_out_kernel, d=d),
        grid=(nt,),
        in_specs=y_specs + [
            pl.BlockSpec((tm, TOP_K), lambda i: (i, 0)),
            pl.BlockSpec((tm, d), lambda i: (i, 0)),
            pl.BlockSpec((None, None, 1, d6), lambda i: (1, i // per_b, 0, 0)),
            pl.BlockSpec((1, d), lambda i: (0, 0)),
        ],
        out_specs=pl.BlockSpec((tm, d), lambda i: (i, 0)),
        out_shape=jax.ShapeDtypeStruct((n_tok, d), F32),
        compiler_params=_cparams(("parallel",), VMEM_LIMIT),
        name="moe_out",
    )(*([y_rows] * TOP_K), gates, x3, mods, g_post)


def _route(logits, n_experts, mb):
    n_tok = logits.shape[0]
    top_logit, top_idx = lax.top_k(logits, TOP_K)
    gates = jax.nn.softmax(top_logit, axis=-1)
    exp_flat = top_idx.reshape(-1).astype(jnp.int32)
    n_asg = n_tok * TOP_K
    onehot = (exp_flat[:, None] == jnp.arange(n_experts, dtype=jnp.int32)[None, :]).astype(jnp.int32)
    counts = jnp.sum(onehot, axis=0)
    padded = ((counts + mb - 1) // mb) * mb
    pends = jnp.cumsum(padded)
    cap = (-(-n_asg // mb) + n_experts) * mb
    pad_ends = jnp.cumsum(padded - counts)
    pad_exp = jnp.sum(jnp.arange(cap - n_asg, dtype=jnp.int32)[:, None] >= pad_ends[None, :], axis=1)
    keys = jnp.concatenate([exp_flat * 2, pad_exp.astype(jnp.int32) * 2 + 1])
    payload = jnp.concatenate([jnp.arange(n_asg, dtype=jnp.int32), jnp.full((cap - n_asg,), -1, jnp.int32)])
    _, slot_flat = lax.sort((keys, payload), num_keys=1, is_stable=True)
    valid = slot_flat >= 0
    slot_tok = jnp.where(valid, slot_flat // TOP_K, 0)
    slot_dst = jnp.where(valid, (slot_flat % TOP_K) * n_tok + slot_flat // TOP_K,
                         n_asg + jnp.arange(cap, dtype=jnp.int32) % mb)
    n_blocks = cap // mb
    block_starts = jnp.arange(n_blocks, dtype=jnp.int32) * mb
    block_exp = jnp.minimum(jnp.searchsorted(pends, block_starts, side='right'), n_experts - 1).astype(jnp.int32)
    nused = (pends[-1] // mb).astype(jnp.int32).reshape(1)
    return gates, slot_tok, slot_dst, block_exp, nused, n_asg


def _rope_tables(s):
    half = ROPE_DIMS // 2
    inv_freq = jnp.power(ROPE_THETA, -jnp.arange(half, dtype=F32) / half)
    ang = jnp.arange(s, dtype=F32)[:, None] * inv_freq[None, :]
    cos, sin = jnp.cos(ang), jnp.sin(ang)
    lane = jnp.arange(LANES) % HEAD_DIM
    idx = lane % half
    is_x1 = lane < half
    is_x2 = (lane >= half) & (lane < ROPE_DIMS)
    c = jnp.where((is_x1 | is_x2)[None, :], cos[:, idx], 1.0)
    sa = jnp.where(is_x2[None, :], sin[:, idx], 0.0)
    sb = jnp.where(is_x1[None, :], -sin[:, idx], 0.0)
    return c.astype(F32), sa.astype(F32), sb.astype(F32)


def kernel(x, c, w_ada, b_ada, norm_g, attn_w_in, fox_b_f, attn_w_out, ffn_w_gate, ffn_w_up, ffn_w_down,
           lru_w_in, lru_conv_w, lru_conv_b, lru_w_a, lru_b_a, lru_w_x, lru_b_x, lru_lambda, lru_w_out,
           moe_w_router, moe_b_router, moe_w_gate, moe_w_up, moe_w_down):
    b, s, d = x.shape
    aw = attn_w_out.shape[1]
    n_experts = moe_w_router.shape[2]
    assert s % ATT_QUERIES == 0 and ATT_QUERIES % ATT_KEYS == 0 and ATT_KEYS % MOBA_BLOCK == 0
    assert ATT_QUERIES % ATT_SUB == 0
    tm = min(512, s)

    c_pad = jnp.zeros((8, d), F32).at[:b].set(c)
    mods = _ada(c_pad, w_ada, b_ada)[:, :b].reshape(w_ada.shape[0], b, 1, 6 * d)

    w_in = attn_w_in[0]
    w_qkv = w_in[:, :3 * aw].astype(BF16)
    w_f = jnp.zeros((d, LANES), F32).at[:, :N_FOX_HEADS].set(w_in[:, 3 * aw:]).astype(BF16)
    b_f = jnp.zeros((1, LANES), F32).at[0, :N_FOX_HEADS].set(fox_b_f[0])
    rope_c, rope_sa, rope_sb = _rope_tables(s)
    q, k, v, logf = _l0_in(x, mods, norm_g[0, 0][None], w_qkv, w_f, b_f, rope_c, rope_sa, rope_sb, tm)

    n_blk = s // MOBA_BLOCK
    bias_t = _moba_gate(q, k, N_MOBA_HEADS)
    qaux_m = jnp.pad(jnp.swapaxes(bias_t, 2, 3).astype(BF16), ((0, 0), (0, 0), (0, 0), (0, LANES - n_blk)))
    blk_of_pos = jnp.arange(s, dtype=jnp.int32) // MOBA_BLOCK
    kaux_row = jnp.arange(LANES, dtype=jnp.int32)[None, :] == blk_of_pos[:, None]
    kaux_m = jnp.broadcast_to(kaux_row.astype(BF16)[None, None], (1, 2, s, LANES))
    o_m = _attention(q, k, v, qaux_m, kaux_m, N_MOBA_HEADS, 0)

    qaux_f, kaux_f = _fox_prep(logf, N_FOX_HEADS)
    o_f = _attention(q, k, v, qaux_f, kaux_f, N_FOX_HEADS, N_MOBA_HEADS)

    x1, h2 = _l0_out(o_m, o_f, attn_w_out[0].astype(BF16), x, mods, norm_g[0, 1][None], norm_g[0, 2][None], tm)

    x2, h3 = _ffn(h2, x1, ffn_w_gate[0].astype(BF16), ffn_w_up[0].astype(BF16), ffn_w_down[0].astype(BF16),
                  mods, norm_g[0, 3][None], norm_g[1, 0][None], min(256, s))

    wax = jnp.concatenate([lru_w_a[0], lru_w_x[0]], axis=-1).astype(BF16)
    w_r = jnp.zeros((d, LANES), F32).at[:, :n_experts].set(moe_w_router[0])
    w_r_hi = w_r.astype(BF16)
    w_r_lo = (w_r - w_r_hi.astype(F32)).astype(BF16)
    w_r = jnp.concatenate([w_r_hi, w_r_lo, w_r_hi], axis=0)
    b_r = jnp.zeros((1, LANES), F32).at[0, :n_experts].set(moe_b_router[0])
    x3, h4, logits = _lru(h3, x2, lru_w_in[0].astype(BF16), lru_conv_w[0], lru_conv_b[0][None], wax,
                          lru_b_a[0][None], lru_b_x[0][None], lru_lambda[0][None], lru_w_out[0].astype(BF16),
                          mods, norm_g[1, 1][None], norm_g[1, 2][None], w_r, b_r, min(256, s))

    n_tok = b * s
    gates, slot_tok, slot_dst, block_exp, nused, n_rows = _route(
        logits.reshape(n_tok, LANES)[:, :n_experts], n_experts, MOE_BLOCK_ROWS)
    y_rows = _moe(block_exp, nused, slot_tok, slot_dst, h4.reshape(n_tok, d),
                  moe_w_gate.reshape(moe_w_gate.shape[1:]), moe_w_up.reshape(moe_w_up.shape[1:]),
                  moe_w_down.reshape(moe_w_down.shape[1:]), n_rows)
    out = _moe_out(y_rows, gates, x3.reshape(n_tok, d), mods, norm_g[1, 3][None], s, min(256, s))
    return out.reshape(b, s, d)
```

```python
import functools

import jax
import jax.numpy as jnp
from jax import lax
from jax.experimental import pallas as pl
from jax.experimental.pallas import tpu as pltpu

F32 = jnp.float32
BF16 = jnp.bfloat16
HIGHEST = lax.Precision.HIGHEST

NORM_EPS = 1e-6
HEAD_DIM = 64
N_MOBA_HEADS = 8
N_FOX_HEADS = 8
ROPE_DIMS = 16
ROPE_THETA = 500000.0
MOBA_BLOCK = 256
MOBA_TOPK = 3
CONV_WIDTH = 4
RG_C = 8.0
TOP_K = 2

LANES = 128
SUBLANES = 8
FOX_PREP_ROWS = 256
ATT_QUERIES = 1024
ATT_KEYS = 512
ATT_SUB = 256
ATT_LOOKAHEAD = 2
NEG = -1e30
LOG2E = 1.4426950408889634
MOE_BLOCK_ROWS = 512
MOE_FF_CHUNK = 512
VMEM_LIMIT = 56 * 1024 * 1024


def _cparams(sem, vmem=None):
    return pltpu.CompilerParams(dimension_semantics=sem, vmem_limit_bytes=vmem)


def _rms(x, g):
    return x * lax.rsqrt(jnp.mean(x * x, axis=-1, keepdims=True) + NORM_EPS) * g


def _const_spec(shape):
    n = len(shape)
    return pl.BlockSpec(shape, lambda *_: (0,) * n)


def _ada_kernel(c_ref, w_ref, b_ref, o_ref):
    c = c_ref[...]
    cond = c * jax.nn.sigmoid(c)
    o_ref[...] = jnp.dot(cond, w_ref[...], preferred_element_type=F32, precision=HIGHEST) + b_ref[...]


def _ada(c_pad, w_ada, b_ada):
    depth, d, d6 = w_ada.shape
    rows = c_pad.shape[0]
    nj = d6 // d
    return pl.pallas_call(
        _ada_kernel,
        grid=(depth, nj),
        in_specs=[
            pl.BlockSpec((rows, d), lambda l, j: (0, 0)),
            pl.BlockSpec((None, d, d), lambda l, j: (l, 0, j)),
            pl.BlockSpec((None, 1, d), lambda l, j: (l, 0, j)),
        ],
        out_specs=pl.BlockSpec((None, rows, d), lambda l, j: (l, 0, j)),
        out_shape=jax.ShapeDtypeStruct((depth, rows, d6), F32),
        compiler_params=_cparams(("parallel", "parallel")),
        name="ada",
    )(c_pad, w_ada, b_ada.reshape(depth, 1, d6))


def _l0_in_kernel(x_ref, mod_ref, g_ref, w_ref, wf_ref, bf_ref, rc_ref, rsa_ref, rsb_ref,
                  q_ref, k_ref, v_ref, lf_ref, *, d, aw, n_rope_chunks):
    m = mod_ref[...]
    h = _rms(x_ref[...], g_ref[...]) * (1.0 + m[:, d:2 * d]) + m[:, 0:d]
    hb = h.astype(BF16)
    proj = jnp.dot(hb, w_ref[...], preferred_element_type=F32)
    rc, rsa, rsb = rc_ref[...], rsa_ref[...], rsb_ref[...]

    def rope(t):
        return t * rc + pltpu.roll(t, ROPE_DIMS // 2, 1) * rsa + pltpu.roll(t, LANES - ROPE_DIMS // 2, 1) * rsb

    scale = HEAD_DIM ** -0.5 * LOG2E
    for c in range(aw // LANES):
        sl = slice(c * LANES, (c + 1) * LANES)
        qc = proj[:, sl] * scale
        kc = proj[:, aw + c * LANES:aw + (c + 1) * LANES]
        if c < n_rope_chunks:
            qc, kc = rope(qc), rope(kc)
        q_ref[:, sl] = qc.astype(BF16)
        k_ref[:, sl] = kc.astype(BF16)
    v_ref[...] = proj[:, 2 * aw:3 * aw].astype(BF16)
    fl = jnp.dot(hb, wf_ref[...], preferred_element_type=F32) + bf_ref[...]
    lf_ref[...] = jnp.minimum(fl, 0.0) - jnp.log(1.0 + jnp.exp(-jnp.abs(fl)))


def _l0_in(x, mods, g_pre, w_qkv, w_f, b_f, rope_c, rope_sa, rope_sb, tm):
    b, s, d = x.shape
    aw = w_qkv.shape[1] // 3
    d6 = mods.shape[-1]
    tok = lambda bi, i: (bi, i, 0)
    kern = functools.partial(_l0_in_kernel, d=d, aw=aw,
                             n_rope_chunks=N_MOBA_HEADS * HEAD_DIM // LANES)
    return pl.pallas_call(
        kern,
        grid=(b, s // tm),
        in_specs=[
            pl.BlockSpec((None, tm, d), tok),
            pl.BlockSpec((None, None, 1, d6), lambda bi, i: (0, bi, 0, 0)),
            _const_spec((1, d)),
            _const_spec(w_qkv.shape),
            _const_spec(w_f.shape),
            _const_spec((1, LANES)),
            pl.BlockSpec((tm, LANES), lambda bi, i: (i, 0)),
            pl.BlockSpec((tm, LANES), lambda bi, i: (i, 0)),
            pl.BlockSpec((tm, LANES), lambda bi, i: (i, 0)),
        ],
        out_specs=[
            pl.BlockSpec((None, tm, aw), tok),
            pl.BlockSpec((None, tm, aw), tok),
            pl.BlockSpec((None, tm, aw), tok),
            pl.BlockSpec((None, tm, LANES), tok),
        ],
        out_shape=[
            jax.ShapeDtypeStruct((b, s, aw), BF16),
            jax.ShapeDtypeStruct((b, s, aw), BF16),
            jax.ShapeDtypeStruct((b, s, aw), BF16),
            jax.ShapeDtypeStruct((b, s, LANES), F32),
        ],
        compiler_params=_cparams(("parallel", "parallel"), VMEM_LIMIT),
        name="l0_in",
    )(x, mods, g_pre, w_qkv, w_f, b_f, rope_c, rope_sa, rope_sb)


def _fox_prep_kernel(lf_ref, qa_ref, ka_ref, carry_ref, *, n_heads):
    t = lf_ref.shape[0]

    @pl.when(pl.program_id(1) == 0)
    def _():
        carry_ref[...] = jnp.zeros_like(carry_ref)

    row = lax.broadcasted_iota(jnp.int32, (t, t), 0)
    col = lax.broadcasted_iota(jnp.int32, (t, t), 1)
    tri = (col <= row).astype(F32)
    cum = jnp.dot(tri, lf_ref[...], preferred_element_type=F32, precision=HIGHEST) + carry_ref[...]
    carry_ref[...] = cum[t - 1:t, :]
    lane = lax.broadcasted_iota(jnp.int32, (t, LANES), 1)
    for h in range(n_heads):
        c = jnp.broadcast_to(cum[:, h:h + 1], (t, LANES)) * LOG2E
        hi = c.astype(BF16).astype(F32)
        r1 = c - hi
        mid = r1.astype(BF16).astype(F32)
        lo = r1 - mid
        qa = jnp.where(lane == 0, hi, jnp.where(lane == 1, mid, jnp.where(lane == 2, lo,
                       jnp.where(lane < 6, 1.0, 0.0))))
        ka = jnp.where(lane < 3, 1.0, jnp.where(lane == 3, -hi, jnp.where(lane == 4, -mid,
                       jnp.where(lane == 5, -lo, 0.0))))
        qa_ref[h] = qa.astype(BF16)
        ka_ref[h] = ka.astype(BF16)


def _fox_prep(logf, n_heads):
    b, s, _ = logf.shape
    t = min(FOX_PREP_ROWS, s)
    aux = jax.ShapeDtypeStruct((b, n_heads, s, LANES), BF16)
    aux_spec = pl.BlockSpec((None, n_heads, t, LANES), lambda bi, i: (bi, 0, i, 0))
    return pl.pallas_call(
        functools.partial(_fox_prep_kernel, n_heads=n_heads),
        grid=(b, s // t),
        in_specs=[pl.BlockSpec((None, t, LANES), lambda bi, i: (bi, i, 0))],
        out_specs=[aux_spec, aux_spec],
        out_shape=[aux, aux],
        scratch_shapes=[pltpu.VMEM((1, LANES), F32)],
        compiler_params=_cparams(("parallel", "arbitrary")),
        name="fox_prep",
    )(logf)


def _moba_gate_kernel(q_ref, k_ref, qa_ref, *, n_blk):
    s = k_ref.shape[0]
    t = MOBA_BLOCK
    rowi = lax.broadcasted_iota(jnp.int32, (n_blk, s), 0)
    cs = lax.broadcasted_iota(jnp.int32, (n_blk, s), 1)
    own = (cs >= rowi * t) & (cs < (rowi + 1) * t)
    km = jnp.dot(jnp.where(own, 1.0 / t, 0.0).astype(BF16), k_ref[...], preferred_element_type=F32)
    lane_k = lax.broadcasted_iota(jnp.int32, (n_blk, LANES), 1)
    km2 = jnp.concatenate([jnp.where(lane_k < HEAD_DIM, km, 0.0), jnp.where(lane_k >= HEAD_DIM, km, 0.0)], axis=0)
    hi = km2.astype(BF16)
    r1 = km2 - hi.astype(F32)
    mid = r1.astype(BF16)
    lo = (r1 - mid.astype(F32)).astype(BF16)
    g3 = lax.dot_general(jnp.concatenate([hi, mid, lo], axis=0), q_ref[...], (((1,), (1,)), ((), ())),
                         preferred_element_type=F32)
    g = g3[0:2 * n_blk] + g3[2 * n_blk:4 * n_blk] + g3[4 * n_blk:6 * n_blk]
    fully_past = (rowi + 1) * t <= cs
    for hh in range(2):
        gv = jnp.where(fully_past, g[hh * n_blk:(hh + 1) * n_blk], -jnp.inf)
        keep = own
        for _ in range(MOBA_TOPK):
            mx = jnp.max(gv, axis=0, keepdims=True)
            cand = jnp.where((gv == mx) & (mx > -jnp.inf), rowi, n_blk)
            pick = rowi == jnp.min(cand, axis=0, keepdims=True)
            keep = keep | pick
            gv = jnp.where(pick, -jnp.inf, gv)
        bias_t = jnp.concatenate([jnp.where(keep, 0.0, NEG), jnp.zeros((LANES - n_blk, s), F32)], axis=0)
        qa_ref[hh] = jnp.transpose(bias_t).astype(BF16)


def _moba_gate(q, k, n_heads):
    b, s, _ = q.shape
    n_blk = s // MOBA_BLOCK
    assert n_blk <= LANES
    return pl.pallas_call(
        functools.partial(_moba_gate_kernel, n_blk=n_blk),
        grid=(b, n_heads // 2),
        in_specs=[
            pl.BlockSpec((None, s, LANES), lambda bi, p: (bi, 0, p)),
            pl.BlockSpec((None, s, LANES), lambda bi, p: (bi, 0, p)),
        ],
        out_specs=pl.BlockSpec((None, 2, s, LANES), lambda bi, p: (bi, p, 0, 0)),
        out_shape=jax.ShapeDtypeStruct((b, n_heads, s, LANES), BF16),
        compiler_params=_cparams(("parallel", "parallel")),
        name="moba_gate",
    )(q, k)


def _attn_kernel(q_ref, k_ref, v_ref, qa_ref, ka_ref, o_ref):
    tq = q_ref.shape[0]
    t = ATT_KEYS
    n_sub = tq // ATT_SUB
    n_diag = tq // t
    g = pl.program_id(2)
    lane = lax.broadcasted_iota(jnp.int32, (ATT_SUB, LANES), 1)
    lane_k = lax.broadcasted_iota(jnp.int32, (t, LANES), 1)
    row = lax.broadcasted_iota(jnp.int32, (ATT_SUB, t), 0)
    col = lax.broadcasted_iota(jnp.int32, (ATT_SUB, t), 1)
    nt = (((1,), (1,)), ((), ()))
    qas = []
    for hh in range(2):
        in_head = (lane < HEAD_DIM) if hh == 0 else (lane >= HEAD_DIM)
        for r in range(n_sub):
            rs = slice(r * ATT_SUB, (r + 1) * ATT_SUB)
            q = q_ref[rs, :]
            qas.append(jnp.concatenate([jnp.where(in_head, q, jnp.zeros_like(q)), qa_ref[hh, rs, :]], axis=1))

    def update(state, j, d):
        rows = pl.ds(pl.multiple_of(j * t, t), t)
        k = k_ref[rows, :]
        v = v_ref[rows, :]
        kks = [jnp.concatenate([k, ka_ref[hh, rows, :]], axis=1) for hh in range(2)]
        vhs = [jnp.where((lane_k < HEAD_DIM) if hh == 0 else (lane_k >= HEAD_DIM), v, jnp.ones_like(v))
               for hh in range(2)]

        def q_off(c):
            return (c % n_sub) * ATT_SUB

        active = [c for c in range(2 * n_sub) if d is None or q_off(c) + ATT_SUB > d * t]

        def qk(c):
            return lax.dot_general(qas[c], kks[c // n_sub], nt, preferred_element_type=F32)

        scs = {c: qk(c) for c in active[:ATT_LOOKAHEAD]}
        new = list(state)
        for n, c in enumerate(active):
            if n + ATT_LOOKAHEAD < len(active):
                nxt = active[n + ATT_LOOKAHEAD]
                scs[nxt] = qk(nxt)
            m_old, acc = state[c]
            sc = scs.pop(c)
            if d is not None and q_off(c) < (d + 1) * t:
                sc = jnp.where(col + d * t <= row + q_off(c), sc, NEG)
            m_new = jnp.maximum(m_old, jnp.max(sc, axis=-1, keepdims=True))
            p = jnp.exp2(sc - m_new).astype(BF16)
            acc = jnp.exp2(m_old - m_new) * acc + jnp.dot(p, vhs[c // n_sub], preferred_element_type=F32)
            new[c] = (m_new, acc)
        return tuple(new)

    init = tuple((jnp.full((ATT_SUB, 1), NEG, F32), jnp.zeros((ATT_SUB, LANES), F32)) for _ in range(2 * n_sub))
    state = lax.fori_loop(0, g * n_diag, lambda j, st: update(st, j, None), init)
    for d in range(n_diag):
        state = update(state, g * n_diag + d, d)
    for r in range(n_sub):
        outs = [state[hh * n_sub + r][1] for hh in range(2)]
        outs = [acc / pltpu.roll(acc, HEAD_DIM, 1) for acc in outs]
        o_ref[r * ATT_SUB:(r + 1) * ATT_SUB, :] = jnp.where(lane < HEAD_DIM, outs[0], outs[1]).astype(o_ref.dtype)


def _attention(q, k, v, qaux, kaux, n_heads, head_off):
    b, s, _ = q.shape
    t = ATT_QUERIES
    po = head_off // 2
    kb, kh = kaux.shape[0], kaux.shape[1]
    ka_map = (lambda bi, p, i: (bi, p, 0, 0)) if kb == b and kh == n_heads else (lambda bi, p, i: (0, 0, 0, 0))
    return pl.pallas_call(
        _attn_kernel,
        grid=(b, n_heads // 2, s // t),
        in_specs=[
            pl.BlockSpec((None, t, LANES), lambda bi, p, i: (bi, i, p + po)),
            pl.BlockSpec((None, s, LANES), lambda bi, p, i: (bi, 0, p + po)),
            pl.BlockSpec((None, s, LANES), lambda bi, p, i: (bi, 0, p + po)),
            pl.BlockSpec((None, 2, t, LANES), lambda bi, p, i: (bi, p, i, 0)),
            pl.BlockSpec((None, 2, s, LANES), ka_map),
        ],
        out_specs=pl.BlockSpec((None, t, LANES), lambda bi, p, i: (bi, i, p)),
        out_shape=jax.ShapeDtypeStruct((b, s, n_heads * HEAD_DIM), BF16),
        compiler_params=_cparams(("parallel", "parallel", "arbitrary")),
        name="attn",
    )(q, k, v, qaux, kaux)


def _l0_tail_kernel(om_ref, of_ref, x_ref, wo_ref, wg_ref, wu_ref, wd_ref, mod0_ref, mod1_ref, g_ref,
                    x2_ref, h3_ref, *, d):
    m0 = mod0_ref[...]
    m1 = mod1_ref[...]
    hw = om_ref.shape[1]
    y = (jnp.dot(om_ref[...], wo_ref[0:hw, :], preferred_element_type=F32)
         + jnp.dot(of_ref[...], wo_ref[hw:, :], preferred_element_type=F32))
    x1 = x_ref[...] + m0[:, 2 * d:3 * d] * _rms(y, g_ref[0:1, :])
    h = (_rms(x1, g_ref[1:2, :]) * (1.0 + m0[:, 4 * d:5 * d]) + m0[:, 3 * d:4 * d]).astype(BF16)
    a = jnp.dot(h, wg_ref[...], preferred_element_type=F32)
    u = jnp.dot(h, wu_ref[...], preferred_element_type=F32)
    act = (a * jax.nn.sigmoid(a) * u).astype(BF16)
    y = jnp.dot(act, wd_ref[...], preferred_element_type=F32)
    x2 = x1 + m0[:, 5 * d:6 * d] * _rms(y, g_ref[2:3, :])
    x2_ref[...] = x2
    h3_ref[...] = (_rms(x2, g_ref[3:4, :]) * (1.0 + m1[:, d:2 * d]) + m1[:, 0:d]).astype(BF16)


def _l0_tail(o_m, o_f, x, w_out, w_gate, w_up, w_down, mods, gains, tm):
    b, s, d = x.shape
    d6 = mods.shape[-1]
    tok = lambda bi, i: (bi, i, 0)
    single = dict(pipeline_mode=pl.Buffered(1))
    return pl.pallas_call(
        functools.partial(_l0_tail_kernel, d=d),
        grid=(b, s // tm),
        in_specs=[
            pl.BlockSpec((None, tm, o_m.shape[2]), tok),
            pl.BlockSpec((None, tm, o_f.shape[2]), tok),
            pl.BlockSpec((None, tm, d), tok),
            pl.BlockSpec(w_out.shape, lambda bi, i: (0, 0), **single),
            pl.BlockSpec(w_gate.shape, lambda bi, i: (0, 0), **single),
            pl.BlockSpec(w_up.shape, lambda bi, i: (0, 0), **single),
            pl.BlockSpec(w_down.shape, lambda bi, i: (0, 0), **single),
            pl.BlockSpec((None, None, 1, d6), lambda bi, i: (0, bi, 0, 0)),
            pl.BlockSpec((None, None, 1, d6), lambda bi, i: (1, bi, 0, 0)),
            _const_spec(gains.shape),
        ],
        out_specs=[pl.BlockSpec((None, tm, d), tok), pl.BlockSpec((None, tm, d), tok)],
        out_shape=[jax.ShapeDtypeStruct((b, s, d), F32), jax.ShapeDtypeStruct((b, s, d), BF16)],
        compiler_params=_cparams(("parallel", "parallel"), VMEM_LIMIT),
        name="l0_tail",
    )(o_m, o_f, x, w_out, w_gate, w_up, w_down, mods, mods, gains)


def _lru_kernel(h_ref, x_ref, win_ref, cw_ref, cb_ref, wax_ref, ba_ref, bx_ref, lam_ref, wout_ref,
                mod_ref, g1_ref, g2_ref, wr_ref, br_ref,
                x3_ref, h4_ref, lg_ref, utail_ref, hc_ref, *, d, dr, n_rnn_blocks):
    tm = h_ref.shape[0]
    sub = utail_ref.shape[0]

    @pl.when(pl.program_id(1) == 0)
    def _():
        utail_ref[...] = jnp.zeros_like(utail_ref)
        hc_ref[...] = jnp.zeros_like(hc_ref)

    proj = jnp.dot(h_ref[...], win_ref[...], preferred_element_type=F32)
    gate_branch = proj[:, :dr]
    u = proj[:, dr:]
    row_sub = lax.broadcasted_iota(jnp.int32, (sub, dr), 0)
    n_groups = tm // sub
    taps = [cw_ref[j:j + 1, :] for j in range(CONV_WIDTH)]
    bias = cb_ref[...]
    prev = utail_ref[...]
    rolled_prev = [None] + [pltpu.roll(prev, back, 0) for back in range(1, CONV_WIDTH)]
    conv_groups = []
    for g in range(n_groups):
        ug = u[g * sub:(g + 1) * sub]
        cg = bias + ug * taps[CONV_WIDTH - 1]
        for back in range(1, CONV_WIDTH):
            rolled = pltpu.roll(ug, back, 0)
            cg = cg + jnp.where(row_sub >= back, rolled, rolled_prev[back]) * taps[CONV_WIDTH - 1 - back]
            rolled_prev[back] = rolled
        conv_groups.append(cg)
    conv = jnp.concatenate(conv_groups, axis=0)
    utail_ref[...] = u[tm - sub:tm]

    w = dr // n_rnn_blocks
    rs, is_ = [], []
    for n in range(n_rnn_blocks):
        cbk = conv[:, n * w:(n + 1) * w].astype(BF16)
        ra = jnp.dot(cbk, wax_ref[n], preferred_element_type=F32)
        rs.append(ra[:, :w])
        is_.append(ra[:, w:])
    r = jax.nn.sigmoid(jnp.concatenate(rs, axis=1) + ba_ref[...])
    ig = jax.nn.sigmoid(jnp.concatenate(is_, axis=1) + bx_ref[...])
    nl = -lam_ref[...]
    softplus = jnp.maximum(nl, 0.0) + jnp.log(1.0 + jnp.exp(-jnp.abs(nl)))
    log_a = (-RG_C * r) * softplus
    a = jnp.exp(log_a)
    xin = jnp.exp(0.5 * jnp.log(1.0 - a * a)) * (ig * conv)

    carry = hc_ref[...]
    groups = []
    for g in range(n_groups):
        sa, sx = a[g * sub:(g + 1) * sub], xin[g * sub:(g + 1) * sub]
        dist = 1
        while dist < sub:
            keep = row_sub >= dist
            xs = jnp.where(keep, pltpu.roll(sx, dist, 0), 0.0)
            as_ = jnp.where(keep, pltpu.roll(sa, dist, 0), 1.0)
            sx = sx + sa * xs
            sa = sa * as_
            dist *= 2
        hg = sx + sa * carry
        carry = hg[sub - 1:sub]
        groups.append(hg)
    hs = jnp.concatenate(groups, axis=0)
    hc_ref[...] = carry

    y = (jax.nn.gelu(gate_branch, approximate=True) * hs).astype(BF16)
    out = jnp.dot(y, wout_ref[...], preferred_element_type=F32)
    m = mod_ref[...]
    x3 = x_ref[...] + m[:, 2 * d:3 * d] * _rms(out, g1_ref[...])
    x3_ref[...] = x3
    h4 = _rms(x3, g2_ref[...]) * (1.0 + m[:, 4 * d:5 * d]) + m[:, 3 * d:4 * d]
    h4_ref[...] = h4
    h_hi = h4.astype(BF16)
    h_lo = (h4 - h_hi.astype(F32)).astype(BF16)
    lg_ref[...] = jnp.dot(jnp.concatenate([h_hi, h_hi, h_lo], axis=1), wr_ref[...],
                          preferred_element_type=F32) + br_ref[...]


def _lru(h3, x2, w_in, conv_w, conv_b, wax, b_a, b_x, lam, w_out, mods, g_post, g_pre2, w_r, b_r, tm):
    b, s, d = x2.shape
    dr = w_out.shape[0]
    d6 = mods.shape[-1]
    nb = wax.shape[0]
    tok = lambda bi, i: (bi, i, 0)
    return pl.pallas_call(
        functools.partial(_lru_kernel, d=d, dr=dr, n_rnn_blocks=nb),
        grid=(b, s // tm),
        in_specs=[
            pl.BlockSpec((None, tm, d), tok),
            pl.BlockSpec((None, tm, d), tok),
            _const_spec(w_in.shape),
            _const_spec(conv_w.shape),
            _const_spec((1, dr)),
            _const_spec(wax.shape),
            _const_spec((1, dr)),
            _const_spec((1, dr)),
            _const_spec((1, dr)),
            _const_spec(w_out.shape),
            pl.BlockSpec((None, None, 1, d6), lambda bi, i: (1, bi, 0, 0)),
            _const_spec((1, d)),
            _const_spec((1, d)),
            _const_spec(w_r.shape),
            _const_spec((1, LANES)),
        ],
        out_specs=[pl.BlockSpec((None, tm, d), tok), pl.BlockSpec((None, tm, d), tok),
                   pl.BlockSpec((None, tm, LANES), tok)],
        out_shape=[jax.ShapeDtypeStruct((b, s, d), F32), jax.ShapeDtypeStruct((b, s, d), F32),
                   jax.ShapeDtypeStruct((b, s, LANES), F32)],
        scratch_shapes=[pltpu.VMEM((SUBLANES, dr), F32), pltpu.VMEM((1, dr), F32)],
        compiler_params=_cparams(("parallel", "arbitrary"), VMEM_LIMIT),
        name="lru",
    )(h3, x2, w_in, conv_w, conv_b, wax, b_a, b_x, lam, w_out, mods, g_post, g_pre2, w_r, b_r)


def _moe_kernel(bexp_ref, nused_ref, tok_ref, dst_ref, h_hbm, wg_hbm, wu_hbm, wd_hbm, o_hbm,
                xbuf, ybuf, wg_bf, wu_bf, wd_bf, stage_g, stage_u, stage_d, gsem, ssem, wsem,
                *, n_blocks, dummy_base):
    mb = xbuf.shape[1]
    i = pl.program_id(0)
    nused = nused_ref[0]
    slot = lax.rem(i, 2)
    other = 1 - slot

    def gather_copy(blk, buf, r):
        tok = tok_ref[blk * mb + r]
        return pltpu.make_async_copy(h_hbm.at[pl.ds(tok, 1), :], xbuf.at[buf, pl.ds(r, 1), :], gsem.at[buf])

    def scatter_copy(dst, buf, r):
        return pltpu.make_async_copy(ybuf.at[buf, pl.ds(r, 1), :], o_hbm.at[pl.ds(dst, 1), :], ssem.at[buf])

    def wait_gather(buf):
        pltpu.make_async_copy(h_hbm.at[pl.ds(0, mb), :], xbuf.at[buf], gsem.at[buf]).wait()

    def wait_scatter(buf):
        pltpu.make_async_copy(ybuf.at[buf], o_hbm.at[pl.ds(0, mb), :], ssem.at[buf]).wait()

    @pl.when(i == 0)
    def _():
        def body(r, c):
            gather_copy(0, 0, r).start()
            return c
        lax.fori_loop(0, mb, body, 0, unroll=8)
        ybuf[1] = jnp.zeros(ybuf.shape[1:], F32)

    @pl.when((i >= 1) & (i <= nused))
    def _():
        wait_scatter(slot)

    ck = MOE_FF_CHUNK
    n_chunks = wg_bf.shape[1] // ck
    e_cur = bexp_ref[jnp.minimum(i, n_blocks - 1)]
    e_prev = bexp_ref[jnp.maximum(i - 1, 0)]
    new_expert = (i == 0) | (e_cur != e_prev)

    def weight_copies(c, buf):
        cs = pl.ds(c * ck, ck)
        return (pltpu.make_async_copy(wg_hbm.at[e_cur, :, cs], stage_g.at[buf], wsem.at[buf]),
                pltpu.make_async_copy(wu_hbm.at[e_cur, :, cs], stage_u.at[buf], wsem.at[buf]),
                pltpu.make_async_copy(wd_hbm.at[e_cur, cs, :], stage_d.at[buf], wsem.at[buf]))

    def compute(load_weights):
        wait_gather(slot)
        x = xbuf[slot].astype(BF16)
        acc = jnp.zeros((mb, wd_bf.shape[1]), F32)
        per = -(-mb // n_chunks)
        nxt = jnp.minimum(i + 1, n_blocks - 1)
        prv = jnp.maximum(i - 1, 0)
        if load_weights:
            for c in range(min(2, n_chunks)):
                for cp in weight_copies(c, c % 2):
                    cp.start()
        for c in range(n_chunks):
            for r in range(c * per, min((c + 1) * per, mb)):
                gather_copy(nxt, other, r).start()
                dst = jnp.where(i == 0, dummy_base + r, dst_ref[prv * mb + r])
                scatter_copy(dst, other, r).start(priority=1)
            cs = slice(c * ck, (c + 1) * ck)
            if load_weights:
                for cp in weight_copies(c, c % 2):
                    cp.wait()
                wg_bf[:, cs] = stage_g[c % 2].astype(BF16)
                wu_bf[:, cs] = stage_u[c % 2].astype(BF16)
                wd_bf[cs, :] = stage_d[c % 2].astype(BF16)
                if c + 2 < n_chunks:
                    for cp in weight_copies(c + 2, c % 2):
                        cp.start()
            a = jnp.dot(x, wg_bf[:, cs], preferred_element_type=F32)
            u = jnp.dot(x, wu_bf[:, cs], preferred_element_type=F32)
            act = (a * jax.nn.sigmoid(a) * u).astype(BF16)
            acc = acc + jnp.dot(act, wd_bf[cs, :], preferred_element_type=F32)
        ybuf[slot] = acc

    @pl.when((i < nused) & new_expert)
    def _():
        compute(True)

    @pl.when((i < nused) & jnp.logical_not(new_expert))
    def _():
        compute(False)

    @pl.when(i == nused)
    def _():
        wait_gather(slot)

        def body(r, c):
            scatter_copy(dst_ref[(i - 1) * mb + r], other, r).start()
            return c
        lax.fori_loop(0, mb, body, 0, unroll=8)
        wait_scatter(other)


def _moe(block_exp, nused, slot_tok, slot_dst, h4, w_gate, w_up, w_down, n_out_rows):
    n_tok, d = h4.shape
    mb = MOE_BLOCK_ROWS
    cap = slot_tok.shape[0]
    n_blocks = cap // mb
    e, _, dff = w_gate.shape
    ck = MOE_FF_CHUNK
    assert dff % ck == 0
    grid_spec = pltpu.PrefetchScalarGridSpec(
        num_scalar_prefetch=4,
        grid=(n_blocks + 1,),
        in_specs=[pl.BlockSpec(memory_space=pl.ANY)] * 4,
        out_specs=pl.BlockSpec(memory_space=pl.ANY),
        scratch_shapes=[pltpu.VMEM((2, mb, d), F32), pltpu.VMEM((2, mb, d), F32),
                        pltpu.VMEM((d, dff), BF16), pltpu.VMEM((d, dff), BF16), pltpu.VMEM((dff, d), BF16),
                        pltpu.VMEM((2, d, ck), F32), pltpu.VMEM((2, d, ck), F32), pltpu.VMEM((2, ck, d), F32),
                        pltpu.SemaphoreType.DMA((2,)), pltpu.SemaphoreType.DMA((2,)),
                        pltpu.SemaphoreType.DMA((2,))],
    )
    return pl.pallas_call(
        functools.partial(_moe_kernel, n_blocks=n_blocks, dummy_base=n_out_rows),
        grid_spec=grid_spec,
        out_shape=jax.ShapeDtypeStruct((n_out_rows + mb, d), F32),
        compiler_params=_cparams(("arbitrary",), VMEM_LIMIT),
        name="moe",
    )(block_exp, nused, slot_tok, slot_dst, h4, w_gate, w_up, w_down)


def _moe_out_kernel(*refs, d):
    y_refs, (gate_ref, x_ref, mod_ref, g_ref, o_ref) = refs[:TOP_K], refs[TOP_K:]
    g = gate_ref[...]
    y = y_refs[0][...] * g[:, 0:1]
    for kk in range(1, TOP_K):
        y = y + y_refs[kk][...] * g[:, kk:kk + 1]
    m = mod_ref[...]
    o_ref[...] = x_ref[...] + m[:, 5 * d:6 * d] * _rms(y, g_ref[...])


def _moe_out(y_rows, gates, x3, mods, g_post, s, tm):
    n_tok, d = x3.shape
    d6 = mods.shape[-1]
    per_b = s // tm
    nt = n_tok // tm
    y_specs = [pl.BlockSpec((tm, d), functools.partial(lambda i, kk: (kk * nt + i, 0), kk=kk))
               for kk in range(TOP_K)]
    return pl.pallas_call(
        functools.partial(_moe_out_kernel, d=d),
        grid=(nt,),
        in_specs=y_specs + [
            pl.BlockSpec((tm, TOP_K), lambda i: (i, 0)),
            pl.BlockSpec((tm, d), lambda i: (i, 0)),
            pl.BlockSpec((None, None, 1, d6), lambda i: (1, i // per_b, 0, 0)),
            pl.BlockSpec((1, d), lambda i: (0, 0)),
        ],
        out_specs=pl.BlockSpec((tm, d), lambda i: (i, 0)),
        out_shape=jax.ShapeDtypeStruct((n_tok, d), F32),
        compiler_params=_cparams(("parallel",), VMEM_LIMIT),
        name="moe_out",
    )(*([y_rows] * TOP_K), gates, x3, mods, g_post)


def _route(logits, n_experts, mb):
    n_tok = logits.shape[0]
    top_logit, top_idx = lax.top_k(logits, TOP_K)
    gates = jax.nn.softmax(top_logit, axis=-1)
    exp_flat = top_idx.reshape(-1).astype(jnp.int32)
    n_asg = n_tok * TOP_K
    onehot = (exp_flat[:, None] == jnp.arange(n_experts, dtype=jnp.int32)[None, :]).astype(jnp.int32)
    counts = jnp.sum(onehot, axis=0)
    padded = ((counts + mb - 1) // mb) * mb
    pends = jnp.cumsum(padded)
    cap = (-(-n_asg // mb) + n_experts) * mb
    pad_ends = jnp.cumsum(padded - counts)
    pad_exp = jnp.sum(jnp.arange(cap - n_asg, dtype=jnp.int32)[:, None] >= pad_ends[None, :], axis=1)
    keys = jnp.concatenate([exp_flat * 2, pad_exp.astype(jnp.int32) * 2 + 1])
    payload = jnp.concatenate([jnp.arange(n_asg, dtype=jnp.int32), jnp.full((cap - n_asg,), -1, jnp.int32)])
    _, slot_flat = lax.sort((keys, payload), num_keys=1, is_stable=True)
    valid = slot_flat >= 0
    slot_tok = jnp.where(valid, slot_flat // TOP_K, 0)
    slot_dst = jnp.where(valid, (slot_flat % TOP_K) * n_tok + slot_flat // TOP_K,
                         n_asg + jnp.arange(cap, dtype=jnp.int32) % mb)
    n_blocks = cap // mb
    block_starts = jnp.arange(n_blocks, dtype=jnp.int32) * mb
    block_exp = jnp.minimum(jnp.sum(block_starts[:, None] >= pends[None, :], axis=1), n_experts - 1).astype(jnp.int32)
    nused = (pends[-1] // mb).astype(jnp.int32).reshape(1)
    return gates, slot_tok, slot_dst, block_exp, nused, n_asg


def _rope_tables(s):
    half = ROPE_DIMS // 2
    inv_freq = jnp.power(ROPE_THETA, -jnp.arange(half, dtype=F32) / half)
    ang = jnp.arange(s, dtype=F32)[:, None] * inv_freq[None, :]
    cos, sin = jnp.cos(ang), jnp.sin(ang)
    lane = jnp.arange(LANES) % HEAD_DIM
    idx = lane % half
    is_x1 = lane < half
    is_x2 = (lane >= half) & (lane < ROPE_DIMS)
    c = jnp.where((is_x1 | is_x2)[None, :], cos[:, idx], 1.0)
    sa = jnp.where(is_x2[None, :], sin[:, idx], 0.0)
    sb = jnp.where(is_x1[None, :], -sin[:, idx], 0.0)
    return c.astype(F32), sa.astype(F32), sb.astype(F32)


def kernel(x, c, w_ada, b_ada, norm_g, attn_w_in, fox_b_f, attn_w_out, ffn_w_gate, ffn_w_up, ffn_w_down,
           lru_w_in, lru_conv_w, lru_conv_b, lru_w_a, lru_b_a, lru_w_x, lru_b_x, lru_lambda, lru_w_out,
           moe_w_router, moe_b_router, moe_w_gate, moe_w_up, moe_w_down):
    b, s, d = x.shape
    aw = attn_w_out.shape[1]
    n_experts = moe_w_router.shape[2]
    assert s % ATT_QUERIES == 0 and ATT_QUERIES % ATT_KEYS == 0 and ATT_KEYS % MOBA_BLOCK == 0
    assert ATT_QUERIES % ATT_SUB == 0
    tm = min(512, s)

    c_pad = jnp.zeros((8, d), F32).at[:b].set(c)
    mods = _ada(c_pad, w_ada, b_ada)[:, :b].reshape(w_ada.shape[0], b, 1, 6 * d)

    w_in = attn_w_in[0]
    w_qkv = w_in[:, :3 * aw].astype(BF16)
    w_f = jnp.zeros((d, LANES), F32).at[:, :N_FOX_HEADS].set(w_in[:, 3 * aw:]).astype(BF16)
    b_f = jnp.zeros((1, LANES), F32).at[0, :N_FOX_HEADS].set(fox_b_f[0])
    rope_c, rope_sa, rope_sb = _rope_tables(s)
    q, k, v, logf = _l0_in(x, mods, norm_g[0, 0][None], w_qkv, w_f, b_f, rope_c, rope_sa, rope_sb, tm)

    n_blk = s // MOBA_BLOCK
    qaux_m = _moba_gate(q, k, N_MOBA_HEADS)
    blk_of_pos = jnp.arange(s, dtype=jnp.int32) // MOBA_BLOCK
    kaux_row = jnp.arange(LANES, dtype=jnp.int32)[None, :] == blk_of_pos[:, None]
    kaux_m = jnp.broadcast_to(kaux_row.astype(BF16)[None, None], (1, 2, s, LANES))
    o_m = _attention(q, k, v, qaux_m, kaux_m, N_MOBA_HEADS, 0)

    qaux_f, kaux_f = _fox_prep(logf, N_FOX_HEADS)
    o_f = _attention(q, k, v, qaux_f, kaux_f, N_FOX_HEADS, N_MOBA_HEADS)

    gains = jnp.concatenate([norm_g[0, 1:4], norm_g[1, 0:1]], axis=0)
    x2, h3 = _l0_tail(o_m, o_f, x, attn_w_out[0].astype(BF16), ffn_w_gate[0].astype(BF16),
                      ffn_w_up[0].astype(BF16), ffn_w_down[0].astype(BF16), mods, gains, min(256, s))

    wax = jnp.concatenate([lru_w_a[0], lru_w_x[0]], axis=-1).astype(BF16)
    w_r = jnp.zeros((d, LANES), F32).at[:, :n_experts].set(moe_w_router[0])
    w_r_hi = w_r.astype(BF16)
    w_r_lo = (w_r - w_r_hi.astype(F32)).astype(BF16)
    w_r = jnp.concatenate([w_r_hi, w_r_lo, w_r_hi], axis=0)
    b_r = jnp.zeros((1, LANES), F32).at[0, :n_experts].set(moe_b_router[0])
    x3, h4, logits = _lru(h3, x2, lru_w_in[0].astype(BF16), lru_conv_w[0], lru_conv_b[0][None], wax,
                          lru_b_a[0][None], lru_b_x[0][None], lru_lambda[0][None], lru_w_out[0].astype(BF16),
                          mods, norm_g[1, 1][None], norm_g[1, 2][None], w_r, b_r, min(256, s))

    n_tok = b * s
    gates, slot_tok, slot_dst, block_exp, nused, n_rows = _route(
        logits.reshape(n_tok, LANES)[:, :n_experts], n_experts, MOE_BLOCK_ROWS)
    y_rows = _moe(block_exp, nused, slot_tok, slot_dst, h4.reshape(n_tok, d),
                  moe_w_gate.reshape(moe_w_gate.shape[1:]), moe_w_up.reshape(moe_w_up.shape[1:]),
                  moe_w_down.reshape(moe_w_down.shape[1:]), n_rows)
    out = _moe_out(y_rows, gates, x3.reshape(n_tok, d), mods, norm_g[1, 3][None], s, min(256, s))
    return out.reshape(b, s, d)
```

```python
import functools

import jax
import jax.numpy as jnp
from jax import lax
from jax.experimental import pallas as pl
from jax.experimental.pallas import tpu as pltpu

F32 = jnp.float32
BF16 = jnp.bfloat16
HIGHEST = lax.Precision.HIGHEST

NORM_EPS = 1e-6
HEAD_DIM = 64
N_MOBA_HEADS = 8
N_FOX_HEADS = 8
ROPE_DIMS = 16
ROPE_THETA = 500000.0
MOBA_BLOCK = 256
MOBA_TOPK = 3
CONV_WIDTH = 4
RG_C = 8.0
TOP_K = 2

LANES = 128
SUBLANES = 8
FOX_PREP_ROWS = 256
ATT_QUERIES = 2048
ATT_KEYS = 512
ATT_SUB = 256
ATT_LOOKAHEAD = 2
NEG = -1e30
LOG2E = 1.4426950408889634
MOE_BLOCK_ROWS = 512
MOE_FF_CHUNK = 512
VMEM_LIMIT = 56 * 1024 * 1024


def _cparams(sem, vmem=None):
    return pltpu.CompilerParams(dimension_semantics=sem, vmem_limit_bytes=vmem)


def _rms(x, g):
    return x * lax.rsqrt(jnp.mean(x * x, axis=-1, keepdims=True) + NORM_EPS) * g


def _const_spec(shape):
    n = len(shape)
    return pl.BlockSpec(shape, lambda *_: (0,) * n)


def _ada_kernel(c_ref, w_ref, b_ref, o_ref):
    c = c_ref[...]
    cond = c * jax.nn.sigmoid(c)
    o_ref[...] = jnp.dot(cond, w_ref[...], preferred_element_type=F32, precision=HIGHEST) + b_ref[...]


def _ada(c_pad, w_ada, b_ada):
    depth, d, d6 = w_ada.shape
    rows = c_pad.shape[0]
    nj = d6 // d
    return pl.pallas_call(
        _ada_kernel,
        grid=(depth, nj),
        in_specs=[
            pl.BlockSpec((rows, d), lambda l, j: (0, 0)),
            pl.BlockSpec((None, d, d), lambda l, j: (l, 0, j)),
            pl.BlockSpec((None, 1, d), lambda l, j: (l, 0, j)),
        ],
        out_specs=pl.BlockSpec((None, rows, d), lambda l, j: (l, 0, j)),
        out_shape=jax.ShapeDtypeStruct((depth, rows, d6), F32),
        compiler_params=_cparams(("parallel", "parallel")),
        name="ada",
    )(c_pad, w_ada, b_ada.reshape(depth, 1, d6))


def _l0_in_kernel(x_ref, mod_ref, g_ref, w_ref, wf_ref, bf_ref, rc_ref, rsa_ref, rsb_ref,
                  q_ref, k_ref, v_ref, lf_ref, *, d, aw, n_rope_chunks):
    m = mod_ref[...]
    h = _rms(x_ref[...], g_ref[...]) * (1.0 + m[:, d:2 * d]) + m[:, 0:d]
    hb = h.astype(BF16)
    proj = jnp.dot(hb, w_ref[...], preferred_element_type=F32)
    rc, rsa, rsb = rc_ref[...], rsa_ref[...], rsb_ref[...]

    def rope(t):
        return t * rc + pltpu.roll(t, ROPE_DIMS // 2, 1) * rsa + pltpu.roll(t, LANES - ROPE_DIMS // 2, 1) * rsb

    scale = HEAD_DIM ** -0.5 * LOG2E
    for c in range(aw // LANES):
        sl = slice(c * LANES, (c + 1) * LANES)
        qc = proj[:, sl] * scale
        kc = proj[:, aw + c * LANES:aw + (c + 1) * LANES]
        if c < n_rope_chunks:
            qc, kc = rope(qc), rope(kc)
        q_ref[:, sl] = qc.astype(BF16)
        k_ref[:, sl] = kc.astype(BF16)
    v_ref[...] = proj[:, 2 * aw:3 * aw].astype(BF16)
    fl = jnp.dot(hb, wf_ref[...], preferred_element_type=F32) + bf_ref[...]
    lf_ref[...] = jnp.minimum(fl, 0.0) - jnp.log(1.0 + jnp.exp(-jnp.abs(fl)))


def _l0_in(x, mods, g_pre, w_qkv, w_f, b_f, rope_c, rope_sa, rope_sb, tm):
    b, s, d = x.shape
    aw = w_qkv.shape[1] // 3
    d6 = mods.shape[-1]
    tok = lambda bi, i: (bi, i, 0)
    kern = functools.partial(_l0_in_kernel, d=d, aw=aw,
                             n_rope_chunks=N_MOBA_HEADS * HEAD_DIM // LANES)
    return pl.pallas_call(
        kern,
        grid=(b, s // tm),
        in_specs=[
            pl.BlockSpec((None, tm, d), tok),
            pl.BlockSpec((None, None, 1, d6), lambda bi, i: (0, bi, 0, 0)),
            _const_spec((1, d)),
            _const_spec(w_qkv.shape),
            _const_spec(w_f.shape),
            _const_spec((1, LANES)),
            pl.BlockSpec((tm, LANES), lambda bi, i: (i, 0)),
            pl.BlockSpec((tm, LANES), lambda bi, i: (i, 0)),
            pl.BlockSpec((tm, LANES), lambda bi, i: (i, 0)),
        ],
        out_specs=[
            pl.BlockSpec((None, tm, aw), tok),
            pl.BlockSpec((None, tm, aw), tok),
            pl.BlockSpec((None, tm, aw), tok),
            pl.BlockSpec((None, tm, LANES), tok),
        ],
        out_shape=[
            jax.ShapeDtypeStruct((b, s, aw), BF16),
            jax.ShapeDtypeStruct((b, s, aw), BF16),
            jax.ShapeDtypeStruct((b, s, aw), BF16),
            jax.ShapeDtypeStruct((b, s, LANES), F32),
        ],
        compiler_params=_cparams(("parallel", "parallel"), VMEM_LIMIT),
        name="l0_in",
    )(x, mods, g_pre, w_qkv, w_f, b_f, rope_c, rope_sa, rope_sb)


def _fox_prep_kernel(lf_ref, qa_ref, ka_ref, carry_ref, *, n_heads):
    t = lf_ref.shape[0]

    @pl.when(pl.program_id(1) == 0)
    def _():
        carry_ref[...] = jnp.zeros_like(carry_ref)

    row = lax.broadcasted_iota(jnp.int32, (t, t), 0)
    col = lax.broadcasted_iota(jnp.int32, (t, t), 1)
    tri = (col <= row).astype(F32)
    cum = jnp.dot(tri, lf_ref[...], preferred_element_type=F32, precision=HIGHEST) + carry_ref[...]
    carry_ref[...] = cum[t - 1:t, :]
    lane = lax.broadcasted_iota(jnp.int32, (t, LANES), 1)
    for h in range(n_heads):
        c = jnp.broadcast_to(cum[:, h:h + 1], (t, LANES)) * LOG2E
        hi = c.astype(BF16).astype(F32)
        r1 = c - hi
        mid = r1.astype(BF16).astype(F32)
        lo = r1 - mid
        qa = jnp.where(lane == 0, hi, jnp.where(lane == 1, mid, jnp.where(lane == 2, lo,
                       jnp.where(lane < 6, 1.0, 0.0))))
        ka = jnp.where(lane < 3, 1.0, jnp.where(lane == 3, -hi, jnp.where(lane == 4, -mid,
                       jnp.where(lane == 5, -lo, 0.0))))
        qa_ref[h] = qa.astype(BF16)
        ka_ref[h] = ka.astype(BF16)


def _fox_prep(logf, n_heads):
    b, s, _ = logf.shape
    t = min(FOX_PREP_ROWS, s)
    aux = jax.ShapeDtypeStruct((b, n_heads, s, LANES), BF16)
    aux_spec = pl.BlockSpec((None, n_heads, t, LANES), lambda bi, i: (bi, 0, i, 0))
    return pl.pallas_call(
        functools.partial(_fox_prep_kernel, n_heads=n_heads),
        grid=(b, s // t),
        in_specs=[pl.BlockSpec((None, t, LANES), lambda bi, i: (bi, i, 0))],
        out_specs=[aux_spec, aux_spec],
        out_shape=[aux, aux],
        scratch_shapes=[pltpu.VMEM((1, LANES), F32)],
        compiler_params=_cparams(("parallel", "arbitrary")),
        name="fox_prep",
    )(logf)


def _moba_gate_kernel(q_ref, k_ref, qa_ref, *, n_blk):
    s = k_ref.shape[0]
    t = MOBA_BLOCK
    rowi = lax.broadcasted_iota(jnp.int32, (n_blk, s), 0)
    cs = lax.broadcasted_iota(jnp.int32, (n_blk, s), 1)
    own = (cs >= rowi * t) & (cs < (rowi + 1) * t)
    km = jnp.dot(jnp.where(own, 1.0 / t, 0.0).astype(BF16), k_ref[...], preferred_element_type=F32)
    lane_k = lax.broadcasted_iota(jnp.int32, (n_blk, LANES), 1)
    km2 = jnp.concatenate([jnp.where(lane_k < HEAD_DIM, km, 0.0), jnp.where(lane_k >= HEAD_DIM, km, 0.0)], axis=0)
    hi = km2.astype(BF16)
    r1 = km2 - hi.astype(F32)
    mid = r1.astype(BF16)
    lo = (r1 - mid.astype(F32)).astype(BF16)
    g3 = lax.dot_general(jnp.concatenate([hi, mid, lo], axis=0), q_ref[...], (((1,), (1,)), ((), ())),
                         preferred_element_type=F32)
    g = g3[0:2 * n_blk] + g3[2 * n_blk:4 * n_blk] + g3[4 * n_blk:6 * n_blk]
    fully_past = (rowi + 1) * t <= cs
    for hh in range(2):
        gv = jnp.where(fully_past, g[hh * n_blk:(hh + 1) * n_blk], -jnp.inf)
        keep = own
        for _ in range(MOBA_TOPK):
            mx = jnp.max(gv, axis=0, keepdims=True)
            cand = jnp.where((gv == mx) & (mx > -jnp.inf), rowi, n_blk)
            pick = rowi == jnp.min(cand, axis=0, keepdims=True)
            keep = keep | pick
            gv = jnp.where(pick, -jnp.inf, gv)
        bias_t = jnp.concatenate([jnp.where(keep, 0.0, NEG), jnp.zeros((LANES - n_blk, s), F32)], axis=0)
        qa_ref[hh] = jnp.transpose(bias_t).astype(BF16)


def _moba_gate(q, k, n_heads):
    b, s, _ = q.shape
    n_blk = s // MOBA_BLOCK
    assert n_blk <= LANES
    return pl.pallas_call(
        functools.partial(_moba_gate_kernel, n_blk=n_blk),
        grid=(b, n_heads // 2),
        in_specs=[
            pl.BlockSpec((None, s, LANES), lambda bi, p: (bi, 0, p)),
            pl.BlockSpec((None, s, LANES), lambda bi, p: (bi, 0, p)),
        ],
        out_specs=pl.BlockSpec((None, 2, s, LANES), lambda bi, p: (bi, p, 0, 0)),
        out_shape=jax.ShapeDtypeStruct((b, n_heads, s, LANES), BF16),
        compiler_params=_cparams(("parallel", "parallel")),
        name="moba_gate",
    )(q, k)


def _attn_kernel(q_ref, k_ref, v_ref, qa_ref, ka_ref, o_ref):
    tq = q_ref.shape[0]
    t = ATT_KEYS
    n_sub = tq // ATT_SUB
    n_diag = tq // t
    g = pl.program_id(2)
    lane = lax.broadcasted_iota(jnp.int32, (ATT_SUB, LANES), 1)
    lane_k = lax.broadcasted_iota(jnp.int32, (t, LANES), 1)
    row = lax.broadcasted_iota(jnp.int32, (ATT_SUB, t), 0)
    col = lax.broadcasted_iota(jnp.int32, (ATT_SUB, t), 1)
    nt = (((1,), (1,)), ((), ()))
    qas = []
    for hh in range(2):
        in_head = (lane < HEAD_DIM) if hh == 0 else (lane >= HEAD_DIM)
        for r in range(n_sub):
            rs = slice(r * ATT_SUB, (r + 1) * ATT_SUB)
            q = q_ref[rs, :]
            qas.append(jnp.concatenate([jnp.where(in_head, q, jnp.zeros_like(q)), qa_ref[hh, rs, :]], axis=1))

    def q_off(c):
        return (c % n_sub) * ATT_SUB

    def update(state, steps):
        loaded = {}

        def operands(si):
            if si not in loaded:
                j = steps[si][0]
                rows = pl.ds(pl.multiple_of(j * t, t), t)
                k = k_ref[rows, :]
                v = v_ref[rows, :]
                kks = [jnp.concatenate([k, ka_ref[hh, rows, :]], axis=1) for hh in range(2)]
                vhs = [jnp.where((lane_k < HEAD_DIM) if hh == 0 else (lane_k >= HEAD_DIM), v, jnp.ones_like(v))
                       for hh in range(2)]
                loaded[si] = (kks, vhs)
            return loaded[si]

        items = [(si, c) for si, (_, d) in enumerate(steps) for c in range(2 * n_sub)
                 if d is None or q_off(c) + ATT_SUB > d * t]

        def qk(item):
            si, c = item
            return lax.dot_general(qas[c], operands(si)[0][c // n_sub], nt, preferred_element_type=F32)

        scs = {n: qk(items[n]) for n in range(min(ATT_LOOKAHEAD, len(items)))}
        new = list(state)
        for n, (si, c) in enumerate(items):
            if n + ATT_LOOKAHEAD < len(items):
                scs[n + ATT_LOOKAHEAD] = qk(items[n + ATT_LOOKAHEAD])
            d = steps[si][1]
            m_old, acc = new[c]
            sc = scs.pop(n)
            if d is not None and q_off(c) < (d + 1) * t:
                sc = jnp.where(col + d * t <= row + q_off(c), sc, NEG)
            m_new = jnp.maximum(m_old, jnp.max(sc, axis=-1, keepdims=True))
            p = jnp.exp2(sc - m_new).astype(BF16)
            acc = jnp.exp2(m_old - m_new) * acc + jnp.dot(p, operands(si)[1][c // n_sub],
                                                          preferred_element_type=F32)
            new[c] = (m_new, acc)
        return tuple(new)

    init = tuple((jnp.full((ATT_SUB, 1), NEG, F32), jnp.zeros((ATT_SUB, LANES), F32)) for _ in range(2 * n_sub))
    state = lax.fori_loop(0, g * n_diag, lambda j, st: update(st, [(j, None)]), init)
    state = update(state, [(g * n_diag + d, d) for d in range(n_diag)])
    for r in range(n_sub):
        outs = [state[hh * n_sub + r][1] for hh in range(2)]
        outs = [acc / pltpu.roll(acc, HEAD_DIM, 1) for acc in outs]
        o_ref[r * ATT_SUB:(r + 1) * ATT_SUB, :] = jnp.where(lane < HEAD_DIM, outs[0], outs[1]).astype(o_ref.dtype)


def _attention(q, k, v, qaux, kaux, n_heads, head_off):
    b, s, _ = q.shape
    t = min(ATT_QUERIES, s)
    po = head_off // 2
    kb, kh = kaux.shape[0], kaux.shape[1]
    ka_map = (lambda bi, p, i: (bi, p, 0, 0)) if kb == b and kh == n_heads else (lambda bi, p, i: (0, 0, 0, 0))
    return pl.pallas_call(
        _attn_kernel,
        grid=(b, n_heads // 2, s // t),
        in_specs=[
            pl.BlockSpec((None, t, LANES), lambda bi, p, i: (bi, i, p + po)),
            pl.BlockSpec((None, s, LANES), lambda bi, p, i: (bi, 0, p + po)),
            pl.BlockSpec((None, s, LANES), lambda bi, p, i: (bi, 0, p + po)),
            pl.BlockSpec((None, 2, t, LANES), lambda bi, p, i: (bi, p, i, 0)),
            pl.BlockSpec((None, 2, s, LANES), ka_map),
        ],
        out_specs=pl.BlockSpec((None, t, LANES), lambda bi, p, i: (bi, i, p)),
        out_shape=jax.ShapeDtypeStruct((b, s, n_heads * HEAD_DIM), BF16),
        compiler_params=_cparams(("parallel", "parallel", "arbitrary")),
        name="attn",
    )(q, k, v, qaux, kaux)


def _l0_tail_kernel(om_ref, of_ref, x_ref, wo_ref, wg_ref, wu_ref, wd_ref, mod0_ref, mod1_ref, g_ref,
                    x2_ref, h3_ref, *, d):
    m0 = mod0_ref[...]
    m1 = mod1_ref[...]
    hw = om_ref.shape[1]
    y = (jnp.dot(om_ref[...], wo_ref[0:hw, :], preferred_element_type=F32)
         + jnp.dot(of_ref[...], wo_ref[hw:, :], preferred_element_type=F32))
    x1 = x_ref[...] + m0[:, 2 * d:3 * d] * _rms(y, g_ref[0:1, :])
    h = (_rms(x1, g_ref[1:2, :]) * (1.0 + m0[:, 4 * d:5 * d]) + m0[:, 3 * d:4 * d]).astype(BF16)
    a = jnp.dot(h, wg_ref[...], preferred_element_type=F32)
    u = jnp.dot(h, wu_ref[...], preferred_element_type=F32)
    act = (a * jax.nn.sigmoid(a) * u).astype(BF16)
    y = jnp.dot(act, wd_ref[...], preferred_element_type=F32)
    x2 = x1 + m0[:, 5 * d:6 * d] * _rms(y, g_ref[2:3, :])
    x2_ref[...] = x2
    h3_ref[...] = (_rms(x2, g_ref[3:4, :]) * (1.0 + m1[:, d:2 * d]) + m1[:, 0:d]).astype(BF16)


def _l0_tail(o_m, o_f, x, w_out, w_gate, w_up, w_down, mods, gains, tm):
    b, s, d = x.shape
    d6 = mods.shape[-1]
    tok = lambda bi, i: (bi, i, 0)
    single = dict(pipeline_mode=pl.Buffered(1))
    return pl.pallas_call(
        functools.partial(_l0_tail_kernel, d=d),
        grid=(b, s // tm),
        in_specs=[
            pl.BlockSpec((None, tm, o_m.shape[2]), tok),
            pl.BlockSpec((None, tm, o_f.shape[2]), tok),
            pl.BlockSpec((None, tm, d), tok),
            pl.BlockSpec(w_out.shape, lambda bi, i: (0, 0), **single),
            pl.BlockSpec(w_gate.shape, lambda bi, i: (0, 0), **single),
            pl.BlockSpec(w_up.shape, lambda bi, i: (0, 0), **single),
            pl.BlockSpec(w_down.shape, lambda bi, i: (0, 0), **single),
            pl.BlockSpec((None, None, 1, d6), lambda bi, i: (0, bi, 0, 0)),
            pl.BlockSpec((None, None, 1, d6), lambda bi, i: (1, bi, 0, 0)),
            _const_spec(gains.shape),
        ],
        out_specs=[pl.BlockSpec((None, tm, d), tok), pl.BlockSpec((None, tm, d), tok)],
        out_shape=[jax.ShapeDtypeStruct((b, s, d), F32), jax.ShapeDtypeStruct((b, s, d), BF16)],
        compiler_params=_cparams(("parallel", "parallel"), VMEM_LIMIT),
        name="l0_tail",
    )(o_m, o_f, x, w_out, w_gate, w_up, w_down, mods, mods, gains)


def _lru_kernel(h_ref, x_ref, win_ref, cw_ref, cb_ref, wax_ref, ba_ref, bx_ref, lam_ref, wout_ref,
                mod_ref, g1_ref, g2_ref, wr_ref, br_ref,
                x3_ref, h4_ref, lg_ref, utail_ref, hc_ref, *, d, dr, n_rnn_blocks):
    tm = h_ref.shape[0]
    sub = utail_ref.shape[0]

    @pl.when(pl.program_id(1) == 0)
    def _():
        utail_ref[...] = jnp.zeros_like(utail_ref)
        hc_ref[...] = jnp.zeros_like(hc_ref)

    proj = jnp.dot(h_ref[...], win_ref[...], preferred_element_type=F32)
    gate_branch = proj[:, :dr]
    u = proj[:, dr:]
    row_sub = lax.broadcasted_iota(jnp.int32, (sub, dr), 0)
    n_groups = tm // sub
    taps = [cw_ref[j:j + 1, :] for j in range(CONV_WIDTH)]
    bias = cb_ref[...]
    prev = utail_ref[...]
    rolled_prev = [None] + [pltpu.roll(prev, back, 0) for back in range(1, CONV_WIDTH)]
    conv_groups = []
    for g in range(n_groups):
        ug = u[g * sub:(g + 1) * sub]
        cg = bias + ug * taps[CONV_WIDTH - 1]
        for back in range(1, CONV_WIDTH):
            rolled = pltpu.roll(ug, back, 0)
            cg = cg + jnp.where(row_sub >= back, rolled, rolled_prev[back]) * taps[CONV_WIDTH - 1 - back]
            rolled_prev[back] = rolled
        conv_groups.append(cg)
    conv = jnp.concatenate(conv_groups, axis=0)
    utail_ref[...] = u[tm - sub:tm]

    w = dr // n_rnn_blocks
    rs, is_ = [], []
    for n in range(n_rnn_blocks):
        cbk = conv[:, n * w:(n + 1) * w].astype(BF16)
        ra = jnp.dot(cbk, wax_ref[n], preferred_element_type=F32)
        rs.append(ra[:, :w])
        is_.append(ra[:, w:])
    r = jax.nn.sigmoid(jnp.concatenate(rs, axis=1) + ba_ref[...])
    ig = jax.nn.sigmoid(jnp.concatenate(is_, axis=1) + bx_ref[...])
    nl = -lam_ref[...]
    softplus = jnp.maximum(nl, 0.0) + jnp.log(1.0 + jnp.exp(-jnp.abs(nl)))
    log_a = (-RG_C * r) * softplus
    a = jnp.exp(log_a)
    xin = jnp.exp(0.5 * jnp.log(1.0 - a * a)) * (ig * conv)

    carry = hc_ref[...]
    groups = []
    for g in range(n_groups):
        sa, sx = a[g * sub:(g + 1) * sub], xin[g * sub:(g + 1) * sub]
        dist = 1
        while dist < sub:
            keep = row_sub >= dist
            xs = jnp.where(keep, pltpu.roll(sx, dist, 0), 0.0)
            as_ = jnp.where(keep, pltpu.roll(sa, dist, 0), 1.0)
            sx = sx + sa * xs
            sa = sa * as_
            dist *= 2
        hg = sx + sa * carry
        carry = hg[sub - 1:sub]
        groups.append(hg)
    hs = jnp.concatenate(groups, axis=0)
    hc_ref[...] = carry

    y = (jax.nn.gelu(gate_branch, approximate=True) * hs).astype(BF16)
    out = jnp.dot(y, wout_ref[...], preferred_element_type=F32)
    m = mod_ref[...]
    x3 = x_ref[...] + m[:, 2 * d:3 * d] * _rms(out, g1_ref[...])
    x3_ref[...] = x3
    h4 = _rms(x3, g2_ref[...]) * (1.0 + m[:, 4 * d:5 * d]) + m[:, 3 * d:4 * d]
    h4_ref[...] = h4
    h_hi = h4.astype(BF16)
    h_lo = (h4 - h_hi.astype(F32)).astype(BF16)
    lg_ref[...] = jnp.dot(jnp.concatenate([h_hi, h_hi, h_lo], axis=1), wr_ref[...],
                          preferred_element_type=F32) + br_ref[...]


def _lru(h3, x2, w_in, conv_w, conv_b, wax, b_a, b_x, lam, w_out, mods, g_post, g_pre2, w_r, b_r, tm):
    b, s, d = x2.shape
    dr = w_out.shape[0]
    d6 = mods.shape[-1]
    nb = wax.shape[0]
    tok = lambda bi, i: (bi, i, 0)
    return pl.pallas_call(
        functools.partial(_lru_kernel, d=d, dr=dr, n_rnn_blocks=nb),
        grid=(b, s // tm),
        in_specs=[
            pl.BlockSpec((None, tm, d), tok),
            pl.BlockSpec((None, tm, d), tok),
            _const_spec(w_in.shape),
            _const_spec(conv_w.shape),
            _const_spec((1, dr)),
            _const_spec(wax.shape),
            _const_spec((1, dr)),
            _const_spec((1, dr)),
            _const_spec((1, dr)),
            _const_spec(w_out.shape),
            pl.BlockSpec((None, None, 1, d6), lambda bi, i: (1, bi, 0, 0)),
            _const_spec((1, d)),
            _const_spec((1, d)),
            _const_spec(w_r.shape),
            _const_spec((1, LANES)),
        ],
        out_specs=[pl.BlockSpec((None, tm, d), tok), pl.BlockSpec((None, tm, d), tok),
                   pl.BlockSpec((None, tm, LANES), tok)],
        out_shape=[jax.ShapeDtypeStruct((b, s, d), F32), jax.ShapeDtypeStruct((b, s, d), F32),
                   jax.ShapeDtypeStruct((b, s, LANES), F32)],
        scratch_shapes=[pltpu.VMEM((SUBLANES, dr), F32), pltpu.VMEM((1, dr), F32)],
        compiler_params=_cparams(("parallel", "arbitrary"), VMEM_LIMIT),
        name="lru",
    )(h3, x2, w_in, conv_w, conv_b, wax, b_a, b_x, lam, w_out, mods, g_post, g_pre2, w_r, b_r)


def _moe_kernel(bexp_ref, nused_ref, tok_ref, dst_ref, h_hbm, wg_hbm, wu_hbm, wd_hbm, o_hbm,
                xbuf, ybuf, wg_bf, wu_bf, wd_bf, stage_g, stage_u, stage_d, gsem, ssem, wsem,
                *, n_blocks, dummy_base):
    mb = xbuf.shape[1]
    i = pl.program_id(0)
    nused = nused_ref[0]
    slot = lax.rem(i, 2)
    other = 1 - slot

    def gather_copy(blk, buf, r):
        tok = tok_ref[blk * mb + r]
        return pltpu.make_async_copy(h_hbm.at[pl.ds(tok, 1), :], xbuf.at[buf, pl.ds(r, 1), :], gsem.at[buf])

    def scatter_copy(dst, buf, r):
        return pltpu.make_async_copy(ybuf.at[buf, pl.ds(r, 1), :], o_hbm.at[pl.ds(dst, 1), :], ssem.at[buf])

    def wait_gather(buf):
        pltpu.make_async_copy(h_hbm.at[pl.ds(0, mb), :], xbuf.at[buf], gsem.at[buf]).wait()

    def wait_scatter(buf):
        pltpu.make_async_copy(ybuf.at[buf], o_hbm.at[pl.ds(0, mb), :], ssem.at[buf]).wait()

    @pl.when(i == 0)
    def _():
        def body(r, c):
            gather_copy(0, 0, r).start()
            return c
        lax.fori_loop(0, mb, body, 0, unroll=8)
        ybuf[1] = jnp.zeros(ybuf.shape[1:], F32)

    @pl.when((i >= 1) & (i <= nused))
    def _():
        wait_scatter(slot)

    ck = MOE_FF_CHUNK
    n_chunks = wg_bf.shape[1] // ck
    e_cur = bexp_ref[jnp.minimum(i, n_blocks - 1)]
    e_prev = bexp_ref[jnp.maximum(i - 1, 0)]
    new_expert = (i == 0) | (e_cur != e_prev)

    def weight_copies(c, buf):
        cs = pl.ds(c * ck, ck)
        return (pltpu.make_async_copy(wg_hbm.at[e_cur, :, cs], stage_g.at[buf], wsem.at[buf]),
                pltpu.make_async_copy(wu_hbm.at[e_cur, :, cs], stage_u.at[buf], wsem.at[buf]),
                pltpu.make_async_copy(wd_hbm.at[e_cur, cs, :], stage_d.at[buf], wsem.at[buf]))

    def compute(load_weights):
        wait_gather(slot)
        x = xbuf[slot].astype(BF16)
        acc = jnp.zeros((mb, wd_bf.shape[1]), F32)
        per = -(-mb // n_chunks)
        nxt = jnp.minimum(i + 1, n_blocks - 1)
        prv = jnp.maximum(i - 1, 0)
        if load_weights:
            for c in range(min(2, n_chunks)):
                for cp in weight_copies(c, c % 2):
                    cp.start()
        for c in range(n_chunks):
            for r in range(c * per, min((c + 1) * per, mb)):
                gather_copy(nxt, other, r).start()
                dst = jnp.where(i == 0, dummy_base + r, dst_ref[prv * mb + r])
                scatter_copy(dst, other, r).start(priority=1)
            cs = slice(c * ck, (c + 1) * ck)
            if load_weights:
                for cp in weight_copies(c, c % 2):
                    cp.wait()
                wg_bf[:, cs] = stage_g[c % 2].astype(BF16)
                wu_bf[:, cs] = stage_u[c % 2].astype(BF16)
                wd_bf[cs, :] = stage_d[c % 2].astype(BF16)
                if c + 2 < n_chunks:
                    for cp in weight_copies(c + 2, c % 2):
                        cp.start()
            a = jnp.dot(x, wg_bf[:, cs], preferred_element_type=F32)
            u = jnp.dot(x, wu_bf[:, cs], preferred_element_type=F32)
            act = (a * jax.nn.sigmoid(a) * u).astype(BF16)
            acc = acc + jnp.dot(act, wd_bf[cs, :], preferred_element_type=F32)
        ybuf[slot] = acc

    @pl.when((i < nused) & new_expert)
    def _():
        compute(True)

    @pl.when((i < nused) & jnp.logical_not(new_expert))
    def _():
        compute(False)

    @pl.when(i == nused)
    def _():
        wait_gather(slot)

        def body(r, c):
            scatter_copy(dst_ref[(i - 1) * mb + r], other, r).start()
            return c
        lax.fori_loop(0, mb, body, 0, unroll=8)
        wait_scatter(other)


def _moe(block_exp, nused, slot_tok, slot_dst, h4, w_gate, w_up, w_down, n_out_rows):
    n_tok, d = h4.shape
    mb = MOE_BLOCK_ROWS
    cap = slot_tok.shape[0]
    n_blocks = cap // mb
    e, _, dff = w_gate.shape
    ck = MOE_FF_CHUNK
    assert dff % ck == 0
    grid_spec = pltpu.PrefetchScalarGridSpec(
        num_scalar_prefetch=4,
        grid=(n_blocks + 1,),
        in_specs=[pl.BlockSpec(memory_space=pl.ANY)] * 4,
        out_specs=pl.BlockSpec(memory_space=pl.ANY),
        scratch_shapes=[pltpu.VMEM((2, mb, d), F32), pltpu.VMEM((2, mb, d), F32),
                        pltpu.VMEM((d, dff), BF16), pltpu.VMEM((d, dff), BF16), pltpu.VMEM((dff, d), BF16),
                        pltpu.VMEM((2, d, ck), F32), pltpu.VMEM((2, d, ck), F32), pltpu.VMEM((2, ck, d), F32),
                        pltpu.SemaphoreType.DMA((2,)), pltpu.SemaphoreType.DMA((2,)),
                        pltpu.SemaphoreType.DMA((2,))],
    )
    return pl.pallas_call(
        functools.partial(_moe_kernel, n_blocks=n_blocks, dummy_base=n_out_rows),
        grid_spec=grid_spec,
        out_shape=jax.ShapeDtypeStruct((n_out_rows + mb, d), F32),
        compiler_params=_cparams(("arbitrary",), VMEM_LIMIT),
        name="moe",
    )(block_exp, nused, slot_tok, slot_dst, h4, w_gate, w_up, w_down)


def _moe_out_kernel(*refs, d):
    y_refs, (gate_ref, x_ref, mod_ref, g_ref, o_ref) = refs[:TOP_K], refs[TOP_K:]
    g = gate_ref[...]
    y = y_refs[0][...] * g[:, 0:1]
    for kk in range(1, TOP_K):
        y = y + y_refs[kk][...] * g[:, kk:kk + 1]
    m = mod_ref[...]
    o_ref[...] = x_ref[...] + m[:, 5 * d:6 * d] * _rms(y, g_ref[...])


def _moe_out(y_rows, gates, x3, mods, g_post, s, tm):
    n_tok, d = x3.shape
    d6 = mods.shape[-1]
    per_b = s // tm
    nt = n_tok // tm
    y_specs = [pl.BlockSpec((tm, d), functools.partial(lambda i, kk: (kk * nt + i, 0), kk=kk))
               for kk in range(TOP_K)]
    return pl.pallas_call(
        functools.partial(_moe_out_kernel, d=d),
        grid=(nt,),
        in_specs=y_specs + [
            pl.BlockSpec((tm, TOP_K), lambda i: (i, 0)),
            pl.BlockSpec((tm, d), lambda i: (i, 0)),
            pl.BlockSpec((None, None, 1, d6), lambda i: (1, i // per_b, 0, 0)),
            pl.BlockSpec((1, d), lambda i: (0, 0)),
        ],
        out_specs=pl.BlockSpec((tm, d), lambda i: (i, 0)),
        out_shape=jax.ShapeDtypeStruct((n_tok, d), F32),
        compiler_params=_cparams(("parallel",), VMEM_LIMIT),
        name="moe_out",
    )(*([y_rows] * TOP_K), gates, x3, mods, g_post)


def _route(logits, n_experts, mb):
    n_tok = logits.shape[0]
    top_logit, top_idx = lax.top_k(logits, TOP_K)
    gates = jax.nn.softmax(top_logit, axis=-1)
    exp_flat = top_idx.reshape(-1).astype(jnp.int32)
    n_asg = n_tok * TOP_K
    onehot = (exp_flat[:, None] == jnp.arange(n_experts, dtype=jnp.int32)[None, :]).astype(jnp.int32)
    counts = jnp.sum(onehot, axis=0)
    padded = ((counts + mb - 1) // mb) * mb
    pends = jnp.cumsum(padded)
    cap = (-(-n_asg // mb) + n_experts) * mb
    pad_ends = jnp.cumsum(padded - counts)
    pad_exp = jnp.sum(jnp.arange(cap - n_asg, dtype=jnp.int32)[:, None] >= pad_ends[None, :], axis=1)
    keys = jnp.concatenate([exp_flat * 2, pad_exp.astype(jnp.int32) * 2 + 1])
    payload = jnp.concatenate([jnp.arange(n_asg, dtype=jnp.int32), jnp.full((cap - n_asg,), -1, jnp.int32)])
    _, slot_flat = lax.sort((keys, payload), num_keys=1, is_stable=True)
    valid = slot_flat >= 0
    slot_tok = jnp.where(valid, slot_flat // TOP_K, 0)
    slot_dst = jnp.where(valid, (slot_flat % TOP_K) * n_tok + slot_flat // TOP_K,
                         n_asg + jnp.arange(cap, dtype=jnp.int32) % mb)
    n_blocks = cap // mb
    block_starts = jnp.arange(n_blocks, dtype=jnp.int32) * mb
    block_exp = jnp.minimum(jnp.sum(block_starts[:, None] >= pends[None, :], axis=1), n_experts - 1).astype(jnp.int32)
    nused = (pends[-1] // mb).astype(jnp.int32).reshape(1)
    return gates, slot_tok, slot_dst, block_exp, nused, n_asg


def _rope_tables(s):
    half = ROPE_DIMS // 2
    inv_freq = jnp.power(ROPE_THETA, -jnp.arange(half, dtype=F32) / half)
    ang = jnp.arange(s, dtype=F32)[:, None] * inv_freq[None, :]
    cos, sin = jnp.cos(ang), jnp.sin(ang)
    lane = jnp.arange(LANES) % HEAD_DIM
    idx = lane % half
    is_x1 = lane < half
    is_x2 = (lane >= half) & (lane < ROPE_DIMS)
    c = jnp.where((is_x1 | is_x2)[None, :], cos[:, idx], 1.0)
    sa = jnp.where(is_x2[None, :], sin[:, idx], 0.0)
    sb = jnp.where(is_x1[None, :], -sin[:, idx], 0.0)
    return c.astype(F32), sa.astype(F32), sb.astype(F32)


def kernel(x, c, w_ada, b_ada, norm_g, attn_w_in, fox_b_f, attn_w_out, ffn_w_gate, ffn_w_up, ffn_w_down,
           lru_w_in, lru_conv_w, lru_conv_b, lru_w_a, lru_b_a, lru_w_x, lru_b_x, lru_lambda, lru_w_out,
           moe_w_router, moe_b_router, moe_w_gate, moe_w_up, moe_w_down):
    b, s, d = x.shape
    aw = attn_w_out.shape[1]
    n_experts = moe_w_router.shape[2]
    assert s % min(ATT_QUERIES, s) == 0 and s % ATT_KEYS == 0
    assert ATT_QUERIES % ATT_KEYS == 0 and ATT_KEYS % MOBA_BLOCK == 0
    assert ATT_QUERIES % ATT_SUB == 0
    tm = min(512, s)

    c_pad = jnp.zeros((8, d), F32).at[:b].set(c)
    mods = _ada(c_pad, w_ada, b_ada)[:, :b].reshape(w_ada.shape[0], b, 1, 6 * d)

    w_in = attn_w_in[0]
    w_qkv = w_in[:, :3 * aw].astype(BF16)
    w_f = jnp.zeros((d, LANES), F32).at[:, :N_FOX_HEADS].set(w_in[:, 3 * aw:]).astype(BF16)
    b_f = jnp.zeros((1, LANES), F32).at[0, :N_FOX_HEADS].set(fox_b_f[0])
    rope_c, rope_sa, rope_sb = _rope_tables(s)
    q, k, v, logf = _l0_in(x, mods, norm_g[0, 0][None], w_qkv, w_f, b_f, rope_c, rope_sa, rope_sb, tm)

    n_blk = s // MOBA_BLOCK
    qaux_m = _moba_gate(q, k, N_MOBA_HEADS)
    blk_of_pos = jnp.arange(s, dtype=jnp.int32) // MOBA_BLOCK
    kaux_row = jnp.arange(LANES, dtype=jnp.int32)[None, :] == blk_of_pos[:, None]
    kaux_m = jnp.broadcast_to(kaux_row.astype(BF16)[None, None], (1, 2, s, LANES))
    o_m = _attention(q, k, v, qaux_m, kaux_m, N_MOBA_HEADS, 0)

    qaux_f, kaux_f = _fox_prep(logf, N_FOX_HEADS)
    o_f = _attention(q, k, v, qaux_f, kaux_f, N_FOX_HEADS, N_MOBA_HEADS)

    gains = jnp.concatenate([norm_g[0, 1:4], norm_g[1, 0:1]], axis=0)
    x2, h3 = _l0_tail(o_m, o_f, x, attn_w_out[0].astype(BF16), ffn_w_gate[0].astype(BF16),
                      ffn_w_up[0].astype(BF16), ffn_w_down[0].astype(BF16), mods, gains, min(256, s))

    wax = jnp.concatenate([lru_w_a[0], lru_w_x[0]], axis=-1).astype(BF16)
    w_r = jnp.zeros((d, LANES), F32).at[:, :n_experts].set(moe_w_router[0])
    w_r_hi = w_r.astype(BF16)
    w_r_lo = (w_r - w_r_hi.astype(F32)).astype(BF16)
    w_r = jnp.concatenate([w_r_hi, w_r_lo, w_r_hi], axis=0)
    b_r = jnp.zeros((1, LANES), F32).at[0, :n_experts].set(moe_b_router[0])
    x3, h4, logits = _lru(h3, x2, lru_w_in[0].astype(BF16), lru_conv_w[0], lru_conv_b[0][None], wax,
                          lru_b_a[0][None], lru_b_x[0][None], lru_lambda[0][None], lru_w_out[0].astype(BF16),
                          mods, norm_g[1, 1][None], norm_g[1, 2][None], w_r, b_r, min(256, s))

    n_tok = b * s
    gates, slot_tok, slot_dst, block_exp, nused, n_rows = _route(
        logits.reshape(n_tok, LANES)[:, :n_experts], n_experts, MOE_BLOCK_ROWS)
    y_rows = _moe(block_exp, nused, slot_tok, slot_dst, h4.reshape(n_tok, d),
                  moe_w_gate.reshape(moe_w_gate.shape[1:]), moe_w_up.reshape(moe_w_up.shape[1:]),
                  moe_w_down.reshape(moe_w_down.shape[1:]), n_rows)
    out = _moe_out(y_rows, gates, x3.reshape(n_tok, d), mods, norm_g[1, 3][None], s, min(256, s))
    return out.reshape(b, s, d)
```

```python
import functools

import jax
import jax.numpy as jnp
from jax import lax
from jax.experimental import pallas as pl
from jax.experimental.pallas import tpu as pltpu

F32 = jnp.float32
BF16 = jnp.bfloat16
HIGHEST = lax.Precision.HIGHEST

NORM_EPS = 1e-6
HEAD_DIM = 64
N_MOBA_HEADS = 8
N_FOX_HEADS = 8
ROPE_DIMS = 16
ROPE_THETA = 500000.0
MOBA_BLOCK = 256
MOBA_TOPK = 3
CONV_WIDTH = 4
RG_C = 8.0
TOP_K = 2

LANES = 128
SUBLANES = 8
FOX_PREP_ROWS = 256
ATT_QUERIES = 2048
ATT_KEYS = 512
ATT_SUB = 256
ATT_LOOKAHEAD = 6
NEG = -1e30
LOG2E = 1.4426950408889634
MOE_BLOCK_ROWS = 512
MOE_FF_CHUNK = 512
VMEM_LIMIT = 56 * 1024 * 1024


def _cparams(sem, vmem=None):
    return pltpu.CompilerParams(dimension_semantics=sem, vmem_limit_bytes=vmem)


def _rms(x, g):
    return x * lax.rsqrt(jnp.mean(x * x, axis=-1, keepdims=True) + NORM_EPS) * g


def _const_spec(shape):
    n = len(shape)
    return pl.BlockSpec(shape, lambda *_: (0,) * n)


def _ada_kernel(c_ref, w_ref, b_ref, o_ref):
    c = c_ref[...]
    cond = c * jax.nn.sigmoid(c)
    o_ref[...] = jnp.dot(cond, w_ref[...], preferred_element_type=F32, precision=HIGHEST) + b_ref[...]


def _ada(c_pad, w_ada, b_ada):
    depth, d, d6 = w_ada.shape
    rows = c_pad.shape[0]
    nj = d6 // d
    return pl.pallas_call(
        _ada_kernel,
        grid=(depth, nj),
        in_specs=[
            pl.BlockSpec((rows, d), lambda l, j: (0, 0)),
            pl.BlockSpec((None, d, d), lambda l, j: (l, 0, j)),
            pl.BlockSpec((None, 1, d), lambda l, j: (l, 0, j)),
        ],
        out_specs=pl.BlockSpec((None, rows, d), lambda l, j: (l, 0, j)),
        out_shape=jax.ShapeDtypeStruct((depth, rows, d6), F32),
        compiler_params=_cparams(("parallel", "parallel")),
        name="ada",
    )(c_pad, w_ada, b_ada.reshape(depth, 1, d6))


def _l0_in_kernel(x_ref, mod_ref, g_ref, w_ref, wf_ref, bf_ref, rc_ref, rsa_ref, rsb_ref,
                  q_ref, k_ref, v_ref, lf_ref, *, d, aw, n_rope_chunks):
    m = mod_ref[...]
    h = _rms(x_ref[...], g_ref[...]) * (1.0 + m[:, d:2 * d]) + m[:, 0:d]
    hb = h.astype(BF16)
    proj = jnp.dot(hb, w_ref[...], preferred_element_type=F32)
    rc, rsa, rsb = rc_ref[...], rsa_ref[...], rsb_ref[...]

    def rope(t):
        return t * rc + pltpu.roll(t, ROPE_DIMS // 2, 1) * rsa + pltpu.roll(t, LANES - ROPE_DIMS // 2, 1) * rsb

    scale = HEAD_DIM ** -0.5 * LOG2E
    for c in range(aw // LANES):
        sl = slice(c * LANES, (c + 1) * LANES)
        qc = proj[:, sl] * scale
        kc = proj[:, aw + c * LANES:aw + (c + 1) * LANES]
        if c < n_rope_chunks:
            qc, kc = rope(qc), rope(kc)
        q_ref[:, sl] = qc.astype(BF16)
        k_ref[:, sl] = kc.astype(BF16)
    v_ref[...] = proj[:, 2 * aw:3 * aw].astype(BF16)
    fl = jnp.dot(hb, wf_ref[...], preferred_element_type=F32) + bf_ref[...]
    lf_ref[...] = jnp.minimum(fl, 0.0) - jnp.log(1.0 + jnp.exp(-jnp.abs(fl)))


def _l0_in(x, mods, g_pre, w_qkv, w_f, b_f, rope_c, rope_sa, rope_sb, tm):
    b, s, d = x.shape
    aw = w_qkv.shape[1] // 3
    d6 = mods.shape[-1]
    tok = lambda bi, i: (bi, i, 0)
    kern = functools.partial(_l0_in_kernel, d=d, aw=aw,
                             n_rope_chunks=N_MOBA_HEADS * HEAD_DIM // LANES)
    return pl.pallas_call(
        kern,
        grid=(b, s // tm),
        in_specs=[
            pl.BlockSpec((None, tm, d), tok),
            pl.BlockSpec((None, None, 1, d6), lambda bi, i: (0, bi, 0, 0)),
            _const_spec((1, d)),
            _const_spec(w_qkv.shape),
            _const_spec(w_f.shape),
            _const_spec((1, LANES)),
            pl.BlockSpec((tm, LANES), lambda bi, i: (i, 0)),
            pl.BlockSpec((tm, LANES), lambda bi, i: (i, 0)),
            pl.BlockSpec((tm, LANES), lambda bi, i: (i, 0)),
        ],
        out_specs=[
            pl.BlockSpec((None, tm, aw), tok),
            pl.BlockSpec((None, tm, aw), tok),
            pl.BlockSpec((None, tm, aw), tok),
            pl.BlockSpec((None, tm, LANES), tok),
        ],
        out_shape=[
            jax.ShapeDtypeStruct((b, s, aw), BF16),
            jax.ShapeDtypeStruct((b, s, aw), BF16),
            jax.ShapeDtypeStruct((b, s, aw), BF16),
            jax.ShapeDtypeStruct((b, s, LANES), F32),
        ],
        compiler_params=_cparams(("parallel", "parallel"), VMEM_LIMIT),
        name="l0_in",
    )(x, mods, g_pre, w_qkv, w_f, b_f, rope_c, rope_sa, rope_sb)


def _fox_prep_kernel(lf_ref, qa_ref, ka_ref, carry_ref, *, n_heads):
    t = lf_ref.shape[0]

    @pl.when(pl.program_id(1) == 0)
    def _():
        carry_ref[...] = jnp.zeros_like(carry_ref)

    row = lax.broadcasted_iota(jnp.int32, (t, t), 0)
    col = lax.broadcasted_iota(jnp.int32, (t, t), 1)
    tri = (col <= row).astype(F32)
    cum = jnp.dot(tri, lf_ref[...], preferred_element_type=F32, precision=HIGHEST) + carry_ref[...]
    carry_ref[...] = cum[t - 1:t, :]
    lane = lax.broadcasted_iota(jnp.int32, (t, LANES), 1)
    for h in range(n_heads):
        c = jnp.broadcast_to(cum[:, h:h + 1], (t, LANES)) * LOG2E
        hi = c.astype(BF16).astype(F32)
        r1 = c - hi
        mid = r1.astype(BF16).astype(F32)
        lo = r1 - mid
        qa = jnp.where(lane == 0, hi, jnp.where(lane == 1, mid, jnp.where(lane == 2, lo,
                       jnp.where(lane < 6, 1.0, 0.0))))
        ka = jnp.where(lane < 3, 1.0, jnp.where(lane == 3, -hi, jnp.where(lane == 4, -mid,
                       jnp.where(lane == 5, -lo, 0.0))))
        qa_ref[h] = qa.astype(BF16)
        ka_ref[h] = ka.astype(BF16)


def _fox_prep(logf, n_heads):
    b, s, _ = logf.shape
    t = min(FOX_PREP_ROWS, s)
    aux = jax.ShapeDtypeStruct((b, n_heads, s, LANES), BF16)
    aux_spec = pl.BlockSpec((None, n_heads, t, LANES), lambda bi, i: (bi, 0, i, 0))
    return pl.pallas_call(
        functools.partial(_fox_prep_kernel, n_heads=n_heads),
        grid=(b, s // t),
        in_specs=[pl.BlockSpec((None, t, LANES), lambda bi, i: (bi, i, 0))],
        out_specs=[aux_spec, aux_spec],
        out_shape=[aux, aux],
        scratch_shapes=[pltpu.VMEM((1, LANES), F32)],
        compiler_params=_cparams(("parallel", "arbitrary")),
        name="fox_prep",
    )(logf)


def _moba_gate_kernel(q_ref, k_ref, qa_ref, *, n_blk):
    s = k_ref.shape[0]
    t = MOBA_BLOCK
    rowi = lax.broadcasted_iota(jnp.int32, (n_blk, s), 0)
    cs = lax.broadcasted_iota(jnp.int32, (n_blk, s), 1)
    own = (cs >= rowi * t) & (cs < (rowi + 1) * t)
    km = jnp.dot(jnp.where(own, 1.0 / t, 0.0).astype(BF16), k_ref[...], preferred_element_type=F32)
    lane_k = lax.broadcasted_iota(jnp.int32, (n_blk, LANES), 1)
    km2 = jnp.concatenate([jnp.where(lane_k < HEAD_DIM, km, 0.0), jnp.where(lane_k >= HEAD_DIM, km, 0.0)], axis=0)
    hi = km2.astype(BF16)
    r1 = km2 - hi.astype(F32)
    mid = r1.astype(BF16)
    lo = (r1 - mid.astype(F32)).astype(BF16)
    g3 = lax.dot_general(jnp.concatenate([hi, mid, lo], axis=0), q_ref[...], (((1,), (1,)), ((), ())),
                         preferred_element_type=F32)
    g = g3[0:2 * n_blk] + g3[2 * n_blk:4 * n_blk] + g3[4 * n_blk:6 * n_blk]
    fully_past = (rowi + 1) * t <= cs
    for hh in range(2):
        gv = jnp.where(fully_past, g[hh * n_blk:(hh + 1) * n_blk], -jnp.inf)
        keep = own
        for _ in range(MOBA_TOPK):
            mx = jnp.max(gv, axis=0, keepdims=True)
            cand = jnp.where((gv == mx) & (mx > -jnp.inf), rowi, n_blk)
            pick = rowi == jnp.min(cand, axis=0, keepdims=True)
            keep = keep | pick
            gv = jnp.where(pick, -jnp.inf, gv)
        bias_t = jnp.concatenate([jnp.where(keep, 0.0, NEG), jnp.zeros((LANES - n_blk, s), F32)], axis=0)
        qa_ref[hh] = jnp.transpose(bias_t).astype(BF16)


def _moba_gate(q, k, n_heads):
    b, s, _ = q.shape
    n_blk = s // MOBA_BLOCK
    assert n_blk <= LANES
    return pl.pallas_call(
        functools.partial(_moba_gate_kernel, n_blk=n_blk),
        grid=(b, n_heads // 2),
        in_specs=[
            pl.BlockSpec((None, s, LANES), lambda bi, p: (bi, 0, p)),
            pl.BlockSpec((None, s, LANES), lambda bi, p: (bi, 0, p)),
        ],
        out_specs=pl.BlockSpec((None, 2, s, LANES), lambda bi, p: (bi, p, 0, 0)),
        out_shape=jax.ShapeDtypeStruct((b, n_heads, s, LANES), BF16),
        compiler_params=_cparams(("parallel", "parallel")),
        name="moba_gate",
    )(q, k)


def _attn_kernel(q_ref, k_ref, vt_ref, qa_ref, ka_ref, o_ref):
    tq = q_ref.shape[0]
    t = ATT_KEYS
    n_sub = tq // ATT_SUB
    n_diag = tq // t
    g = pl.program_id(2)
    lane = lax.broadcasted_iota(jnp.int32, (ATT_SUB, LANES), 1)
    row_v = lax.broadcasted_iota(jnp.int32, (LANES, t), 0)
    row_o = lax.broadcasted_iota(jnp.int32, (LANES, ATT_SUB), 0)
    key = lax.broadcasted_iota(jnp.int32, (t, ATT_SUB), 0)
    qry = lax.broadcasted_iota(jnp.int32, (t, ATT_SUB), 1)
    nt = (((1,), (1,)), ((), ()))
    qas = []
    for hh in range(2):
        in_head = (lane < HEAD_DIM) if hh == 0 else (lane >= HEAD_DIM)
        for r in range(n_sub):
            rs = slice(r * ATT_SUB, (r + 1) * ATT_SUB)
            q = q_ref[rs, :]
            qas.append(jnp.concatenate([jnp.where(in_head, q, jnp.zeros_like(q)), qa_ref[hh, rs, :]], axis=1))

    def q_off(c):
        return (c % n_sub) * ATT_SUB

    def update(state, steps):
        loaded = {}

        def operands(si):
            if si not in loaded:
                j = steps[si][0]
                rows = pl.ds(pl.multiple_of(j * t, t), t)
                k = k_ref[rows, :]
                vt = vt_ref[:, rows]
                kks = [jnp.concatenate([k, ka_ref[hh, rows, :]], axis=1) for hh in range(2)]
                vhs = [jnp.where((row_v < HEAD_DIM) if hh == 0 else (row_v >= HEAD_DIM), vt, jnp.ones_like(vt))
                       for hh in range(2)]
                loaded[si] = (kks, vhs)
            return loaded[si]

        items = [(si, c) for si, (_, d) in enumerate(steps) for c in range(2 * n_sub)
                 if d is None or q_off(c) + ATT_SUB > d * t]

        def qk(item):
            si, c = item
            return lax.dot_general(operands(si)[0][c // n_sub], qas[c], nt, preferred_element_type=F32)

        scs = {n: qk(items[n]) for n in range(min(ATT_LOOKAHEAD, len(items)))}
        new = list(state)
        for n, (si, c) in enumerate(items):
            if n + ATT_LOOKAHEAD < len(items):
                scs[n + ATT_LOOKAHEAD] = qk(items[n + ATT_LOOKAHEAD])
            d = steps[si][1]
            m_old, acc = new[c]
            sc = scs.pop(n)
            if d is not None and q_off(c) < (d + 1) * t:
                sc = jnp.where(key + d * t <= qry + q_off(c), sc, NEG)
            m_new = jnp.maximum(m_old, jnp.max(sc, axis=0, keepdims=True))
            p = jnp.exp2(sc - m_new).astype(BF16)
            acc = jnp.exp2(m_old - m_new) * acc + jnp.dot(operands(si)[1][c // n_sub], p,
                                                          preferred_element_type=F32)
            new[c] = (m_new, acc)
        return tuple(new)

    init = tuple((jnp.full((1, ATT_SUB), NEG, F32), jnp.zeros((LANES, ATT_SUB), F32)) for _ in range(2 * n_sub))
    state = lax.fori_loop(0, g * n_diag, lambda j, st: update(st, [(j, None)]), init)
    state = update(state, [(g * n_diag + d, d) for d in range(n_diag)])
    for r in range(n_sub):
        outs = [state[hh * n_sub + r][1] for hh in range(2)]
        outs = [acc / pltpu.roll(acc, HEAD_DIM, 0) for acc in outs]
        o_t = jnp.where(row_o < HEAD_DIM, outs[0], outs[1])
        o_ref[r * ATT_SUB:(r + 1) * ATT_SUB, :] = jnp.transpose(o_t).astype(o_ref.dtype)


def _attention(q, k, vt, qaux, kaux, n_heads, head_off):
    b, s, _ = q.shape
    t = min(ATT_QUERIES, s)
    po = head_off // 2
    kb, kh = kaux.shape[0], kaux.shape[1]
    ka_map = (lambda bi, p, i: (bi, p, 0, 0)) if kb == b and kh == n_heads else (lambda bi, p, i: (0, 0, 0, 0))
    return pl.pallas_call(
        _attn_kernel,
        grid=(b, n_heads // 2, s // t),
        in_specs=[
            pl.BlockSpec((None, t, LANES), lambda bi, p, i: (bi, i, p + po)),
            pl.BlockSpec((None, s, LANES), lambda bi, p, i: (bi, 0, p + po)),
            pl.BlockSpec((None, LANES, s), lambda bi, p, i: (bi, p + po, 0)),
            pl.BlockSpec((None, 2, t, LANES), lambda bi, p, i: (bi, p, i, 0)),
            pl.BlockSpec((None, 2, s, LANES), ka_map),
        ],
        out_specs=pl.BlockSpec((None, t, LANES), lambda bi, p, i: (bi, i, p)),
        out_shape=jax.ShapeDtypeStruct((b, s, n_heads * HEAD_DIM), BF16),
        compiler_params=_cparams(("parallel", "parallel", "arbitrary")),
        name="attn",
    )(q, k, vt, qaux, kaux)


def _l0_tail_kernel(om_ref, of_ref, x_ref, wo_ref, wg_ref, wu_ref, wd_ref, mod0_ref, mod1_ref, g_ref,
                    x2_ref, h3_ref, *, d):
    m0 = mod0_ref[...]
    m1 = mod1_ref[...]
    hw = om_ref.shape[1]
    y = (jnp.dot(om_ref[...], wo_ref[0:hw, :], preferred_element_type=F32)
         + jnp.dot(of_ref[...], wo_ref[hw:, :], preferred_element_type=F32))
    x1 = x_ref[...] + m0[:, 2 * d:3 * d] * _rms(y, g_ref[0:1, :])
    h = (_rms(x1, g_ref[1:2, :]) * (1.0 + m0[:, 4 * d:5 * d]) + m0[:, 3 * d:4 * d]).astype(BF16)
    a = jnp.dot(h, wg_ref[...], preferred_element_type=F32)
    u = jnp.dot(h, wu_ref[...], preferred_element_type=F32)
    act = (a * jax.nn.sigmoid(a) * u).astype(BF16)
    y = jnp.dot(act, wd_ref[...], preferred_element_type=F32)
    x2 = x1 + m0[:, 5 * d:6 * d] * _rms(y, g_ref[2:3, :])
    x2_ref[...] = x2
    h3_ref[...] = (_rms(x2, g_ref[3:4, :]) * (1.0 + m1[:, d:2 * d]) + m1[:, 0:d]).astype(BF16)


def _l0_tail(o_m, o_f, x, w_out, w_gate, w_up, w_down, mods, gains, tm):
    b, s, d = x.shape
    d6 = mods.shape[-1]
    tok = lambda bi, i: (bi, i, 0)
    single = dict(pipeline_mode=pl.Buffered(1))
    return pl.pallas_call(
        functools.partial(_l0_tail_kernel, d=d),
        grid=(b, s // tm),
        in_specs=[
            pl.BlockSpec((None, tm, o_m.shape[2]), tok),
            pl.BlockSpec((None, tm, o_f.shape[2]), tok),
            pl.BlockSpec((None, tm, d), tok),
            pl.BlockSpec(w_out.shape, lambda bi, i: (0, 0), **single),
            pl.BlockSpec(w_gate.shape, lambda bi, i: (0, 0), **single),
            pl.BlockSpec(w_up.shape, lambda bi, i: (0, 0), **single),
            pl.BlockSpec(w_down.shape, lambda bi, i: (0, 0), **single),
            pl.BlockSpec((None, None, 1, d6), lambda bi, i: (0, bi, 0, 0)),
            pl.BlockSpec((None, None, 1, d6), lambda bi, i: (1, bi, 0, 0)),
            _const_spec(gains.shape),
        ],
        out_specs=[pl.BlockSpec((None, tm, d), tok), pl.BlockSpec((None, tm, d), tok)],
        out_shape=[jax.ShapeDtypeStruct((b, s, d), F32), jax.ShapeDtypeStruct((b, s, d), BF16)],
        compiler_params=_cparams(("parallel", "parallel"), VMEM_LIMIT),
        name="l0_tail",
    )(o_m, o_f, x, w_out, w_gate, w_up, w_down, mods, mods, gains)


def _lru_kernel(h_ref, x_ref, win_ref, cw_ref, cb_ref, wax_ref, ba_ref, bx_ref, lam_ref, wout_ref,
                mod_ref, g1_ref, g2_ref, wr_ref, br_ref,
                x3_ref, h4_ref, lg_ref, utail_ref, hc_ref, *, d, dr, n_rnn_blocks):
    tm = h_ref.shape[0]
    sub = utail_ref.shape[0]

    @pl.when(pl.program_id(1) == 0)
    def _():
        utail_ref[...] = jnp.zeros_like(utail_ref)
        hc_ref[...] = jnp.zeros_like(hc_ref)

    proj = jnp.dot(h_ref[...], win_ref[...], preferred_element_type=F32)
    gate_branch = proj[:, :dr]
    u = proj[:, dr:]
    row_sub = lax.broadcasted_iota(jnp.int32, (sub, dr), 0)
    n_groups = tm // sub
    taps = [cw_ref[j:j + 1, :] for j in range(CONV_WIDTH)]
    bias = cb_ref[...]
    prev = utail_ref[...]
    rolled_prev = [None] + [pltpu.roll(prev, back, 0) for back in range(1, CONV_WIDTH)]
    conv_groups = []
    for g in range(n_groups):
        ug = u[g * sub:(g + 1) * sub]
        cg = bias + ug * taps[CONV_WIDTH - 1]
        for back in range(1, CONV_WIDTH):
            rolled = pltpu.roll(ug, back, 0)
            cg = cg + jnp.where(row_sub >= back, rolled, rolled_prev[back]) * taps[CONV_WIDTH - 1 - back]
            rolled_prev[back] = rolled
        conv_groups.append(cg)
    conv = jnp.concatenate(conv_groups, axis=0)
    utail_ref[...] = u[tm - sub:tm]

    w = dr // n_rnn_blocks
    rs, is_ = [], []
    for n in range(n_rnn_blocks):
        cbk = conv[:, n * w:(n + 1) * w].astype(BF16)
        ra = jnp.dot(cbk, wax_ref[n], preferred_element_type=F32)
        rs.append(ra[:, :w])
        is_.append(ra[:, w:])
    r = jax.nn.sigmoid(jnp.concatenate(rs, axis=1) + ba_ref[...])
    ig = jax.nn.sigmoid(jnp.concatenate(is_, axis=1) + bx_ref[...])
    nl = -lam_ref[...]
    softplus = jnp.maximum(nl, 0.0) + jnp.log(1.0 + jnp.exp(-jnp.abs(nl)))
    log_a = (-RG_C * r) * softplus
    a = jnp.exp(log_a)
    xin = jnp.exp(0.5 * jnp.log(1.0 - a * a)) * (ig * conv)

    carry = hc_ref[...]
    groups = []
    for g in range(n_groups):
        sa, sx = a[g * sub:(g + 1) * sub], xin[g * sub:(g + 1) * sub]
        dist = 1
        while dist < sub:
            keep = row_sub >= dist
            xs = jnp.where(keep, pltpu.roll(sx, dist, 0), 0.0)
            as_ = jnp.where(keep, pltpu.roll(sa, dist, 0), 1.0)
            sx = sx + sa * xs
            sa = sa * as_
            dist *= 2
        hg = sx + sa * carry
        carry = hg[sub - 1:sub]
        groups.append(hg)
    hs = jnp.concatenate(groups, axis=0)
    hc_ref[...] = carry

    y = (jax.nn.gelu(gate_branch, approximate=True) * hs).astype(BF16)
    out = jnp.dot(y, wout_ref[...], preferred_element_type=F32)
    m = mod_ref[...]
    x3 = x_ref[...] + m[:, 2 * d:3 * d] * _rms(out, g1_ref[...])
    x3_ref[...] = x3
    h4 = _rms(x3, g2_ref[...]) * (1.0 + m[:, 4 * d:5 * d]) + m[:, 3 * d:4 * d]
    h4_ref[...] = h4
    h_hi = h4.astype(BF16)
    h_lo = (h4 - h_hi.astype(F32)).astype(BF16)
    lg_ref[...] = jnp.dot(jnp.concatenate([h_hi, h_hi, h_lo], axis=1), wr_ref[...],
                          preferred_element_type=F32) + br_ref[...]


def _lru(h3, x2, w_in, conv_w, conv_b, wax, b_a, b_x, lam, w_out, mods, g_post, g_pre2, w_r, b_r, tm):
    b, s, d = x2.shape
    dr = w_out.shape[0]
    d6 = mods.shape[-1]
    nb = wax.shape[0]
    tok = lambda bi, i: (bi, i, 0)
    return pl.pallas_call(
        functools.partial(_lru_kernel, d=d, dr=dr, n_rnn_blocks=nb),
        grid=(b, s // tm),
        in_specs=[
            pl.BlockSpec((None, tm, d), tok),
            pl.BlockSpec((None, tm, d), tok),
            _const_spec(w_in.shape),
            _const_spec(conv_w.shape),
            _const_spec((1, dr)),
            _const_spec(wax.shape),
            _const_spec((1, dr)),
            _const_spec((1, dr)),
            _const_spec((1, dr)),
            _const_spec(w_out.shape),
            pl.BlockSpec((None, None, 1, d6), lambda bi, i: (1, bi, 0, 0)),
            _const_spec((1, d)),
            _const_spec((1, d)),
            _const_spec(w_r.shape),
            _const_spec((1, LANES)),
        ],
        out_specs=[pl.BlockSpec((None, tm, d), tok), pl.BlockSpec((None, tm, d), tok),
                   pl.BlockSpec((None, tm, LANES), tok)],
        out_shape=[jax.ShapeDtypeStruct((b, s, d), F32), jax.ShapeDtypeStruct((b, s, d), F32),
                   jax.ShapeDtypeStruct((b, s, LANES), F32)],
        scratch_shapes=[pltpu.VMEM((SUBLANES, dr), F32), pltpu.VMEM((1, dr), F32)],
        compiler_params=_cparams(("parallel", "arbitrary"), VMEM_LIMIT),
        name="lru",
    )(h3, x2, w_in, conv_w, conv_b, wax, b_a, b_x, lam, w_out, mods, g_post, g_pre2, w_r, b_r)


def _moe_kernel(bexp_ref, nused_ref, tok_ref, dst_ref, h_hbm, wg_hbm, wu_hbm, wd_hbm, o_hbm,
                xbuf, ybuf, wg_bf, wu_bf, wd_bf, stage_g, stage_u, stage_d, gsem, ssem, wsem,
                *, n_blocks, dummy_base):
    mb = xbuf.shape[1]
    i = pl.program_id(0)
    nused = nused_ref[0]
    slot = lax.rem(i, 2)
    other = 1 - slot

    def gather_copy(blk, buf, r):
        tok = tok_ref[blk * mb + r]
        return pltpu.make_async_copy(h_hbm.at[pl.ds(tok, 1), :], xbuf.at[buf, pl.ds(r, 1), :], gsem.at[buf])

    def scatter_copy(dst, buf, r):
        return pltpu.make_async_copy(ybuf.at[buf, pl.ds(r, 1), :], o_hbm.at[pl.ds(dst, 1), :], ssem.at[buf])

    def wait_gather(buf):
        pltpu.make_async_copy(h_hbm.at[pl.ds(0, mb), :], xbuf.at[buf], gsem.at[buf]).wait()

    def wait_scatter(buf):
        pltpu.make_async_copy(ybuf.at[buf], o_hbm.at[pl.ds(0, mb), :], ssem.at[buf]).wait()

    @pl.when(i == 0)
    def _():
        def body(r, c):
            gather_copy(0, 0, r).start()
            return c
        lax.fori_loop(0, mb, body, 0, unroll=8)
        ybuf[1] = jnp.zeros(ybuf.shape[1:], F32)

    @pl.when((i >= 1) & (i <= nused))
    def _():
        wait_scatter(slot)

    ck = MOE_FF_CHUNK
    n_chunks = wg_bf.shape[1] // ck
    e_cur = bexp_ref[jnp.minimum(i, n_blocks - 1)]
    e_prev = bexp_ref[jnp.maximum(i - 1, 0)]
    new_expert = (i == 0) | (e_cur != e_prev)

    def weight_copies(c, buf):
        cs = pl.ds(c * ck, ck)
        return (pltpu.make_async_copy(wg_hbm.at[e_cur, :, cs], stage_g.at[buf], wsem.at[buf]),
                pltpu.make_async_copy(wu_hbm.at[e_cur, :, cs], stage_u.at[buf], wsem.at[buf]),
                pltpu.make_async_copy(wd_hbm.at[e_cur, cs, :], stage_d.at[buf], wsem.at[buf]))

    def compute(load_weights):
        wait_gather(slot)
        x = xbuf[slot].astype(BF16)
        acc = jnp.zeros((mb, wd_bf.shape[1]), F32)
        per = -(-mb // n_chunks)
        nxt = jnp.minimum(i + 1, n_blocks - 1)
        prv = jnp.maximum(i - 1, 0)
        if load_weights:
            for c in range(min(2, n_chunks)):
                for cp in weight_copies(c, c % 2):
                    cp.start()
        for c in range(n_chunks):
            for r in range(c * per, min((c + 1) * per, mb)):
                gather_copy(nxt, other, r).start()
                dst = jnp.where(i == 0, dummy_base + r, dst_ref[prv * mb + r])
                scatter_copy(dst, other, r).start(priority=1)
            cs = slice(c * ck, (c + 1) * ck)
            if load_weights:
                for cp in weight_copies(c, c % 2):
                    cp.wait()
                wg_bf[:, cs] = stage_g[c % 2].astype(BF16)
                wu_bf[:, cs] = stage_u[c % 2].astype(BF16)
                wd_bf[cs, :] = stage_d[c % 2].astype(BF16)
                if c + 2 < n_chunks:
                    for cp in weight_copies(c + 2, c % 2):
                        cp.start()
            a = jnp.dot(x, wg_bf[:, cs], preferred_element_type=F32)
            u = jnp.dot(x, wu_bf[:, cs], preferred_element_type=F32)
            act = (a * jax.nn.sigmoid(a) * u).astype(BF16)
            acc = acc + jnp.dot(act, wd_bf[cs, :], preferred_element_type=F32)
        ybuf[slot] = acc

    @pl.when((i < nused) & new_expert)
    def _():
        compute(True)

    @pl.when((i < nused) & jnp.logical_not(new_expert))
    def _():
        compute(False)

    @pl.when(i == nused)
    def _():
        wait_gather(slot)

        def body(r, c):
            scatter_copy(dst_ref[(i - 1) * mb + r], other, r).start()
            return c
        lax.fori_loop(0, mb, body, 0, unroll=8)
        wait_scatter(other)


def _moe(block_exp, nused, slot_tok, slot_dst, h4, w_gate, w_up, w_down, n_out_rows):
    n_tok, d = h4.shape
    mb = MOE_BLOCK_ROWS
    cap = slot_tok.shape[0]
    n_blocks = cap // mb
    e, _, dff = w_gate.shape
    ck = MOE_FF_CHUNK
    assert dff % ck == 0
    grid_spec = pltpu.PrefetchScalarGridSpec(
        num_scalar_prefetch=4,
        grid=(n_blocks + 1,),
        in_specs=[pl.BlockSpec(memory_space=pl.ANY)] * 4,
        out_specs=pl.BlockSpec(memory_space=pl.ANY),
        scratch_shapes=[pltpu.VMEM((2, mb, d), F32), pltpu.VMEM((2, mb, d), F32),
                        pltpu.VMEM((d, dff), BF16), pltpu.VMEM((d, dff), BF16), pltpu.VMEM((dff, d), BF16),
                        pltpu.VMEM((2, d, ck), F32), pltpu.VMEM((2, d, ck), F32), pltpu.VMEM((2, ck, d), F32),
                        pltpu.SemaphoreType.DMA((2,)), pltpu.SemaphoreType.DMA((2,)),
                        pltpu.SemaphoreType.DMA((2,))],
    )
    return pl.pallas_call(
        functools.partial(_moe_kernel, n_blocks=n_blocks, dummy_base=n_out_rows),
        grid_spec=grid_spec,
        out_shape=jax.ShapeDtypeStruct((n_out_rows + mb, d), F32),
        compiler_params=_cparams(("arbitrary",), VMEM_LIMIT),
        name="moe",
    )(block_exp, nused, slot_tok, slot_dst, h4, w_gate, w_up, w_down)


def _moe_out_kernel(*refs, d):
    y_refs, (gate_ref, x_ref, mod_ref, g_ref, o_ref) = refs[:TOP_K], refs[TOP_K:]
    g = gate_ref[...]
    y = y_refs[0][...] * g[:, 0:1]
    for kk in range(1, TOP_K):
        y = y + y_refs[kk][...] * g[:, kk:kk + 1]
    m = mod_ref[...]
    o_ref[...] = x_ref[...] + m[:, 5 * d:6 * d] * _rms(y, g_ref[...])


def _moe_out(y_rows, gates, x3, mods, g_post, s, tm):
    n_tok, d = x3.shape
    d6 = mods.shape[-1]
    per_b = s // tm
    nt = n_tok // tm
    y_specs = [pl.BlockSpec((tm, d), functools.partial(lambda i, kk: (kk * nt + i, 0), kk=kk))
               for kk in range(TOP_K)]
    return pl.pallas_call(
        functools.partial(_moe_out_kernel, d=d),
        grid=(nt,),
        in_specs=y_specs + [
            pl.BlockSpec((tm, TOP_K), lambda i: (i, 0)),
            pl.BlockSpec((tm, d), lambda i: (i, 0)),
            pl.BlockSpec((None, None, 1, d6), lambda i: (1, i // per_b, 0, 0)),
            pl.BlockSpec((1, d), lambda i: (0, 0)),
        ],
        out_specs=pl.BlockSpec((tm, d), lambda i: (i, 0)),
        out_shape=jax.ShapeDtypeStruct((n_tok, d), F32),
        compiler_params=_cparams(("parallel",), VMEM_LIMIT),
        name="moe_out",
    )(*([y_rows] * TOP_K), gates, x3, mods, g_post)


def _route(logits, n_experts, mb):
    n_tok = logits.shape[0]
    top_logit, top_idx = lax.top_k(logits, TOP_K)
    gates = jax.nn.softmax(top_logit, axis=-1)
    exp_flat = top_idx.reshape(-1).astype(jnp.int32)
    n_asg = n_tok * TOP_K
    onehot = (exp_flat[:, None] == jnp.arange(n_experts, dtype=jnp.int32)[None, :]).astype(jnp.int32)
    counts = jnp.sum(onehot, axis=0)
    padded = ((counts + mb - 1) // mb) * mb
    pends = jnp.cumsum(padded)
    cap = (-(-n_asg // mb) + n_experts) * mb
    pad_ends = jnp.cumsum(padded - counts)
    pad_exp = jnp.sum(jnp.arange(cap - n_asg, dtype=jnp.int32)[:, None] >= pad_ends[None, :], axis=1)
    keys = jnp.concatenate([exp_flat * 2, pad_exp.astype(jnp.int32) * 2 + 1])
    payload = jnp.concatenate([jnp.arange(n_asg, dtype=jnp.int32), jnp.full((cap - n_asg,), -1, jnp.int32)])
    _, slot_flat = lax.sort((keys, payload), num_keys=1, is_stable=True)
    valid = slot_flat >= 0
    slot_tok = jnp.where(valid, slot_flat // TOP_K, 0)
    slot_dst = jnp.where(valid, (slot_flat % TOP_K) * n_tok + slot_flat // TOP_K,
                         n_asg + jnp.arange(cap, dtype=jnp.int32) % mb)
    n_blocks = cap // mb
    block_starts = jnp.arange(n_blocks, dtype=jnp.int32) * mb
    block_exp = jnp.minimum(jnp.sum(block_starts[:, None] >= pends[None, :], axis=1), n_experts - 1).astype(jnp.int32)
    nused = (pends[-1] // mb).astype(jnp.int32).reshape(1)
    return gates, slot_tok, slot_dst, block_exp, nused, n_asg


def _rope_tables(s):
    half = ROPE_DIMS // 2
    inv_freq = jnp.power(ROPE_THETA, -jnp.arange(half, dtype=F32) / half)
    ang = jnp.arange(s, dtype=F32)[:, None] * inv_freq[None, :]
    cos, sin = jnp.cos(ang), jnp.sin(ang)
    lane = jnp.arange(LANES) % HEAD_DIM
    idx = lane % half
    is_x1 = lane < half
    is_x2 = (lane >= half) & (lane < ROPE_DIMS)
    c = jnp.where((is_x1 | is_x2)[None, :], cos[:, idx], 1.0)
    sa = jnp.where(is_x2[None, :], sin[:, idx], 0.0)
    sb = jnp.where(is_x1[None, :], -sin[:, idx], 0.0)
    return c.astype(F32), sa.astype(F32), sb.astype(F32)


def kernel(x, c, w_ada, b_ada, norm_g, attn_w_in, fox_b_f, attn_w_out, ffn_w_gate, ffn_w_up, ffn_w_down,
           lru_w_in, lru_conv_w, lru_conv_b, lru_w_a, lru_b_a, lru_w_x, lru_b_x, lru_lambda, lru_w_out,
           moe_w_router, moe_b_router, moe_w_gate, moe_w_up, moe_w_down):
    b, s, d = x.shape
    aw = attn_w_out.shape[1]
    n_experts = moe_w_router.shape[2]
    assert s % min(ATT_QUERIES, s) == 0 and s % ATT_KEYS == 0
    assert ATT_QUERIES % ATT_KEYS == 0 and ATT_KEYS % MOBA_BLOCK == 0
    assert ATT_QUERIES % ATT_SUB == 0
    tm = min(512, s)

    c_pad = jnp.zeros((8, d), F32).at[:b].set(c)
    mods = _ada(c_pad, w_ada, b_ada)[:, :b].reshape(w_ada.shape[0], b, 1, 6 * d)

    w_in = attn_w_in[0]
    w_qkv = w_in[:, :3 * aw].astype(BF16)
    w_f = jnp.zeros((d, LANES), F32).at[:, :N_FOX_HEADS].set(w_in[:, 3 * aw:]).astype(BF16)
    b_f = jnp.zeros((1, LANES), F32).at[0, :N_FOX_HEADS].set(fox_b_f[0])
    rope_c, rope_sa, rope_sb = _rope_tables(s)
    q, k, v, logf = _l0_in(x, mods, norm_g[0, 0][None], w_qkv, w_f, b_f, rope_c, rope_sa, rope_sb, tm)

    n_blk = s // MOBA_BLOCK
    qaux_m = _moba_gate(q, k, N_MOBA_HEADS)
    blk_of_pos = jnp.arange(s, dtype=jnp.int32) // MOBA_BLOCK
    kaux_row = jnp.arange(LANES, dtype=jnp.int32)[None, :] == blk_of_pos[:, None]
    kaux_m = jnp.broadcast_to(kaux_row.astype(BF16)[None, None], (1, 2, s, LANES))
    vt = jnp.swapaxes(v, 1, 2)
    o_m = _attention(q, k, vt, qaux_m, kaux_m, N_MOBA_HEADS, 0)

    qaux_f, kaux_f = _fox_prep(logf, N_FOX_HEADS)
    o_f = _attention(q, k, vt, qaux_f, kaux_f, N_FOX_HEADS, N_MOBA_HEADS)

    gains = jnp.concatenate([norm_g[0, 1:4], norm_g[1, 0:1]], axis=0)
    x2, h3 = _l0_tail(o_m, o_f, x, attn_w_out[0].astype(BF16), ffn_w_gate[0].astype(BF16),
                      ffn_w_up[0].astype(BF16), ffn_w_down[0].astype(BF16), mods, gains, min(256, s))

    wax = jnp.concatenate([lru_w_a[0], lru_w_x[0]], axis=-1).astype(BF16)
    w_r = jnp.zeros((d, LANES), F32).at[:, :n_experts].set(moe_w_router[0])
    w_r_hi = w_r.astype(BF16)
    w_r_lo = (w_r - w_r_hi.astype(F32)).astype(BF16)
    w_r = jnp.concatenate([w_r_hi, w_r_lo, w_r_hi], axis=0)
    b_r = jnp.zeros((1, LANES), F32).at[0, :n_experts].set(moe_b_router[0])
    x3, h4, logits = _lru(h3, x2, lru_w_in[0].astype(BF16), lru_conv_w[0], lru_conv_b[0][None], wax,
                          lru_b_a[0][None], lru_b_x[0][None], lru_lambda[0][None], lru_w_out[0].astype(BF16),
                          mods, norm_g[1, 1][None], norm_g[1, 2][None], w_r, b_r, min(256, s))

    n_tok = b * s
    gates, slot_tok, slot_dst, block_exp, nused, n_rows = _route(
        logits.reshape(n_tok, LANES)[:, :n_experts], n_experts, MOE_BLOCK_ROWS)
    y_rows = _moe(block_exp, nused, slot_tok, slot_dst, h4.reshape(n_tok, d),
                  moe_w_gate.reshape(moe_w_gate.shape[1:]), moe_w_up.reshape(moe_w_up.shape[1:]),
                  moe_w_down.reshape(moe_w_down.shape[1:]), n_rows)
    out = _moe_out(y_rows, gates, x3.reshape(n_tok, d), mods, norm_g[1, 3][None], s, min(256, s))
    return out.reshape(b, s, d)
```

```python
import functools

import jax
import jax.numpy as jnp
from jax import lax
from jax.experimental import pallas as pl
from jax.experimental.pallas import tpu as pltpu

F32 = jnp.float32
BF16 = jnp.bfloat16
HIGHEST = lax.Precision.HIGHEST

NORM_EPS = 1e-6
HEAD_DIM = 64
N_MOBA_HEADS = 8
N_FOX_HEADS = 8
ROPE_DIMS = 16
ROPE_THETA = 500000.0
MOBA_BLOCK = 256
MOBA_TOPK = 3
CONV_WIDTH = 4
RG_C = 8.0
TOP_K = 2

LANES = 128
SUBLANES = 8
FOX_PREP_ROWS = 256
ATT_QUERIES = 2048
ATT_KEYS = 512
ATT_SUB = 256
ATT_LOOKAHEAD = 6
NEG = -1e30
LOG2E = 1.4426950408889634
MOE_BLOCK_ROWS = 512
MOE_FF_CHUNK = 512
VMEM_LIMIT = 56 * 1024 * 1024


def _cparams(sem, vmem=None):
    return pltpu.CompilerParams(dimension_semantics=sem, vmem_limit_bytes=vmem)


def _rms(x, g):
    return x * lax.rsqrt(jnp.mean(x * x, axis=-1, keepdims=True) + NORM_EPS) * g


def _const_spec(shape):
    n = len(shape)
    return pl.BlockSpec(shape, lambda *_: (0,) * n)


def _ada_kernel(c_ref, w_ref, b_ref, o_ref):
    c = c_ref[...]
    cond = c * jax.nn.sigmoid(c)
    o_ref[...] = jnp.dot(cond, w_ref[...], preferred_element_type=F32, precision=HIGHEST) + b_ref[...]


def _ada(c_pad, w_ada, b_ada):
    depth, d, d6 = w_ada.shape
    rows = c_pad.shape[0]
    nj = d6 // d
    return pl.pallas_call(
        _ada_kernel,
        grid=(depth, nj),
        in_specs=[
            pl.BlockSpec((rows, d), lambda l, j: (0, 0)),
            pl.BlockSpec((None, d, d), lambda l, j: (l, 0, j)),
            pl.BlockSpec((None, 1, d), lambda l, j: (l, 0, j)),
        ],
        out_specs=pl.BlockSpec((None, rows, d), lambda l, j: (l, 0, j)),
        out_shape=jax.ShapeDtypeStruct((depth, rows, d6), F32),
        compiler_params=_cparams(("parallel", "parallel")),
        name="ada",
    )(c_pad, w_ada, b_ada.reshape(depth, 1, d6))


def _fox_bias_terms(logf, carry_ref, qa_ref, ka_ref, row0, n_heads):
    t = logf.shape[0]
    row = lax.broadcasted_iota(jnp.int32, (t, t), 0)
    col = lax.broadcasted_iota(jnp.int32, (t, t), 1)
    tri = (col <= row).astype(F32)
    cum = jnp.dot(tri, logf, preferred_element_type=F32, precision=HIGHEST) + carry_ref[...]
    carry_ref[...] = cum[t - 1:t, :]
    lane = lax.broadcasted_iota(jnp.int32, (t, LANES), 1)
    for h in range(n_heads):
        c = jnp.broadcast_to(cum[:, h:h + 1], (t, LANES)) * LOG2E
        hi = c.astype(BF16).astype(F32)
        r1 = c - hi
        mid = r1.astype(BF16).astype(F32)
        lo = r1 - mid
        qa = jnp.where(lane == 0, hi, jnp.where(lane == 1, mid, jnp.where(lane == 2, lo,
                       jnp.where(lane < 6, 1.0, 0.0))))
        ka = jnp.where(lane < 3, 1.0, jnp.where(lane == 3, -hi, jnp.where(lane == 4, -mid,
                       jnp.where(lane == 5, -lo, 0.0))))
        qa_ref[h, row0:row0 + t, :] = qa.astype(BF16)
        ka_ref[h, row0:row0 + t, :] = ka.astype(BF16)


def _l0_in_kernel(x_ref, mod_ref, g_ref, w_ref, wf_ref, bf_ref, rc_ref, rsa_ref, rsb_ref,
                  q_ref, k_ref, v_ref, qa_ref, ka_ref, carry_ref, *, d, aw, n_rope_chunks, n_fox_heads):
    @pl.when(pl.program_id(1) == 0)
    def _():
        carry_ref[...] = jnp.zeros_like(carry_ref)

    m = mod_ref[...]
    h = _rms(x_ref[...], g_ref[...]) * (1.0 + m[:, d:2 * d]) + m[:, 0:d]
    hb = h.astype(BF16)
    proj = jnp.dot(hb, w_ref[...], preferred_element_type=F32)
    rc, rsa, rsb = rc_ref[...], rsa_ref[...], rsb_ref[...]

    def rope(t):
        return t * rc + pltpu.roll(t, ROPE_DIMS // 2, 1) * rsa + pltpu.roll(t, LANES - ROPE_DIMS // 2, 1) * rsb

    scale = HEAD_DIM ** -0.5 * LOG2E
    for c in range(aw // LANES):
        sl = slice(c * LANES, (c + 1) * LANES)
        qc = proj[:, sl] * scale
        kc = proj[:, aw + c * LANES:aw + (c + 1) * LANES]
        if c < n_rope_chunks:
            qc, kc = rope(qc), rope(kc)
        q_ref[:, sl] = qc.astype(BF16)
        k_ref[:, sl] = kc.astype(BF16)
    v_ref[...] = proj[:, 2 * aw:3 * aw].astype(BF16)
    fl = jnp.dot(hb, wf_ref[...], preferred_element_type=F32) + bf_ref[...]
    logf = jnp.minimum(fl, 0.0) - jnp.log(1.0 + jnp.exp(-jnp.abs(fl)))
    tm = logf.shape[0]
    sub = min(FOX_PREP_ROWS, tm)
    for r0 in range(0, tm, sub):
        _fox_bias_terms(logf[r0:r0 + sub], carry_ref, qa_ref, ka_ref, r0, n_fox_heads)


def _l0_in(x, mods, g_pre, w_qkv, w_f, b_f, rope_c, rope_sa, rope_sb, tm, n_fox_heads):
    b, s, d = x.shape
    aw = w_qkv.shape[1] // 3
    d6 = mods.shape[-1]
    tok = lambda bi, i: (bi, i, 0)
    aux_spec = pl.BlockSpec((None, n_fox_heads, tm, LANES), lambda bi, i: (bi, 0, i, 0))
    aux_shape = jax.ShapeDtypeStruct((b, n_fox_heads, s, LANES), BF16)
    kern = functools.partial(_l0_in_kernel, d=d, aw=aw, n_fox_heads=n_fox_heads,
                             n_rope_chunks=N_MOBA_HEADS * HEAD_DIM // LANES)
    return pl.pallas_call(
        kern,
        grid=(b, s // tm),
        in_specs=[
            pl.BlockSpec((None, tm, d), tok),
            pl.BlockSpec((None, None, 1, d6), lambda bi, i: (0, bi, 0, 0)),
            _const_spec((1, d)),
            _const_spec(w_qkv.shape),
            _const_spec(w_f.shape),
            _const_spec((1, LANES)),
            pl.BlockSpec((tm, LANES), lambda bi, i: (i, 0)),
            pl.BlockSpec((tm, LANES), lambda bi, i: (i, 0)),
            pl.BlockSpec((tm, LANES), lambda bi, i: (i, 0)),
        ],
        out_specs=[
            pl.BlockSpec((None, tm, aw), tok),
            pl.BlockSpec((None, tm, aw), tok),
            pl.BlockSpec((None, tm, aw), tok),
            aux_spec,
            aux_spec,
        ],
        out_shape=[
            jax.ShapeDtypeStruct((b, s, aw), BF16),
            jax.ShapeDtypeStruct((b, s, aw), BF16),
            jax.ShapeDtypeStruct((b, s, aw), BF16),
            aux_shape,
            aux_shape,
        ],
        scratch_shapes=[pltpu.VMEM((1, LANES), F32)],
        compiler_params=_cparams(("parallel", "arbitrary"), VMEM_LIMIT),
        name="l0_in",
    )(x, mods, g_pre, w_qkv, w_f, b_f, rope_c, rope_sa, rope_sb)


def _moba_gate_kernel(q_ref, k_ref, qa_ref, *, n_blk):
    s = k_ref.shape[0]
    t = MOBA_BLOCK
    rowi = lax.broadcasted_iota(jnp.int32, (n_blk, s), 0)
    cs = lax.broadcasted_iota(jnp.int32, (n_blk, s), 1)
    own = (cs >= rowi * t) & (cs < (rowi + 1) * t)
    km = jnp.dot(jnp.where(own, 1.0 / t, 0.0).astype(BF16), k_ref[...], preferred_element_type=F32)
    lane_k = lax.broadcasted_iota(jnp.int32, (n_blk, LANES), 1)
    km2 = jnp.concatenate([jnp.where(lane_k < HEAD_DIM, km, 0.0), jnp.where(lane_k >= HEAD_DIM, km, 0.0)], axis=0)
    hi = km2.astype(BF16)
    r1 = km2 - hi.astype(F32)
    mid = r1.astype(BF16)
    lo = (r1 - mid.astype(F32)).astype(BF16)
    g3 = lax.dot_general(jnp.concatenate([hi, mid, lo], axis=0), q_ref[...], (((1,), (1,)), ((), ())),
                         preferred_element_type=F32)
    g = g3[0:2 * n_blk] + g3[2 * n_blk:4 * n_blk] + g3[4 * n_blk:6 * n_blk]
    fully_past = (rowi + 1) * t <= cs
    for hh in range(2):
        gv = jnp.where(fully_past, g[hh * n_blk:(hh + 1) * n_blk], -jnp.inf)
        keep = own
        for _ in range(MOBA_TOPK):
            mx = jnp.max(gv, axis=0, keepdims=True)
            cand = jnp.where((gv == mx) & (mx > -jnp.inf), rowi, n_blk)
            pick = rowi == jnp.min(cand, axis=0, keepdims=True)
            keep = keep | pick
            gv = jnp.where(pick, -jnp.inf, gv)
        bias_t = jnp.concatenate([jnp.where(keep, 0.0, NEG), jnp.zeros((LANES - n_blk, s), F32)], axis=0)
        qa_ref[hh] = jnp.transpose(bias_t).astype(BF16)


def _moba_gate(q, k, n_heads):
    b, s, _ = q.shape
    n_blk = s // MOBA_BLOCK
    assert n_blk <= LANES
    return pl.pallas_call(
        functools.partial(_moba_gate_kernel, n_blk=n_blk),
        grid=(b, n_heads // 2),
        in_specs=[
            pl.BlockSpec((None, s, LANES), lambda bi, p: (bi, 0, p)),
            pl.BlockSpec((None, s, LANES), lambda bi, p: (bi, 0, p)),
        ],
        out_specs=pl.BlockSpec((None, 2, s, LANES), lambda bi, p: (bi, p, 0, 0)),
        out_shape=jax.ShapeDtypeStruct((b, n_heads, s, LANES), BF16),
        compiler_params=_cparams(("parallel", "parallel")),
        name="moba_gate",
    )(q, k)


def _attn_kernel(q_ref, k_ref, vt_ref, qa_ref, ka_ref, o_ref):
    tq = q_ref.shape[0]
    t = ATT_KEYS
    n_sub = tq // ATT_SUB
    n_diag = tq // t
    g = pl.program_id(2)
    lane = lax.broadcasted_iota(jnp.int32, (ATT_SUB, LANES), 1)
    row_v = lax.broadcasted_iota(jnp.int32, (LANES, t), 0)
    row_o = lax.broadcasted_iota(jnp.int32, (LANES, ATT_SUB), 0)
    key = lax.broadcasted_iota(jnp.int32, (t, ATT_SUB), 0)
    qry = lax.broadcasted_iota(jnp.int32, (t, ATT_SUB), 1)
    nt = (((1,), (1,)), ((), ()))
    qas = []
    for hh in range(2):
        in_head = (lane < HEAD_DIM) if hh == 0 else (lane >= HEAD_DIM)
        for r in range(n_sub):
            rs = slice(r * ATT_SUB, (r + 1) * ATT_SUB)
            q = q_ref[rs, :]
            qas.append(jnp.concatenate([jnp.where(in_head, q, jnp.zeros_like(q)), qa_ref[hh, rs, :]], axis=1))

    def q_off(c):
        return (c % n_sub) * ATT_SUB

    def update(state, steps):
        loaded = {}

        def operands(si):
            if si not in loaded:
                j = steps[si][0]
                rows = pl.ds(pl.multiple_of(j * t, t), t)
                k = k_ref[rows, :]
                vt = vt_ref[:, rows]
                kks = [jnp.concatenate([k, ka_ref[hh, rows, :]], axis=1) for hh in range(2)]
                vhs = [jnp.where((row_v < HEAD_DIM) if hh == 0 else (row_v >= HEAD_DIM), vt, jnp.ones_like(vt))
                       for hh in range(2)]
                loaded[si] = (kks, vhs)
            return loaded[si]

        items = [(si, c) for si, (_, d) in enumerate(steps) for c in range(2 * n_sub)
                 if d is None or q_off(c) + ATT_SUB > d * t]

        def qk(item):
            si, c = item
            return lax.dot_general(operands(si)[0][c // n_sub], qas[c], nt, preferred_element_type=F32)

        scs = {n: qk(items[n]) for n in range(min(ATT_LOOKAHEAD, len(items)))}
        new = list(state)
        for n, (si, c) in enumerate(items):
            if n + ATT_LOOKAHEAD < len(items):
                scs[n + ATT_LOOKAHEAD] = qk(items[n + ATT_LOOKAHEAD])
            d = steps[si][1]
            m_old, acc = new[c]
            sc = scs.pop(n)
            if d is not None and q_off(c) < (d + 1) * t:
                sc = jnp.where(key + d * t <= qry + q_off(c), sc, NEG)
            m_new = jnp.maximum(m_old, jnp.max(sc, axis=0, keepdims=True))
            p = jnp.exp2(sc - m_new).astype(BF16)
            acc = jnp.exp2(m_old - m_new) * acc + jnp.dot(operands(si)[1][c // n_sub], p,
                                                          preferred_element_type=F32)
            new[c] = (m_new, acc)
        return tuple(new)

    init = tuple((jnp.full((1, ATT_SUB), NEG, F32), jnp.zeros((LANES, ATT_SUB), F32)) for _ in range(2 * n_sub))
    per_iter = 2 if n_diag % 2 == 0 else 1
    state = lax.fori_loop(0, g * (n_diag // per_iter),
                          lambda j, st: update(st, [(j * per_iter + u, None) for u in range(per_iter)]), init)
    state = update(state, [(g * n_diag + d, d) for d in range(n_diag)])
    for r in range(n_sub):
        outs = [state[hh * n_sub + r][1] for hh in range(2)]
        outs = [acc / pltpu.roll(acc, HEAD_DIM, 0) for acc in outs]
        o_t = jnp.where(row_o < HEAD_DIM, outs[0], outs[1])
        o_ref[r * ATT_SUB:(r + 1) * ATT_SUB, :] = jnp.transpose(o_t).astype(o_ref.dtype)


def _attention(q, k, vt, qaux, kaux, n_heads, head_off):
    b, s, _ = q.shape
    t = min(ATT_QUERIES, s)
    po = head_off // 2
    kb, kh = kaux.shape[0], kaux.shape[1]
    ka_map = (lambda bi, p, i: (bi, p, 0, 0)) if kb == b and kh == n_heads else (lambda bi, p, i: (0, 0, 0, 0))
    return pl.pallas_call(
        _attn_kernel,
        grid=(b, n_heads // 2, s // t),
        in_specs=[
            pl.BlockSpec((None, t, LANES), lambda bi, p, i: (bi, i, p + po)),
            pl.BlockSpec((None, s, LANES), lambda bi, p, i: (bi, 0, p + po)),
            pl.BlockSpec((None, LANES, s), lambda bi, p, i: (bi, p + po, 0)),
            pl.BlockSpec((None, 2, t, LANES), lambda bi, p, i: (bi, p, i, 0)),
            pl.BlockSpec((None, 2, s, LANES), ka_map),
        ],
        out_specs=pl.BlockSpec((None, t, LANES), lambda bi, p, i: (bi, i, p)),
        out_shape=jax.ShapeDtypeStruct((b, s, n_heads * HEAD_DIM), BF16),
        compiler_params=_cparams(("parallel", "parallel", "arbitrary")),
        name="attn",
    )(q, k, vt, qaux, kaux)


def _l0_tail_kernel(om_ref, of_ref, x_ref, wo_ref, wg_ref, wu_ref, wd_ref, mod0_ref, mod1_ref, g_ref,
                    x2_ref, h3_ref, *, d):
    m0 = mod0_ref[...]
    m1 = mod1_ref[...]
    hw = om_ref.shape[1]
    y = (jnp.dot(om_ref[...], wo_ref[0:hw, :], preferred_element_type=F32)
         + jnp.dot(of_ref[...], wo_ref[hw:, :], preferred_element_type=F32))
    x1 = x_ref[...] + m0[:, 2 * d:3 * d] * _rms(y, g_ref[0:1, :])
    h = (_rms(x1, g_ref[1:2, :]) * (1.0 + m0[:, 4 * d:5 * d]) + m0[:, 3 * d:4 * d]).astype(BF16)
    a = jnp.dot(h, wg_ref[...], preferred_element_type=F32)
    u = jnp.dot(h, wu_ref[...], preferred_element_type=F32)
    act = (a * jax.nn.sigmoid(a) * u).astype(BF16)
    y = jnp.dot(act, wd_ref[...], preferred_element_type=F32)
    x2 = x1 + m0[:, 5 * d:6 * d] * _rms(y, g_ref[2:3, :])
    x2_ref[...] = x2
    h3_ref[...] = (_rms(x2, g_ref[3:4, :]) * (1.0 + m1[:, d:2 * d]) + m1[:, 0:d]).astype(BF16)


def _l0_tail(o_m, o_f, x, w_out, w_gate, w_up, w_down, mods, gains, tm):
    b, s, d = x.shape
    d6 = mods.shape[-1]
    tok = lambda bi, i: (bi, i, 0)
    single = dict(pipeline_mode=pl.Buffered(1))
    return pl.pallas_call(
        functools.partial(_l0_tail_kernel, d=d),
        grid=(b, s // tm),
        in_specs=[
            pl.BlockSpec((None, tm, o_m.shape[2]), tok),
            pl.BlockSpec((None, tm, o_f.shape[2]), tok),
            pl.BlockSpec((None, tm, d), tok),
            pl.BlockSpec(w_out.shape, lambda bi, i: (0, 0), **single),
            pl.BlockSpec(w_gate.shape, lambda bi, i: (0, 0), **single),
            pl.BlockSpec(w_up.shape, lambda bi, i: (0, 0), **single),
            pl.BlockSpec(w_down.shape, lambda bi, i: (0, 0), **single),
            pl.BlockSpec((None, None, 1, d6), lambda bi, i: (0, bi, 0, 0)),
            pl.BlockSpec((None, None, 1, d6), lambda bi, i: (1, bi, 0, 0)),
            _const_spec(gains.shape),
        ],
        out_specs=[pl.BlockSpec((None, tm, d), tok), pl.BlockSpec((None, tm, d), tok)],
        out_shape=[jax.ShapeDtypeStruct((b, s, d), F32), jax.ShapeDtypeStruct((b, s, d), BF16)],
        compiler_params=_cparams(("parallel", "parallel"), VMEM_LIMIT),
        name="l0_tail",
    )(o_m, o_f, x, w_out, w_gate, w_up, w_down, mods, mods, gains)


def _lru_kernel(h_ref, x_ref, win_ref, cw_ref, cb_ref, wax_ref, ba_ref, bx_ref, lam_ref, wout_ref,
                mod_ref, g1_ref, g2_ref, wr_ref, br_ref,
                x3_ref, h4_ref, lg_ref, utail_ref, hc_ref, *, d, dr, n_rnn_blocks):
    tm = h_ref.shape[0]
    sub = utail_ref.shape[0]

    @pl.when(pl.program_id(1) == 0)
    def _():
        utail_ref[...] = jnp.zeros_like(utail_ref)
        hc_ref[...] = jnp.zeros_like(hc_ref)

    proj = jnp.dot(h_ref[...], win_ref[...], preferred_element_type=F32)
    gate_branch = proj[:, :dr]
    u = proj[:, dr:]
    row_sub = lax.broadcasted_iota(jnp.int32, (sub, dr), 0)
    n_groups = tm // sub
    taps = [cw_ref[j:j + 1, :] for j in range(CONV_WIDTH)]
    bias = cb_ref[...]
    prev = utail_ref[...]
    rolled_prev = [None] + [pltpu.roll(prev, back, 0) for back in range(1, CONV_WIDTH)]
    conv_groups = []
    for g in range(n_groups):
        ug = u[g * sub:(g + 1) * sub]
        cg = bias + ug * taps[CONV_WIDTH - 1]
        for back in range(1, CONV_WIDTH):
            rolled = pltpu.roll(ug, back, 0)
            cg = cg + jnp.where(row_sub >= back, rolled, rolled_prev[back]) * taps[CONV_WIDTH - 1 - back]
            rolled_prev[back] = rolled
        conv_groups.append(cg)
    conv = jnp.concatenate(conv_groups, axis=0)
    utail_ref[...] = u[tm - sub:tm]

    w = dr // n_rnn_blocks
    rs, is_ = [], []
    for n in range(n_rnn_blocks):
        cbk = conv[:, n * w:(n + 1) * w].astype(BF16)
        ra = jnp.dot(cbk, wax_ref[n], preferred_element_type=F32)
        rs.append(ra[:, :w])
        is_.append(ra[:, w:])
    r = jax.nn.sigmoid(jnp.concatenate(rs, axis=1) + ba_ref[...])
    ig = jax.nn.sigmoid(jnp.concatenate(is_, axis=1) + bx_ref[...])
    nl = -lam_ref[...]
    softplus = jnp.maximum(nl, 0.0) + jnp.log(1.0 + jnp.exp(-jnp.abs(nl)))
    log_a = (-RG_C * r) * softplus
    a = jnp.exp(log_a)
    xin = jnp.exp2(0.5 * jnp.log2(1.0 - a * a)) * (ig * conv)

    carry = hc_ref[...]
    groups = []
    for g in range(n_groups):
        sa, sx = a[g * sub:(g + 1) * sub], xin[g * sub:(g + 1) * sub]
        dist = 1
        while dist < sub:
            keep = row_sub >= dist
            xs = jnp.where(keep, pltpu.roll(sx, dist, 0), 0.0)
            as_ = jnp.where(keep, pltpu.roll(sa, dist, 0), 1.0)
            sx = sx + sa * xs
            sa = sa * as_
            dist *= 2
        hg = sx + sa * carry
        carry = hg[sub - 1:sub]
        groups.append(hg)
    hs = jnp.concatenate(groups, axis=0)
    hc_ref[...] = carry

    y = (jax.nn.gelu(gate_branch, approximate=True) * hs).astype(BF16)
    out = jnp.dot(y, wout_ref[...], preferred_element_type=F32)
    m = mod_ref[...]
    x3 = x_ref[...] + m[:, 2 * d:3 * d] * _rms(out, g1_ref[...])
    x3_ref[...] = x3
    h4 = _rms(x3, g2_ref[...]) * (1.0 + m[:, 4 * d:5 * d]) + m[:, 3 * d:4 * d]
    h4_ref[...] = h4
    h_hi = h4.astype(BF16)
    h_lo = (h4 - h_hi.astype(F32)).astype(BF16)
    lg_ref[...] = jnp.dot(jnp.concatenate([h_hi, h_hi, h_lo], axis=1), wr_ref[...],
                          preferred_element_type=F32) + br_ref[...]


def _lru(h3, x2, w_in, conv_w, conv_b, wax, b_a, b_x, lam, w_out, mods, g_post, g_pre2, w_r, b_r, tm):
    b, s, d = x2.shape
    dr = w_out.shape[0]
    d6 = mods.shape[-1]
    nb = wax.shape[0]
    tok = lambda bi, i: (bi, i, 0)
    return pl.pallas_call(
        functools.partial(_lru_kernel, d=d, dr=dr, n_rnn_blocks=nb),
        grid=(b, s // tm),
        in_specs=[
            pl.BlockSpec((None, tm, d), tok),
            pl.BlockSpec((None, tm, d), tok),
            _const_spec(w_in.shape),
            _const_spec(conv_w.shape),
            _const_spec((1, dr)),
            _const_spec(wax.shape),
            _const_spec((1, dr)),
            _const_spec((1, dr)),
            _const_spec((1, dr)),
            _const_spec(w_out.shape),
            pl.BlockSpec((None, None, 1, d6), lambda bi, i: (1, bi, 0, 0)),
            _const_spec((1, d)),
            _const_spec((1, d)),
            _const_spec(w_r.shape),
            _const_spec((1, LANES)),
        ],
        out_specs=[pl.BlockSpec((None, tm, d), tok), pl.BlockSpec((None, tm, d), tok),
                   pl.BlockSpec((None, tm, LANES), tok)],
        out_shape=[jax.ShapeDtypeStruct((b, s, d), F32), jax.ShapeDtypeStruct((b, s, d), F32),
                   jax.ShapeDtypeStruct((b, s, LANES), F32)],
        scratch_shapes=[pltpu.VMEM((SUBLANES, dr), F32), pltpu.VMEM((1, dr), F32)],
        compiler_params=_cparams(("parallel", "arbitrary"), VMEM_LIMIT),
        name="lru",
    )(h3, x2, w_in, conv_w, conv_b, wax, b_a, b_x, lam, w_out, mods, g_post, g_pre2, w_r, b_r)


def _moe_kernel(bexp_ref, nused_ref, tok_ref, dst_ref, h_hbm, wg_hbm, wu_hbm, wd_hbm, o_hbm,
                xbuf, ybuf, wg_bf, wu_bf, wd_bf, stage_g, stage_u, stage_d, gsem, ssem, wsem,
                *, n_blocks, dummy_base):
    mb = xbuf.shape[1]
    i = pl.program_id(0)
    nused = nused_ref[0]
    slot = lax.rem(i, 2)
    other = 1 - slot

    def gather_copy(blk, buf, r):
        tok = tok_ref[blk * mb + r]
        return pltpu.make_async_copy(h_hbm.at[pl.ds(tok, 1), :], xbuf.at[buf, pl.ds(r, 1), :], gsem.at[buf])

    def scatter_copy(dst, buf, r):
        return pltpu.make_async_copy(ybuf.at[buf, pl.ds(r, 1), :], o_hbm.at[pl.ds(dst, 1), :], ssem.at[buf])

    def wait_gather(buf):
        pltpu.make_async_copy(h_hbm.at[pl.ds(0, mb), :], xbuf.at[buf], gsem.at[buf]).wait()

    def wait_scatter(buf):
        pltpu.make_async_copy(ybuf.at[buf], o_hbm.at[pl.ds(0, mb), :], ssem.at[buf]).wait()

    @pl.when(i == 0)
    def _():
        def body(r, c):
            gather_copy(0, 0, r).start()
            return c
        lax.fori_loop(0, mb, body, 0, unroll=8)
        ybuf[1] = jnp.zeros(ybuf.shape[1:], F32)

    @pl.when((i >= 1) & (i <= nused))
    def _():
        wait_scatter(slot)

    ck = MOE_FF_CHUNK
    n_chunks = wg_bf.shape[1] // ck
    e_cur = bexp_ref[jnp.minimum(i, n_blocks - 1)]
    e_prev = bexp_ref[jnp.maximum(i - 1, 0)]
    new_expert = (i == 0) | (e_cur != e_prev)

    def weight_copies(c, buf):
        cs = pl.ds(c * ck, ck)
        return (pltpu.make_async_copy(wg_hbm.at[e_cur, :, cs], stage_g.at[buf], wsem.at[buf]),
                pltpu.make_async_copy(wu_hbm.at[e_cur, :, cs], stage_u.at[buf], wsem.at[buf]),
                pltpu.make_async_copy(wd_hbm.at[e_cur, cs, :], stage_d.at[buf], wsem.at[buf]))

    def compute(load_weights):
        wait_gather(slot)
        x = xbuf[slot].astype(BF16)
        acc = jnp.zeros((mb, wd_bf.shape[1]), F32)
        per = -(-mb // n_chunks)
        nxt = jnp.minimum(i + 1, n_blocks - 1)
        prv = jnp.maximum(i - 1, 0)
        if load_weights:
            for c in range(min(2, n_chunks)):
                for cp in weight_copies(c, c % 2):
                    cp.start()
        for c in range(n_chunks):
            for r in range(c * per, min((c + 1) * per, mb)):
                gather_copy(nxt, other, r).start()
                dst = jnp.where(i == 0, dummy_base + r, dst_ref[prv * mb + r])
                scatter_copy(dst, other, r).start(priority=1)
            cs = slice(c * ck, (c + 1) * ck)
            if load_weights:
                for cp in weight_copies(c, c % 2):
                    cp.wait()
                wg_bf[:, cs] = stage_g[c % 2].astype(BF16)
                wu_bf[:, cs] = stage_u[c % 2].astype(BF16)
                wd_bf[cs, :] = stage_d[c % 2].astype(BF16)
                if c + 2 < n_chunks:
                    for cp in weight_copies(c + 2, c % 2):
                        cp.start()
            a = jnp.dot(x, wg_bf[:, cs], preferred_element_type=F32)
            u = jnp.dot(x, wu_bf[:, cs], preferred_element_type=F32)
            act = (a * jax.nn.sigmoid(a) * u).astype(BF16)
            acc = acc + jnp.dot(act, wd_bf[cs, :], preferred_element_type=F32)
        ybuf[slot] = acc

    @pl.when((i < nused) & new_expert)
    def _():
        compute(True)

    @pl.when((i < nused) & jnp.logical_not(new_expert))
    def _():
        compute(False)

    @pl.when(i == nused)
    def _():
        wait_gather(slot)

        def body(r, c):
            scatter_copy(dst_ref[(i - 1) * mb + r], other, r).start()
            return c
        lax.fori_loop(0, mb, body, 0, unroll=8)
        wait_scatter(other)


def _moe(block_exp, nused, slot_tok, slot_dst, h4, w_gate, w_up, w_down, n_out_rows):
    n_tok, d = h4.shape
    mb = MOE_BLOCK_ROWS
    cap = slot_tok.shape[0]
    n_blocks = cap // mb
    e, _, dff = w_gate.shape
    ck = MOE_FF_CHUNK
    assert dff % ck == 0
    grid_spec = pltpu.PrefetchScalarGridSpec(
        num_scalar_prefetch=4,
        grid=(n_blocks + 1,),
        in_specs=[pl.BlockSpec(memory_space=pl.ANY)] * 4,
        out_specs=pl.BlockSpec(memory_space=pl.ANY),
        scratch_shapes=[pltpu.VMEM((2, mb, d), F32), pltpu.VMEM((2, mb, d), F32),
                        pltpu.VMEM((d, dff), BF16), pltpu.VMEM((d, dff), BF16), pltpu.VMEM((dff, d), BF16),
                        pltpu.VMEM((2, d, ck), F32), pltpu.VMEM((2, d, ck), F32), pltpu.VMEM((2, ck, d), F32),
                        pltpu.SemaphoreType.DMA((2,)), pltpu.SemaphoreType.DMA((2,)),
                        pltpu.SemaphoreType.DMA((2,))],
    )
    return pl.pallas_call(
        functools.partial(_moe_kernel, n_blocks=n_blocks, dummy_base=n_out_rows),
        grid_spec=grid_spec,
        out_shape=jax.ShapeDtypeStruct((n_out_rows + mb, d), F32),
        compiler_params=_cparams(("arbitrary",), VMEM_LIMIT),
        name="moe",
    )(block_exp, nused, slot_tok, slot_dst, h4, w_gate, w_up, w_down)


def _moe_out_kernel(*refs, d):
    y_refs, (gate_ref, x_ref, mod_ref, g_ref, o_ref) = refs[:TOP_K], refs[TOP_K:]
    g = gate_ref[...]
    y = y_refs[0][...] * g[:, 0:1]
    for kk in range(1, TOP_K):
        y = y + y_refs[kk][...] * g[:, kk:kk + 1]
    m = mod_ref[...]
    o_ref[...] = x_ref[...] + m[:, 5 * d:6 * d] * _rms(y, g_ref[...])


def _moe_out(y_rows, gates, x3, mods, g_post, s, tm):
    n_tok, d = x3.shape
    d6 = mods.shape[-1]
    per_b = s // tm
    nt = n_tok // tm
    y_specs = [pl.BlockSpec((tm, d), functools.partial(lambda i, kk: (kk * nt + i, 0), kk=kk))
               for kk in range(TOP_K)]
    return pl.pallas_call(
        functools.partial(_moe_out_kernel, d=d),
        grid=(nt,),
        in_specs=y_specs + [
            pl.BlockSpec((tm, TOP_K), lambda i: (i, 0)),
            pl.BlockSpec((tm, d), lambda i: (i, 0)),
            pl.BlockSpec((None, None, 1, d6), lambda i: (1, i // per_b, 0, 0)),
            pl.BlockSpec((1, d), lambda i: (0, 0)),
        ],
        out_specs=pl.BlockSpec((tm, d), lambda i: (i, 0)),
        out_shape=jax.ShapeDtypeStruct((n_tok, d), F32),
        compiler_params=_cparams(("parallel",), VMEM_LIMIT),
        name="moe_out",
    )(*([y_rows] * TOP_K), gates, x3, mods, g_post)


def _route(logits, n_experts, mb):
    n_tok = logits.shape[0]
    top_logit, top_idx = lax.top_k(logits, TOP_K)
    gates = jax.nn.softmax(top_logit, axis=-1)
    exp_flat = top_idx.reshape(-1).astype(jnp.int32)
    n_asg = n_tok * TOP_K
    onehot = (exp_flat[:, None] == jnp.arange(n_experts, dtype=jnp.int32)[None, :]).astype(jnp.int32)
    counts = jnp.sum(onehot, axis=0)
    padded = ((counts + mb - 1) // mb) * mb
    pends = jnp.cumsum(padded)
    cap = (-(-n_asg // mb) + n_experts) * mb
    pad_ends = jnp.cumsum(padded - counts)
    pad_exp = jnp.sum(jnp.arange(cap - n_asg, dtype=jnp.int32)[:, None] >= pad_ends[None, :], axis=1)
    keys = jnp.concatenate([exp_flat * 2, pad_exp.astype(jnp.int32) * 2 + 1])
    payload = jnp.concatenate([jnp.arange(n_asg, dtype=jnp.int32), jnp.full((cap - n_asg,), -1, jnp.int32)])
    _, slot_flat = lax.sort((keys, payload), num_keys=1, is_stable=True)
    valid = slot_flat >= 0
    slot_tok = jnp.where(valid, slot_flat // TOP_K, 0)
    slot_dst = jnp.where(valid, (slot_flat % TOP_K) * n_tok + slot_flat // TOP_K,
                         n_asg + jnp.arange(cap, dtype=jnp.int32) % mb)
    n_blocks = cap // mb
    block_starts = jnp.arange(n_blocks, dtype=jnp.int32) * mb
    block_exp = jnp.minimum(jnp.sum(block_starts[:, None] >= pends[None, :], axis=1), n_experts - 1).astype(jnp.int32)
    nused = (pends[-1] // mb).astype(jnp.int32).reshape(1)
    return gates, slot_tok, slot_dst, block_exp, nused, n_asg


def _rope_tables(s):
    half = ROPE_DIMS // 2
    inv_freq = jnp.power(ROPE_THETA, -jnp.arange(half, dtype=F32) / half)
    ang = jnp.arange(s, dtype=F32)[:, None] * inv_freq[None, :]
    cos, sin = jnp.cos(ang), jnp.sin(ang)
    lane = jnp.arange(LANES) % HEAD_DIM
    idx = lane % half
    is_x1 = lane < half
    is_x2 = (lane >= half) & (lane < ROPE_DIMS)
    c = jnp.where((is_x1 | is_x2)[None, :], cos[:, idx], 1.0)
    sa = jnp.where(is_x2[None, :], sin[:, idx], 0.0)
    sb = jnp.where(is_x1[None, :], -sin[:, idx], 0.0)
    return c.astype(F32), sa.astype(F32), sb.astype(F32)


def kernel(x, c, w_ada, b_ada, norm_g, attn_w_in, fox_b_f, attn_w_out, ffn_w_gate, ffn_w_up, ffn_w_down,
           lru_w_in, lru_conv_w, lru_conv_b, lru_w_a, lru_b_a, lru_w_x, lru_b_x, lru_lambda, lru_w_out,
           moe_w_router, moe_b_router, moe_w_gate, moe_w_up, moe_w_down):
    b, s, d = x.shape
    aw = attn_w_out.shape[1]
    n_experts = moe_w_router.shape[2]
    assert s % min(ATT_QUERIES, s) == 0 and s % ATT_KEYS == 0
    assert ATT_QUERIES % ATT_KEYS == 0 and ATT_KEYS % MOBA_BLOCK == 0
    assert ATT_QUERIES % ATT_SUB == 0
    tm = min(512, s)

    c_pad = jnp.zeros((8, d), F32).at[:b].set(c)
    mods = _ada(c_pad, w_ada, b_ada)[:, :b].reshape(w_ada.shape[0], b, 1, 6 * d)

    w_in = attn_w_in[0]
    w_qkv = w_in[:, :3 * aw].astype(BF16)
    w_f = jnp.zeros((d, LANES), F32).at[:, :N_FOX_HEADS].set(w_in[:, 3 * aw:]).astype(BF16)
    b_f = jnp.zeros((1, LANES), F32).at[0, :N_FOX_HEADS].set(fox_b_f[0])
    rope_c, rope_sa, rope_sb = _rope_tables(s)
    q, k, v, qaux_f, kaux_f = _l0_in(x, mods, norm_g[0, 0][None], w_qkv, w_f, b_f, rope_c, rope_sa, rope_sb,
                                     tm, N_FOX_HEADS)

    n_blk = s // MOBA_BLOCK
    qaux_m = _moba_gate(q, k, N_MOBA_HEADS)
    blk_of_pos = jnp.arange(s, dtype=jnp.int32) // MOBA_BLOCK
    kaux_row = jnp.arange(LANES, dtype=jnp.int32)[None, :] == blk_of_pos[:, None]
    kaux_m = jnp.broadcast_to(kaux_row.astype(BF16)[None, None], (1, 2, s, LANES))
    vt = jnp.swapaxes(v, 1, 2)
    o_m = _attention(q, k, vt, qaux_m, kaux_m, N_MOBA_HEADS, 0)

    o_f = _attention(q, k, vt, qaux_f, kaux_f, N_FOX_HEADS, N_MOBA_HEADS)

    gains = jnp.concatenate([norm_g[0, 1:4], norm_g[1, 0:1]], axis=0)
    x2, h3 = _l0_tail(o_m, o_f, x, attn_w_out[0].astype(BF16), ffn_w_gate[0].astype(BF16),
                      ffn_w_up[0].astype(BF16), ffn_w_down[0].astype(BF16), mods, gains, min(256, s))

    wax = jnp.concatenate([lru_w_a[0], lru_w_x[0]], axis=-1).astype(BF16)
    w_r = jnp.zeros((d, LANES), F32).at[:, :n_experts].set(moe_w_router[0])
    w_r_hi = w_r.astype(BF16)
    w_r_lo = (w_r - w_r_hi.astype(F32)).astype(BF16)
    w_r = jnp.concatenate([w_r_hi, w_r_lo, w_r_hi], axis=0)
    b_r = jnp.zeros((1, LANES), F32).at[0, :n_experts].set(moe_b_router[0])
    x3, h4, logits = _lru(h3, x2, lru_w_in[0].astype(BF16), lru_conv_w[0], lru_conv_b[0][None], wax,
                          lru_b_a[0][None], lru_b_x[0][None], lru_lambda[0][None], lru_w_out[0].astype(BF16),
                          mods, norm_g[1, 1][None], norm_g[1, 2][None], w_r, b_r, min(256, s))

    n_tok = b * s
    gates, slot_tok, slot_dst, block_exp, nused, n_rows = _route(
        logits.reshape(n_tok, LANES)[:, :n_experts], n_experts, MOE_BLOCK_ROWS)
    y_rows = _moe(block_exp, nused, slot_tok, slot_dst, h4.reshape(n_tok, d),
                  moe_w_gate.reshape(moe_w_gate.shape[1:]), moe_w_up.reshape(moe_w_up.shape[1:]),
                  moe_w_down.reshape(moe_w_down.shape[1:]), n_rows)
    out = _moe_out(y_rows, gates, x3.reshape(n_tok, d), mods, norm_g[1, 3][None], s, min(256, s))
    return out.reshape(b, s, d)
```

```python
import functools

import jax
import jax.numpy as jnp
from jax import lax
from jax.experimental import pallas as pl
from jax.experimental.pallas import tpu as pltpu

F32 = jnp.float32
BF16 = jnp.bfloat16
HIGHEST = lax.Precision.HIGHEST

NORM_EPS = 1e-6
HEAD_DIM = 64
N_MOBA_HEADS = 8
N_FOX_HEADS = 8
ROPE_DIMS = 16
ROPE_THETA = 500000.0
MOBA_BLOCK = 256
MOBA_TOPK = 3
CONV_WIDTH = 4
RG_C = 8.0
TOP_K = 2

LANES = 128
SUBLANES = 8
FOX_PREP_ROWS = 256
ATT_QUERIES = 2048
ATT_KEYS = 512
ATT_SUB = 256
ATT_LOOKAHEAD = 6
NEG = -1e30
LOG2E = 1.4426950408889634
MOE_BLOCK_ROWS = 512
MOE_FF_CHUNK = 512
VMEM_LIMIT = 56 * 1024 * 1024


def _cparams(sem, vmem=None):
    return pltpu.CompilerParams(dimension_semantics=sem, vmem_limit_bytes=vmem)


def _rms(x, g):
    return x * lax.rsqrt(jnp.mean(x * x, axis=-1, keepdims=True) + NORM_EPS) * g


def _const_spec(shape):
    n = len(shape)
    return pl.BlockSpec(shape, lambda *_: (0,) * n)


def _ada_kernel(c_ref, w_ref, b_ref, o_ref):
    c = c_ref[...]
    cond = c * jax.nn.sigmoid(c)
    o_ref[...] = jnp.dot(cond, w_ref[...], preferred_element_type=F32, precision=HIGHEST) + b_ref[...]


def _ada(c_pad, w_ada, b_ada):
    depth, d, d6 = w_ada.shape
    rows = c_pad.shape[0]
    nj = d6 // d
    return pl.pallas_call(
        _ada_kernel,
        grid=(depth, nj),
        in_specs=[
            pl.BlockSpec((rows, d), lambda l, j: (0, 0)),
            pl.BlockSpec((None, d, d), lambda l, j: (l, 0, j)),
            pl.BlockSpec((None, 1, d), lambda l, j: (l, 0, j)),
        ],
        out_specs=pl.BlockSpec((None, rows, d), lambda l, j: (l, 0, j)),
        out_shape=jax.ShapeDtypeStruct((depth, rows, d6), F32),
        compiler_params=_cparams(("parallel", "parallel")),
        name="ada",
    )(c_pad, w_ada, b_ada.reshape(depth, 1, d6))


def _fox_bias_terms(logf, carry_ref, qx_ref, kx_ref, row0, n_heads):
    t = logf.shape[0]
    row = lax.broadcasted_iota(jnp.int32, (t, t), 0)
    col = lax.broadcasted_iota(jnp.int32, (t, t), 1)
    tri = (col <= row).astype(F32)
    cum = jnp.dot(tri, logf, preferred_element_type=F32, precision=HIGHEST) + carry_ref[...]
    carry_ref[...] = cum[t - 1:t, :]
    c = cum * LOG2E
    hi = c.astype(BF16).astype(F32)
    r1 = c - hi
    mid = r1.astype(BF16).astype(F32)
    lo = r1 - mid
    lane = lax.broadcasted_iota(jnp.int32, (t, LANES), 1)
    n_terms = 3
    in_terms = (lane % HEAD_DIM) < n_terms
    in_ones = ((lane % HEAD_DIM) >= n_terms) & ((lane % HEAD_DIM) < 2 * n_terms)
    for p in range(n_heads // 2):
        vals = jnp.zeros((t, LANES), F32)
        for h, base in ((2 * p, HEAD_DIM), (2 * p + 1, 0)):
            for n, term in enumerate((hi, mid, lo)):
                shift = (base + n - h) % LANES
                moved = pltpu.roll(term, shift, 1) if shift else term
                vals = jnp.where(lane == base + n, moved, vals)
        qx = jnp.where(in_ones, 1.0, vals)
        kx = jnp.where(in_terms, 1.0, jnp.where(in_ones, -pltpu.roll(vals, n_terms, 1), 0.0))
        qx_ref[row0:row0 + t, p * LANES:(p + 1) * LANES] = qx.astype(BF16)
        kx_ref[row0:row0 + t, p * LANES:(p + 1) * LANES] = kx.astype(BF16)


def _l0_in_kernel(x_ref, mod_ref, g_ref, w_ref, wf_ref, bf_ref, rc_ref, rsa_ref, rsb_ref,
                  q_ref, k_ref, v_ref, qx_ref, kx_ref, carry_ref, *, d, aw, n_rope_chunks, n_fox_heads):
    @pl.when(pl.program_id(1) == 0)
    def _():
        carry_ref[...] = jnp.zeros_like(carry_ref)

    m = mod_ref[...]
    h = _rms(x_ref[...], g_ref[...]) * (1.0 + m[:, d:2 * d]) + m[:, 0:d]
    hb = h.astype(BF16)
    proj = jnp.dot(hb, w_ref[...], preferred_element_type=F32)
    rc, rsa, rsb = rc_ref[...], rsa_ref[...], rsb_ref[...]

    def rope(t):
        return t * rc + pltpu.roll(t, ROPE_DIMS // 2, 1) * rsa + pltpu.roll(t, LANES - ROPE_DIMS // 2, 1) * rsb

    scale = HEAD_DIM ** -0.5 * LOG2E
    for c in range(aw // LANES):
        sl = slice(c * LANES, (c + 1) * LANES)
        qc = proj[:, sl] * scale
        kc = proj[:, aw + c * LANES:aw + (c + 1) * LANES]
        if c < n_rope_chunks:
            qc, kc = rope(qc), rope(kc)
        q_ref[:, sl] = qc.astype(BF16)
        k_ref[:, sl] = kc.astype(BF16)
    v_ref[...] = proj[:, 2 * aw:3 * aw].astype(BF16)
    fl = jnp.dot(hb, wf_ref[...], preferred_element_type=F32) + bf_ref[...]
    logf = jnp.minimum(fl, 0.0) - jnp.log(1.0 + jnp.exp(-jnp.abs(fl)))
    tm = logf.shape[0]
    sub = min(FOX_PREP_ROWS, tm)
    for r0 in range(0, tm, sub):
        _fox_bias_terms(logf[r0:r0 + sub], carry_ref, qx_ref, kx_ref, r0, n_fox_heads)


def _l0_in(x, mods, g_pre, w_qkv, w_f, b_f, rope_c, rope_sa, rope_sb, tm, n_fox_heads):
    b, s, d = x.shape
    aw = w_qkv.shape[1] // 3
    d6 = mods.shape[-1]
    tok = lambda bi, i: (bi, i, 0)
    aux_w = n_fox_heads // 2 * LANES
    aux_spec = pl.BlockSpec((None, tm, aux_w), tok)
    aux_shape = jax.ShapeDtypeStruct((b, s, aux_w), BF16)
    kern = functools.partial(_l0_in_kernel, d=d, aw=aw, n_fox_heads=n_fox_heads,
                             n_rope_chunks=N_MOBA_HEADS * HEAD_DIM // LANES)
    return pl.pallas_call(
        kern,
        grid=(b, s // tm),
        in_specs=[
            pl.BlockSpec((None, tm, d), tok),
            pl.BlockSpec((None, None, 1, d6), lambda bi, i: (0, bi, 0, 0)),
            _const_spec((1, d)),
            _const_spec(w_qkv.shape),
            _const_spec(w_f.shape),
            _const_spec((1, LANES)),
            pl.BlockSpec((tm, LANES), lambda bi, i: (i, 0)),
            pl.BlockSpec((tm, LANES), lambda bi, i: (i, 0)),
            pl.BlockSpec((tm, LANES), lambda bi, i: (i, 0)),
        ],
        out_specs=[
            pl.BlockSpec((None, tm, aw), tok),
            pl.BlockSpec((None, tm, aw), tok),
            pl.BlockSpec((None, tm, aw), tok),
            aux_spec,
            aux_spec,
        ],
        out_shape=[
            jax.ShapeDtypeStruct((b, s, aw), BF16),
            jax.ShapeDtypeStruct((b, s, aw), BF16),
            jax.ShapeDtypeStruct((b, s, aw), BF16),
            aux_shape,
            aux_shape,
        ],
        scratch_shapes=[pltpu.VMEM((1, LANES), F32)],
        compiler_params=_cparams(("parallel", "arbitrary"), VMEM_LIMIT),
        name="l0_in",
    )(x, mods, g_pre, w_qkv, w_f, b_f, rope_c, rope_sa, rope_sb)


def _moba_gate_kernel(q_ref, k_ref, qx_ref, *, n_blk):
    s = k_ref.shape[0]
    t = MOBA_BLOCK
    rowi = lax.broadcasted_iota(jnp.int32, (n_blk, s), 0)
    cs = lax.broadcasted_iota(jnp.int32, (n_blk, s), 1)
    own = (cs >= rowi * t) & (cs < (rowi + 1) * t)
    km = jnp.dot(jnp.where(own, 1.0 / t, 0.0).astype(BF16), k_ref[...], preferred_element_type=F32)
    lane_k = lax.broadcasted_iota(jnp.int32, (n_blk, LANES), 1)
    km2 = jnp.concatenate([jnp.where(lane_k < HEAD_DIM, km, 0.0), jnp.where(lane_k >= HEAD_DIM, km, 0.0)], axis=0)
    hi = km2.astype(BF16)
    r1 = km2 - hi.astype(F32)
    mid = r1.astype(BF16)
    lo = (r1 - mid.astype(F32)).astype(BF16)
    g3 = lax.dot_general(jnp.concatenate([hi, mid, lo], axis=0), q_ref[...], (((1,), (1,)), ((), ())),
                         preferred_element_type=F32)
    g = g3[0:2 * n_blk] + g3[2 * n_blk:4 * n_blk] + g3[4 * n_blk:6 * n_blk]
    fully_past = (rowi + 1) * t <= cs
    biases = []
    for hh in range(2):
        gv = jnp.where(fully_past, g[hh * n_blk:(hh + 1) * n_blk], -jnp.inf)
        keep = own
        for _ in range(MOBA_TOPK):
            mx = jnp.max(gv, axis=0, keepdims=True)
            cand = jnp.where((gv == mx) & (mx > -jnp.inf), rowi, n_blk)
            pick = rowi == jnp.min(cand, axis=0, keepdims=True)
            keep = keep | pick
            gv = jnp.where(pick, -jnp.inf, gv)
        biases.append(jnp.where(keep, 0.0, NEG))
    fill = jnp.zeros((HEAD_DIM - n_blk, s), F32)
    qx_ref[...] = jnp.transpose(jnp.concatenate([biases[1], fill, biases[0], fill], axis=0)).astype(BF16)


def _moba_gate(q, k, n_heads):
    b, s, _ = q.shape
    n_blk = s // MOBA_BLOCK
    assert n_blk <= HEAD_DIM
    return pl.pallas_call(
        functools.partial(_moba_gate_kernel, n_blk=n_blk),
        grid=(b, n_heads // 2),
        in_specs=[
            pl.BlockSpec((None, s, LANES), lambda bi, p: (bi, 0, p)),
            pl.BlockSpec((None, s, LANES), lambda bi, p: (bi, 0, p)),
        ],
        out_specs=pl.BlockSpec((None, s, LANES), lambda bi, p: (bi, 0, p)),
        out_shape=jax.ShapeDtypeStruct((b, s, n_heads // 2 * LANES), BF16),
        compiler_params=_cparams(("parallel", "parallel")),
        name="moba_gate",
    )(q, k)


def _attn_kernel(q_ref, k_ref, vt_ref, qx_ref, kx_ref, o_ref):
    tq = q_ref.shape[0]
    t = ATT_KEYS
    n_sub = tq // ATT_SUB
    n_diag = tq // t
    g = pl.program_id(2)
    lane = lax.broadcasted_iota(jnp.int32, (ATT_SUB, LANES), 1)
    lane_k = lax.broadcasted_iota(jnp.int32, (t, LANES), 1)
    row_v = lax.broadcasted_iota(jnp.int32, (LANES, t), 0)
    row_o = lax.broadcasted_iota(jnp.int32, (LANES, ATT_SUB), 0)
    key = lax.broadcasted_iota(jnp.int32, (t, ATT_SUB), 0)
    qry = lax.broadcasted_iota(jnp.int32, (t, ATT_SUB), 1)
    nt = (((1,), (1,)), ((), ()))
    qas = []
    for hh in range(2):
        in_head = (lane < HEAD_DIM) if hh == 0 else (lane >= HEAD_DIM)
        for r in range(n_sub):
            rs = slice(r * ATT_SUB, (r + 1) * ATT_SUB)
            qas.append(jnp.where(in_head, q_ref[rs, :], qx_ref[rs, :]))

    def q_off(c):
        return (c % n_sub) * ATT_SUB

    def update(state, steps):
        loaded = {}

        def operands(si):
            if si not in loaded:
                j = steps[si][0]
                rows = pl.ds(pl.multiple_of(j * t, t), t)
                k = k_ref[rows, :]
                kx = kx_ref[rows, :]
                vt = vt_ref[:, rows]
                kks = [jnp.where((lane_k < HEAD_DIM) if hh == 0 else (lane_k >= HEAD_DIM), k, kx)
                       for hh in range(2)]
                vhs = [jnp.where((row_v < HEAD_DIM) if hh == 0 else (row_v >= HEAD_DIM), vt, jnp.ones_like(vt))
                       for hh in range(2)]
                loaded[si] = (kks, vhs)
            return loaded[si]

        items = [(si, c) for si, (_, d) in enumerate(steps) for c in range(2 * n_sub)
                 if d is None or q_off(c) + ATT_SUB > d * t]

        def qk(item):
            si, c = item
            return lax.dot_general(operands(si)[0][c // n_sub], qas[c], nt, preferred_element_type=F32)

        scs = {n: qk(items[n]) for n in range(min(ATT_LOOKAHEAD, len(items)))}
        new = list(state)
        for n, (si, c) in enumerate(items):
            if n + ATT_LOOKAHEAD < len(items):
                scs[n + ATT_LOOKAHEAD] = qk(items[n + ATT_LOOKAHEAD])
            d = steps[si][1]
            m_old, acc = new[c]
            sc = scs.pop(n)
            if d is not None and q_off(c) < (d + 1) * t:
                sc = jnp.where(key + d * t <= qry + q_off(c), sc, NEG)
            m_new = jnp.maximum(m_old, jnp.max(sc, axis=0, keepdims=True))
            p = jnp.exp2(sc - m_new).astype(BF16)
            acc = jnp.exp2(m_old - m_new) * acc + jnp.dot(operands(si)[1][c // n_sub], p,
                                                          preferred_element_type=F32)
            new[c] = (m_new, acc)
        return tuple(new)

    init = tuple((jnp.full((1, ATT_SUB), NEG, F32), jnp.zeros((LANES, ATT_SUB), F32)) for _ in range(2 * n_sub))
    per_iter = 2 if n_diag % 2 == 0 else 1
    state = lax.fori_loop(0, g * (n_diag // per_iter),
                          lambda j, st: update(st, [(j * per_iter + u, None) for u in range(per_iter)]), init)
    state = update(state, [(g * n_diag + d, d) for d in range(n_diag)])
    for r in range(n_sub):
        outs = [state[hh * n_sub + r][1] for hh in range(2)]
        outs = [acc / pltpu.roll(acc, HEAD_DIM, 0) for acc in outs]
        o_t = jnp.where(row_o < HEAD_DIM, outs[0], outs[1])
        o_ref[r * ATT_SUB:(r + 1) * ATT_SUB, :] = jnp.transpose(o_t).astype(o_ref.dtype)


def _attention(q, k, vt, qx, kx, n_heads, head_off):
    b, s, _ = q.shape
    t = min(ATT_QUERIES, s)
    po = head_off // 2
    shared_kx = kx.shape[0] == 1 and kx.shape[2] == LANES
    kx_map = (lambda bi, p, i: (0, 0, 0)) if shared_kx else (lambda bi, p, i: (bi, 0, p))
    return pl.pallas_call(
        _attn_kernel,
        grid=(b, n_heads // 2, s // t),
        in_specs=[
            pl.BlockSpec((None, t, LANES), lambda bi, p, i: (bi, i, p + po)),
            pl.BlockSpec((None, s, LANES), lambda bi, p, i: (bi, 0, p + po)),
            pl.BlockSpec((None, LANES, s), lambda bi, p, i: (bi, p + po, 0)),
            pl.BlockSpec((None, t, LANES), lambda bi, p, i: (bi, i, p)),
            pl.BlockSpec((None, s, LANES), kx_map),
        ],
        out_specs=pl.BlockSpec((None, t, LANES), lambda bi, p, i: (bi, i, p)),
        out_shape=jax.ShapeDtypeStruct((b, s, n_heads * HEAD_DIM), BF16),
        compiler_params=_cparams(("parallel", "parallel", "arbitrary")),
        name="attn",
    )(q, k, vt, qx, kx)


def _l0_tail_kernel(om_ref, of_ref, x_ref, wo_ref, wg_ref, wu_ref, wd_ref, mod0_ref, mod1_ref, g_ref,
                    x2_ref, h3_ref, *, d):
    m0 = mod0_ref[...]
    m1 = mod1_ref[...]
    hw = om_ref.shape[1]
    y = (jnp.dot(om_ref[...], wo_ref[0:hw, :], preferred_element_type=F32)
         + jnp.dot(of_ref[...], wo_ref[hw:, :], preferred_element_type=F32))
    x1 = x_ref[...] + m0[:, 2 * d:3 * d] * _rms(y, g_ref[0:1, :])
    h = (_rms(x1, g_ref[1:2, :]) * (1.0 + m0[:, 4 * d:5 * d]) + m0[:, 3 * d:4 * d]).astype(BF16)
    a = jnp.dot(h, wg_ref[...], preferred_element_type=F32)
    u = jnp.dot(h, wu_ref[...], preferred_element_type=F32)
    act = (a * jax.nn.sigmoid(a) * u).astype(BF16)
    y = jnp.dot(act, wd_ref[...], preferred_element_type=F32)
    x2 = x1 + m0[:, 5 * d:6 * d] * _rms(y, g_ref[2:3, :])
    x2_ref[...] = x2
    h3_ref[...] = (_rms(x2, g_ref[3:4, :]) * (1.0 + m1[:, d:2 * d]) + m1[:, 0:d]).astype(BF16)


def _l0_tail(o_m, o_f, x, w_out, w_gate, w_up, w_down, mods, gains, tm):
    b, s, d = x.shape
    d6 = mods.shape[-1]
    tok = lambda bi, i: (bi, i, 0)
    single = dict(pipeline_mode=pl.Buffered(1))
    return pl.pallas_call(
        functools.partial(_l0_tail_kernel, d=d),
        grid=(b, s // tm),
        in_specs=[
            pl.BlockSpec((None, tm, o_m.shape[2]), tok),
            pl.BlockSpec((None, tm, o_f.shape[2]), tok),
            pl.BlockSpec((None, tm, d), tok),
            pl.BlockSpec(w_out.shape, lambda bi, i: (0, 0), **single),
            pl.BlockSpec(w_gate.shape, lambda bi, i: (0, 0), **single),
            pl.BlockSpec(w_up.shape, lambda bi, i: (0, 0), **single),
            pl.BlockSpec(w_down.shape, lambda bi, i: (0, 0), **single),
            pl.BlockSpec((None, None, 1, d6), lambda bi, i: (0, bi, 0, 0)),
            pl.BlockSpec((None, None, 1, d6), lambda bi, i: (1, bi, 0, 0)),
            _const_spec(gains.shape),
        ],
        out_specs=[pl.BlockSpec((None, tm, d), tok), pl.BlockSpec((None, tm, d), tok)],
        out_shape=[jax.ShapeDtypeStruct((b, s, d), F32), jax.ShapeDtypeStruct((b, s, d), BF16)],
        compiler_params=_cparams(("parallel", "parallel"), VMEM_LIMIT),
        name="l0_tail",
    )(o_m, o_f, x, w_out, w_gate, w_up, w_down, mods, mods, gains)


def _lru_kernel(h_ref, x_ref, win_ref, cw_ref, cb_ref, wax_ref, ba_ref, bx_ref, lam_ref, wout_ref,
                mod_ref, g1_ref, g2_ref, wr_ref, br_ref,
                x3_ref, h4_ref, lg_ref, utail_ref, hc_ref, *, d, dr, n_rnn_blocks):
    tm = h_ref.shape[0]
    sub = utail_ref.shape[0]

    @pl.when(pl.program_id(1) == 0)
    def _():
        utail_ref[...] = jnp.zeros_like(utail_ref)
        hc_ref[...] = jnp.zeros_like(hc_ref)

    proj = jnp.dot(h_ref[...], win_ref[...], preferred_element_type=F32)
    gate_branch = proj[:, :dr]
    u = proj[:, dr:]
    row_sub = lax.broadcasted_iota(jnp.int32, (sub, dr), 0)
    n_groups = tm // sub
    taps = [cw_ref[j:j + 1, :] for j in range(CONV_WIDTH)]
    bias = cb_ref[...]
    prev = utail_ref[...]
    rolled_prev = [None] + [pltpu.roll(prev, back, 0) for back in range(1, CONV_WIDTH)]
    conv_groups = []
    for g in range(n_groups):
        ug = u[g * sub:(g + 1) * sub]
        cg = bias + ug * taps[CONV_WIDTH - 1]
        for back in range(1, CONV_WIDTH):
            rolled = pltpu.roll(ug, back, 0)
            cg = cg + jnp.where(row_sub >= back, rolled, rolled_prev[back]) * taps[CONV_WIDTH - 1 - back]
            rolled_prev[back] = rolled
        conv_groups.append(cg)
    conv = jnp.concatenate(conv_groups, axis=0)
    utail_ref[...] = u[tm - sub:tm]

    w = dr // n_rnn_blocks
    rs, is_ = [], []
    for n in range(n_rnn_blocks):
        cbk = conv[:, n * w:(n + 1) * w].astype(BF16)
        ra = jnp.dot(cbk, wax_ref[n], preferred_element_type=F32)
        rs.append(ra[:, :w])
        is_.append(ra[:, w:])
    r = jax.nn.sigmoid(jnp.concatenate(rs, axis=1) + ba_ref[...])
    ig = jax.nn.sigmoid(jnp.concatenate(is_, axis=1) + bx_ref[...])
    nl = -lam_ref[...]
    softplus = jnp.maximum(nl, 0.0) + jnp.log(1.0 + jnp.exp(-jnp.abs(nl)))
    log_a = (-RG_C * r) * softplus
    a = jnp.exp(log_a)
    xin = jnp.exp2(0.5 * jnp.log2(1.0 - a * a)) * (ig * conv)

    carry = hc_ref[...]
    groups = []
    for g in range(n_groups):
        sa, sx = a[g * sub:(g + 1) * sub], xin[g * sub:(g + 1) * sub]
        dist = 1
        while dist < sub:
            keep = row_sub >= dist
            xs = jnp.where(keep, pltpu.roll(sx, dist, 0), 0.0)
            as_ = jnp.where(keep, pltpu.roll(sa, dist, 0), 1.0)
            sx = sx + sa * xs
            sa = sa * as_
            dist *= 2
        hg = sx + sa * carry
        carry = hg[sub - 1:sub]
        groups.append(hg)
    hs = jnp.concatenate(groups, axis=0)
    hc_ref[...] = carry

    y = (jax.nn.gelu(gate_branch, approximate=True) * hs).astype(BF16)
    out = jnp.dot(y, wout_ref[...], preferred_element_type=F32)
    m = mod_ref[...]
    x3 = x_ref[...] + m[:, 2 * d:3 * d] * _rms(out, g1_ref[...])
    x3_ref[...] = x3
    h4 = _rms(x3, g2_ref[...]) * (1.0 + m[:, 4 * d:5 * d]) + m[:, 3 * d:4 * d]
    h4_ref[...] = h4
    h_hi = h4.astype(BF16)
    h_lo = (h4 - h_hi.astype(F32)).astype(BF16)
    lg_ref[...] = jnp.dot(jnp.concatenate([h_hi, h_hi, h_lo], axis=1), wr_ref[...],
                          preferred_element_type=F32) + br_ref[...]


def _lru(h3, x2, w_in, conv_w, conv_b, wax, b_a, b_x, lam, w_out, mods, g_post, g_pre2, w_r, b_r, tm):
    b, s, d = x2.shape
    dr = w_out.shape[0]
    d6 = mods.shape[-1]
    nb = wax.shape[0]
    tok = lambda bi, i: (bi, i, 0)
    return pl.pallas_call(
        functools.partial(_lru_kernel, d=d, dr=dr, n_rnn_blocks=nb),
        grid=(b, s // tm),
        in_specs=[
            pl.BlockSpec((None, tm, d), tok),
            pl.BlockSpec((None, tm, d), tok),
            _const_spec(w_in.shape),
            _const_spec(conv_w.shape),
            _const_spec((1, dr)),
            _const_spec(wax.shape),
            _const_spec((1, dr)),
            _const_spec((1, dr)),
            _const_spec((1, dr)),
            _const_spec(w_out.shape),
            pl.BlockSpec((None, None, 1, d6), lambda bi, i: (1, bi, 0, 0)),
            _const_spec((1, d)),
            _const_spec((1, d)),
            _const_spec(w_r.shape),
            _const_spec((1, LANES)),
        ],
        out_specs=[pl.BlockSpec((None, tm, d), tok), pl.BlockSpec((None, tm, d), tok),
                   pl.BlockSpec((None, tm, LANES), tok)],
        out_shape=[jax.ShapeDtypeStruct((b, s, d), F32), jax.ShapeDtypeStruct((b, s, d), F32),
                   jax.ShapeDtypeStruct((b, s, LANES), F32)],
        scratch_shapes=[pltpu.VMEM((SUBLANES, dr), F32), pltpu.VMEM((1, dr), F32)],
        compiler_params=_cparams(("parallel", "arbitrary"), VMEM_LIMIT),
        name="lru",
    )(h3, x2, w_in, conv_w, conv_b, wax, b_a, b_x, lam, w_out, mods, g_post, g_pre2, w_r, b_r)


def _moe_kernel(bexp_ref, nused_ref, tok_ref, dst_ref, h_hbm, wg_hbm, wu_hbm, wd_hbm, o_hbm,
                xbuf, ybuf, wg_bf, wu_bf, wd_bf, stage_g, stage_u, stage_d, gsem, ssem, wsem,
                *, n_blocks, dummy_base):
    mb = xbuf.shape[1]
    i = pl.program_id(0)
    nused = nused_ref[0]
    slot = lax.rem(i, 2)
    other = 1 - slot

    def gather_copy(blk, buf, r):
        tok = tok_ref[blk * mb + r]
        return pltpu.make_async_copy(h_hbm.at[pl.ds(tok, 1), :], xbuf.at[buf, pl.ds(r, 1), :], gsem.at[buf])

    def scatter_copy(dst, buf, r):
        return pltpu.make_async_copy(ybuf.at[buf, pl.ds(r, 1), :], o_hbm.at[pl.ds(dst, 1), :], ssem.at[buf])

    def wait_gather(buf):
        pltpu.make_async_copy(h_hbm.at[pl.ds(0, mb), :], xbuf.at[buf], gsem.at[buf]).wait()

    def wait_scatter(buf):
        pltpu.make_async_copy(ybuf.at[buf], o_hbm.at[pl.ds(0, mb), :], ssem.at[buf]).wait()

    @pl.when(i == 0)
    def _():
        def body(r, c):
            gather_copy(0, 0, r).start()
            return c
        lax.fori_loop(0, mb, body, 0, unroll=8)
        ybuf[1] = jnp.zeros(ybuf.shape[1:], F32)

    @pl.when((i >= 1) & (i <= nused))
    def _():
        wait_scatter(slot)

    ck = MOE_FF_CHUNK
    n_chunks = wg_bf.shape[1] // ck
    e_cur = bexp_ref[jnp.minimum(i, n_blocks - 1)]
    e_prev = bexp_ref[jnp.maximum(i - 1, 0)]
    new_expert = (i == 0) | (e_cur != e_prev)

    def weight_copies(c, buf):
        cs = pl.ds(c * ck, ck)
        return (pltpu.make_async_copy(wg_hbm.at[e_cur, :, cs], stage_g.at[buf], wsem.at[buf]),
                pltpu.make_async_copy(wu_hbm.at[e_cur, :, cs], stage_u.at[buf], wsem.at[buf]),
                pltpu.make_async_copy(wd_hbm.at[e_cur, cs, :], stage_d.at[buf], wsem.at[buf]))

    def compute(load_weights):
        wait_gather(slot)
        x = xbuf[slot].astype(BF16)
        acc = jnp.zeros((mb, wd_bf.shape[1]), F32)
        per = -(-mb // n_chunks)
        nxt = jnp.minimum(i + 1, n_blocks - 1)
        prv = jnp.maximum(i - 1, 0)
        if load_weights:
            for c in range(min(2, n_chunks)):
                for cp in weight_copies(c, c % 2):
                    cp.start()
        for c in range(n_chunks):
            for r in range(c * per, min((c + 1) * per, mb)):
                gather_copy(nxt, other, r).start()
                dst = jnp.where(i == 0, dummy_base + r, dst_ref[prv * mb + r])
                scatter_copy(dst, other, r).start(priority=1)
            cs = slice(c * ck, (c + 1) * ck)
            if load_weights:
                for cp in weight_copies(c, c % 2):
                    cp.wait()
                wg_bf[:, cs] = stage_g[c % 2].astype(BF16)
                wu_bf[:, cs] = stage_u[c % 2].astype(BF16)
                wd_bf[cs, :] = stage_d[c % 2].astype(BF16)
                if c + 2 < n_chunks:
                    for cp in weight_copies(c + 2, c % 2):
                        cp.start()
            a = jnp.dot(x, wg_bf[:, cs], preferred_element_type=F32)
            u = jnp.dot(x, wu_bf[:, cs], preferred_element_type=F32)
            act = (a * jax.nn.sigmoid(a) * u).astype(BF16)
            acc = acc + jnp.dot(act, wd_bf[cs, :], preferred_element_type=F32)
        ybuf[slot] = acc

    @pl.when((i < nused) & new_expert)
    def _():
        compute(True)

    @pl.when((i < nused) & jnp.logical_not(new_expert))
    def _():
        compute(False)

    @pl.when(i == nused)
    def _():
        wait_gather(slot)

        def body(r, c):
            scatter_copy(dst_ref[(i - 1) * mb + r], other, r).start()
            return c
        lax.fori_loop(0, mb, body, 0, unroll=8)
        wait_scatter(other)


def _moe(block_exp, nused, slot_tok, slot_dst, h4, w_gate, w_up, w_down, n_out_rows):
    n_tok, d = h4.shape
    mb = MOE_BLOCK_ROWS
    cap = slot_tok.shape[0]
    n_blocks = cap // mb
    e, _, dff = w_gate.shape
    ck = MOE_FF_CHUNK
    assert dff % ck == 0
    grid_spec = pltpu.PrefetchScalarGridSpec(
        num_scalar_prefetch=4,
        grid=(n_blocks + 1,),
        in_specs=[pl.BlockSpec(memory_space=pl.ANY)] * 4,
        out_specs=pl.BlockSpec(memory_space=pl.ANY),
        scratch_shapes=[pltpu.VMEM((2, mb, d), F32), pltpu.VMEM((2, mb, d), F32),
                        pltpu.VMEM((d, dff), BF16), pltpu.VMEM((d, dff), BF16), pltpu.VMEM((dff, d), BF16),
                        pltpu.VMEM((2, d, ck), F32), pltpu.VMEM((2, d, ck), F32), pltpu.VMEM((2, ck, d), F32),
                        pltpu.SemaphoreType.DMA((2,)), pltpu.SemaphoreType.DMA((2,)),
                        pltpu.SemaphoreType.DMA((2,))],
    )
    return pl.pallas_call(
        functools.partial(_moe_kernel, n_blocks=n_blocks, dummy_base=n_out_rows),
        grid_spec=grid_spec,
        out_shape=jax.ShapeDtypeStruct((n_out_rows + mb, d), F32),
        compiler_params=_cparams(("arbitrary",), VMEM_LIMIT),
        name="moe",
    )(block_exp, nused, slot_tok, slot_dst, h4, w_gate, w_up, w_down)


def _moe_out_kernel(*refs, d):
    y_refs, (gate_ref, x_ref, mod_ref, g_ref, o_ref) = refs[:TOP_K], refs[TOP_K:]
    g = gate_ref[...]
    y = y_refs[0][...] * g[:, 0:1]
    for kk in range(1, TOP_K):
        y = y + y_refs[kk][...] * g[:, kk:kk + 1]
    m = mod_ref[...]
    o_ref[...] = x_ref[...] + m[:, 5 * d:6 * d] * _rms(y, g_ref[...])


def _moe_out(y_rows, gates, x3, mods, g_post, s, tm):
    n_tok, d = x3.shape
    d6 = mods.shape[-1]
    per_b = s // tm
    nt = n_tok // tm
    y_specs = [pl.BlockSpec((tm, d), functools.partial(lambda i, kk: (kk * nt + i, 0), kk=kk))
               for kk in range(TOP_K)]
    return pl.pallas_call(
        functools.partial(_moe_out_kernel, d=d),
        grid=(nt,),
        in_specs=y_specs + [
            pl.BlockSpec((tm, TOP_K), lambda i: (i, 0)),
            pl.BlockSpec((tm, d), lambda i: (i, 0)),
            pl.BlockSpec((None, None, 1, d6), lambda i: (1, i // per_b, 0, 0)),
            pl.BlockSpec((1, d), lambda i: (0, 0)),
        ],
        out_specs=pl.BlockSpec((tm, d), lambda i: (i, 0)),
        out_shape=jax.ShapeDtypeStruct((n_tok, d), F32),
        compiler_params=_cparams(("parallel",), VMEM_LIMIT),
        name="moe_out",
    )(*([y_rows] * TOP_K), gates, x3, mods, g_post)


def _route(logits, n_experts, mb):
    n_tok = logits.shape[0]
    top_logit, top_idx = lax.top_k(logits, TOP_K)
    gates = jax.nn.softmax(top_logit, axis=-1)
    exp_flat = top_idx.reshape(-1).astype(jnp.int32)
    n_asg = n_tok * TOP_K
    onehot = (exp_flat[:, None] == jnp.arange(n_experts, dtype=jnp.int32)[None, :]).astype(jnp.int32)
    counts = jnp.sum(onehot, axis=0)
    padded = ((counts + mb - 1) // mb) * mb
    pends = jnp.cumsum(padded)
    cap = (-(-n_asg // mb) + n_experts) * mb
    pad_ends = jnp.cumsum(padded - counts)
    pad_exp = jnp.sum(jnp.arange(cap - n_asg, dtype=jnp.int32)[:, None] >= pad_ends[None, :], axis=1)
    keys = jnp.concatenate([exp_flat * 2, pad_exp.astype(jnp.int32) * 2 + 1])
    payload = jnp.concatenate([jnp.arange(n_asg, dtype=jnp.int32), jnp.full((cap - n_asg,), -1, jnp.int32)])
    _, slot_flat = lax.sort((keys, payload), num_keys=1, is_stable=True)
    valid = slot_flat >= 0
    slot_tok = jnp.where(valid, slot_flat // TOP_K, 0)
    slot_dst = jnp.where(valid, (slot_flat % TOP_K) * n_tok + slot_flat // TOP_K,
                         n_asg + jnp.arange(cap, dtype=jnp.int32) % mb)
    n_blocks = cap // mb
    block_starts = jnp.arange(n_blocks, dtype=jnp.int32) * mb
    block_exp = jnp.minimum(jnp.sum(block_starts[:, None] >= pends[None, :], axis=1), n_experts - 1).astype(jnp.int32)
    nused = (pends[-1] // mb).astype(jnp.int32).reshape(1)
    return gates, slot_tok, slot_dst, block_exp, nused, n_asg


def _rope_tables(s):
    half = ROPE_DIMS // 2
    inv_freq = jnp.power(ROPE_THETA, -jnp.arange(half, dtype=F32) / half)
    ang = jnp.arange(s, dtype=F32)[:, None] * inv_freq[None, :]
    cos, sin = jnp.cos(ang), jnp.sin(ang)
    lane = jnp.arange(LANES) % HEAD_DIM
    idx = lane % half
    is_x1 = lane < half
    is_x2 = (lane >= half) & (lane < ROPE_DIMS)
    c = jnp.where((is_x1 | is_x2)[None, :], cos[:, idx], 1.0)
    sa = jnp.where(is_x2[None, :], sin[:, idx], 0.0)
    sb = jnp.where(is_x1[None, :], -sin[:, idx], 0.0)
    return c.astype(F32), sa.astype(F32), sb.astype(F32)


def kernel(x, c, w_ada, b_ada, norm_g, attn_w_in, fox_b_f, attn_w_out, ffn_w_gate, ffn_w_up, ffn_w_down,
           lru_w_in, lru_conv_w, lru_conv_b, lru_w_a, lru_b_a, lru_w_x, lru_b_x, lru_lambda, lru_w_out,
           moe_w_router, moe_b_router, moe_w_gate, moe_w_up, moe_w_down):
    b, s, d = x.shape
    aw = attn_w_out.shape[1]
    n_experts = moe_w_router.shape[2]
    assert s % min(ATT_QUERIES, s) == 0 and s % ATT_KEYS == 0
    assert ATT_QUERIES % ATT_KEYS == 0 and ATT_KEYS % MOBA_BLOCK == 0
    assert ATT_QUERIES % ATT_SUB == 0
    tm = min(512, s)

    c_pad = jnp.zeros((8, d), F32).at[:b].set(c)
    mods = _ada(c_pad, w_ada, b_ada)[:, :b].reshape(w_ada.shape[0], b, 1, 6 * d)

    w_in = attn_w_in[0]
    w_qkv = w_in[:, :3 * aw].astype(BF16)
    w_f = jnp.zeros((d, LANES), F32).at[:, :N_FOX_HEADS].set(w_in[:, 3 * aw:]).astype(BF16)
    b_f = jnp.zeros((1, LANES), F32).at[0, :N_FOX_HEADS].set(fox_b_f[0])
    rope_c, rope_sa, rope_sb = _rope_tables(s)
    q, k, v, qx_f, kx_f = _l0_in(x, mods, norm_g[0, 0][None], w_qkv, w_f, b_f, rope_c, rope_sa, rope_sb,
                                 tm, N_FOX_HEADS)

    qx_m = _moba_gate(q, k, N_MOBA_HEADS)
    blk_of_pos = jnp.arange(s, dtype=jnp.int32) // MOBA_BLOCK
    lane_blk = jnp.arange(LANES, dtype=jnp.int32) % HEAD_DIM
    kx_m = (lane_blk[None, :] == blk_of_pos[:, None]).astype(BF16)[None]
    vt = jnp.swapaxes(v, 1, 2)
    o_m = _attention(q, k, vt, qx_m, kx_m, N_MOBA_HEADS, 0)
    o_f = _attention(q, k, vt, qx_f, kx_f, N_FOX_HEADS, N_MOBA_HEADS)

    gains = jnp.concatenate([norm_g[0, 1:4], norm_g[1, 0:1]], axis=0)
    x2, h3 = _l0_tail(o_m, o_f, x, attn_w_out[0].astype(BF16), ffn_w_gate[0].astype(BF16),
                      ffn_w_up[0].astype(BF16), ffn_w_down[0].astype(BF16), mods, gains, min(256, s))

    wax = jnp.concatenate([lru_w_a[0], lru_w_x[0]], axis=-1).astype(BF16)
    w_r = jnp.zeros((d, LANES), F32).at[:, :n_experts].set(moe_w_router[0])
    w_r_hi = w_r.astype(BF16)
    w_r_lo = (w_r - w_r_hi.astype(F32)).astype(BF16)
    w_r = jnp.concatenate([w_r_hi, w_r_lo, w_r_hi], axis=0)
    b_r = jnp.zeros((1, LANES), F32).at[0, :n_experts].set(moe_b_router[0])
    x3, h4, logits = _lru(h3, x2, lru_w_in[0].astype(BF16), lru_conv_w[0], lru_conv_b[0][None], wax,
                          lru_b_a[0][None], lru_b_x[0][None], lru_lambda[0][None], lru_w_out[0].astype(BF16),
                          mods, norm_g[1, 1][None], norm_g[1, 2][None], w_r, b_r, min(256, s))

    n_tok = b * s
    gates, slot_tok, slot_dst, block_exp, nused, n_rows = _route(
        logits.reshape(n_tok, LANES)[:, :n_experts], n_experts, MOE_BLOCK_ROWS)
    y_rows = _moe(block_exp, nused, slot_tok, slot_dst, h4.reshape(n_tok, d),
                  moe_w_gate.reshape(moe_w_gate.shape[1:]), moe_w_up.reshape(moe_w_up.shape[1:]),
                  moe_w_down.reshape(moe_w_down.shape[1:]), n_rows)
    out = _moe_out(y_rows, gates, x3.reshape(n_tok, d), mods, norm_g[1, 3][None], s, min(256, s))
    return out.reshape(b, s, d)
```

```python
import functools

import jax
import jax.numpy as jnp
from jax import lax
from jax.experimental import pallas as pl
from jax.experimental.pallas import tpu as pltpu

F32 = jnp.float32
BF16 = jnp.bfloat16
HIGHEST = lax.Precision.HIGHEST

NORM_EPS = 1e-6
HEAD_DIM = 64
N_MOBA_HEADS = 8
N_FOX_HEADS = 8
ROPE_DIMS = 16
ROPE_THETA = 500000.0
MOBA_BLOCK = 256
MOBA_TOPK = 3
CONV_WIDTH = 4
RG_C = 8.0
TOP_K = 2

LANES = 128
SUBLANES = 8
FOX_PREP_ROWS = 256
ATT_QUERIES = 2048
ATT_KEYS = 512
ATT_SUB = 256
ATT_LOOKAHEAD = 6
NEG = -1e30
LOG2E = 1.4426950408889634
MOE_BLOCK_ROWS = 512
MOE_FF_CHUNK = 512
VMEM_LIMIT = 56 * 1024 * 1024


def _cparams(sem, vmem=None):
    return pltpu.CompilerParams(dimension_semantics=sem, vmem_limit_bytes=vmem)


def _rms(x, g):
    return x * lax.rsqrt(jnp.mean(x * x, axis=-1, keepdims=True) + NORM_EPS) * g


def _const_spec(shape):
    n = len(shape)
    return pl.BlockSpec(shape, lambda *_: (0,) * n)


def _ada_kernel(c_ref, w_ref, b_ref, o_ref):
    c = c_ref[...]
    cond = c * jax.nn.sigmoid(c)
    o_ref[...] = jnp.dot(cond, w_ref[...], preferred_element_type=F32, precision=HIGHEST) + b_ref[...]


def _ada(c_pad, w_ada, b_ada):
    depth, d, d6 = w_ada.shape
    rows = c_pad.shape[0]
    nj = d6 // d
    return pl.pallas_call(
        _ada_kernel,
        grid=(depth, nj),
        in_specs=[
            pl.BlockSpec((rows, d), lambda l, j: (0, 0)),
            pl.BlockSpec((None, d, d), lambda l, j: (l, 0, j)),
            pl.BlockSpec((None, 1, d), lambda l, j: (l, 0, j)),
        ],
        out_specs=pl.BlockSpec((None, rows, d), lambda l, j: (l, 0, j)),
        out_shape=jax.ShapeDtypeStruct((depth, rows, d6), F32),
        compiler_params=_cparams(("parallel", "parallel")),
        name="ada",
    )(c_pad, w_ada, b_ada.reshape(depth, 1, d6))


def _fox_bias_terms(logf, carry_ref, qx_ref, kx_ref, row0, n_heads):
    t = logf.shape[0]
    row = lax.broadcasted_iota(jnp.int32, (t, t), 0)
    col = lax.broadcasted_iota(jnp.int32, (t, t), 1)
    tri = jnp.where(col <= row, 1.0, 0.0).astype(BF16)
    l_hi = logf.astype(BF16)
    r0 = logf - l_hi.astype(F32)
    l_mid = r0.astype(BF16)
    l_lo = (r0 - l_mid.astype(F32)).astype(BF16)
    c3 = jnp.dot(tri, jnp.concatenate([l_hi, l_mid, l_lo], axis=1), preferred_element_type=F32)
    cum = c3[:, 0:LANES] + c3[:, LANES:2 * LANES] + c3[:, 2 * LANES:3 * LANES] + carry_ref[...]
    carry_ref[...] = cum[t - 1:t, :]
    c = cum * LOG2E
    hi = c.astype(BF16).astype(F32)
    r1 = c - hi
    mid = r1.astype(BF16).astype(F32)
    lo = r1 - mid
    lane = lax.broadcasted_iota(jnp.int32, (t, LANES), 1)
    n_terms = 3
    in_terms = (lane % HEAD_DIM) < n_terms
    in_ones = ((lane % HEAD_DIM) >= n_terms) & ((lane % HEAD_DIM) < 2 * n_terms)
    for p in range(n_heads // 2):
        vals = jnp.zeros((t, LANES), F32)
        for h, base in ((2 * p, HEAD_DIM), (2 * p + 1, 0)):
            for n, term in enumerate((hi, mid, lo)):
                shift = (base + n - h) % LANES
                moved = pltpu.roll(term, shift, 1) if shift else term
                vals = jnp.where(lane == base + n, moved, vals)
        qx = jnp.where(in_ones, 1.0, vals)
        kx = jnp.where(in_terms, 1.0, jnp.where(in_ones, -pltpu.roll(vals, n_terms, 1), 0.0))
        qx_ref[row0:row0 + t, p * LANES:(p + 1) * LANES] = qx.astype(BF16)
        kx_ref[row0:row0 + t, p * LANES:(p + 1) * LANES] = kx.astype(BF16)


def _l0_in_kernel(x_ref, mod_ref, g_ref, w_ref, wf_ref, bf_ref, rc_ref, rsa_ref, rsb_ref,
                  q_ref, k_ref, v_ref, qx_ref, kx_ref, carry_ref, *, d, aw, n_rope_chunks, n_fox_heads):
    @pl.when(pl.program_id(1) == 0)
    def _():
        carry_ref[...] = jnp.zeros_like(carry_ref)

    m = mod_ref[...]
    h = _rms(x_ref[...], g_ref[...]) * (1.0 + m[:, d:2 * d]) + m[:, 0:d]
    hb = h.astype(BF16)
    proj = jnp.dot(hb, w_ref[...], preferred_element_type=F32)
    rc, rsa, rsb = rc_ref[...], rsa_ref[...], rsb_ref[...]

    def rope(t):
        return t * rc + pltpu.roll(t, ROPE_DIMS // 2, 1) * rsa + pltpu.roll(t, LANES - ROPE_DIMS // 2, 1) * rsb

    scale = HEAD_DIM ** -0.5 * LOG2E
    for c in range(aw // LANES):
        sl = slice(c * LANES, (c + 1) * LANES)
        qc = proj[:, sl] * scale
        kc = proj[:, aw + c * LANES:aw + (c + 1) * LANES]
        if c < n_rope_chunks:
            qc, kc = rope(qc), rope(kc)
        q_ref[:, sl] = qc.astype(BF16)
        k_ref[:, sl] = kc.astype(BF16)
    v_ref[...] = proj[:, 2 * aw:3 * aw].astype(BF16)
    fl = jnp.dot(hb, wf_ref[...], preferred_element_type=F32) + bf_ref[...]
    logf = jnp.minimum(fl, 0.0) - jnp.log(1.0 + jnp.exp(-jnp.abs(fl)))
    tm = logf.shape[0]
    sub = min(FOX_PREP_ROWS, tm)
    for r0 in range(0, tm, sub):
        _fox_bias_terms(logf[r0:r0 + sub], carry_ref, qx_ref, kx_ref, r0, n_fox_heads)


def _l0_in(x, mods, g_pre, w_qkv, w_f, b_f, rope_c, rope_sa, rope_sb, tm, n_fox_heads):
    b, s, d = x.shape
    aw = w_qkv.shape[1] // 3
    d6 = mods.shape[-1]
    tok = lambda bi, i: (bi, i, 0)
    aux_w = n_fox_heads // 2 * LANES
    aux_spec = pl.BlockSpec((None, tm, aux_w), tok)
    aux_shape = jax.ShapeDtypeStruct((b, s, aux_w), BF16)
    kern = functools.partial(_l0_in_kernel, d=d, aw=aw, n_fox_heads=n_fox_heads,
                             n_rope_chunks=N_MOBA_HEADS * HEAD_DIM // LANES)
    return pl.pallas_call(
        kern,
        grid=(b, s // tm),
        in_specs=[
            pl.BlockSpec((None, tm, d), tok),
            pl.BlockSpec((None, None, 1, d6), lambda bi, i: (0, bi, 0, 0)),
            _const_spec((1, d)),
            _const_spec(w_qkv.shape),
            _const_spec(w_f.shape),
            _const_spec((1, LANES)),
            pl.BlockSpec((tm, LANES), lambda bi, i: (i, 0)),
            pl.BlockSpec((tm, LANES), lambda bi, i: (i, 0)),
            pl.BlockSpec((tm, LANES), lambda bi, i: (i, 0)),
        ],
        out_specs=[
            pl.BlockSpec((None, tm, aw), tok),
            pl.BlockSpec((None, tm, aw), tok),
            pl.BlockSpec((None, tm, aw), tok),
            aux_spec,
            aux_spec,
        ],
        out_shape=[
            jax.ShapeDtypeStruct((b, s, aw), BF16),
            jax.ShapeDtypeStruct((b, s, aw), BF16),
            jax.ShapeDtypeStruct((b, s, aw), BF16),
            aux_shape,
            aux_shape,
        ],
        scratch_shapes=[pltpu.VMEM((1, LANES), F32)],
        compiler_params=_cparams(("parallel", "arbitrary"), VMEM_LIMIT),
        name="l0_in",
    )(x, mods, g_pre, w_qkv, w_f, b_f, rope_c, rope_sa, rope_sb)


def _moba_gate_kernel(q_ref, k_ref, qx_ref, *, n_blk):
    s = k_ref.shape[0]
    t = MOBA_BLOCK
    rowi = lax.broadcasted_iota(jnp.int32, (n_blk, s), 0)
    cs = lax.broadcasted_iota(jnp.int32, (n_blk, s), 1)
    own = (cs >= rowi * t) & (cs < (rowi + 1) * t)
    km = jnp.dot(jnp.where(own, 1.0 / t, 0.0).astype(BF16), k_ref[...], preferred_element_type=F32)
    lane_k = lax.broadcasted_iota(jnp.int32, (n_blk, LANES), 1)
    km2 = jnp.concatenate([jnp.where(lane_k < HEAD_DIM, km, 0.0), jnp.where(lane_k >= HEAD_DIM, km, 0.0)], axis=0)
    hi = km2.astype(BF16)
    r1 = km2 - hi.astype(F32)
    mid = r1.astype(BF16)
    lo = (r1 - mid.astype(F32)).astype(BF16)
    g3 = lax.dot_general(jnp.concatenate([hi, mid, lo], axis=0), q_ref[...], (((1,), (1,)), ((), ())),
                         preferred_element_type=F32)
    g = g3[0:2 * n_blk] + g3[2 * n_blk:4 * n_blk] + g3[4 * n_blk:6 * n_blk]
    fully_past = (rowi + 1) * t <= cs
    biases = []
    for hh in range(2):
        gv = jnp.where(fully_past, g[hh * n_blk:(hh + 1) * n_blk], -jnp.inf)
        keep = own
        for _ in range(MOBA_TOPK):
            mx = jnp.max(gv, axis=0, keepdims=True)
            cand = jnp.where((gv == mx) & (mx > -jnp.inf), rowi, n_blk)
            pick = rowi == jnp.min(cand, axis=0, keepdims=True)
            keep = keep | pick
            gv = jnp.where(pick, -jnp.inf, gv)
        biases.append(jnp.where(keep, 0.0, NEG))
    fill = jnp.zeros((HEAD_DIM - n_blk, s), F32)
    qx_ref[...] = jnp.transpose(jnp.concatenate([biases[1], fill, biases[0], fill], axis=0)).astype(BF16)


def _moba_gate(q, k, n_heads):
    b, s, _ = q.shape
    n_blk = s // MOBA_BLOCK
    assert n_blk <= HEAD_DIM
    return pl.pallas_call(
        functools.partial(_moba_gate_kernel, n_blk=n_blk),
        grid=(b, n_heads // 2),
        in_specs=[
            pl.BlockSpec((None, s, LANES), lambda bi, p: (bi, 0, p)),
            pl.BlockSpec((None, s, LANES), lambda bi, p: (bi, 0, p)),
        ],
        out_specs=pl.BlockSpec((None, s, LANES), lambda bi, p: (bi, 0, p)),
        out_shape=jax.ShapeDtypeStruct((b, s, n_heads // 2 * LANES), BF16),
        compiler_params=_cparams(("parallel", "parallel")),
        name="moba_gate",
    )(q, k)


def _attn_kernel(q_ref, k_ref, vt_ref, qx_ref, kx_ref, o_ref):
    tq = q_ref.shape[0]
    t = ATT_KEYS
    n_sub = tq // ATT_SUB
    n_diag = tq // t
    g = pl.program_id(2)
    lane = lax.broadcasted_iota(jnp.int32, (ATT_SUB, LANES), 1)
    lane_k = lax.broadcasted_iota(jnp.int32, (t, LANES), 1)
    row_v = lax.broadcasted_iota(jnp.int32, (LANES, t), 0)
    row_o = lax.broadcasted_iota(jnp.int32, (LANES, ATT_SUB), 0)
    key = lax.broadcasted_iota(jnp.int32, (t, ATT_SUB), 0)
    qry = lax.broadcasted_iota(jnp.int32, (t, ATT_SUB), 1)
    nt = (((1,), (1,)), ((), ()))
    qas = []
    for hh in range(2):
        in_head = (lane < HEAD_DIM) if hh == 0 else (lane >= HEAD_DIM)
        for r in range(n_sub):
            rs = slice(r * ATT_SUB, (r + 1) * ATT_SUB)
            qas.append(jnp.where(in_head, q_ref[rs, :], qx_ref[rs, :]))

    def q_off(c):
        return (c % n_sub) * ATT_SUB

    def update(state, steps):
        loaded = {}

        def operands(si):
            if si not in loaded:
                j = steps[si][0]
                rows = pl.ds(pl.multiple_of(j * t, t), t)
                k = k_ref[rows, :]
                kx = kx_ref[rows, :]
                vt = vt_ref[:, rows]
                kks = [jnp.where((lane_k < HEAD_DIM) if hh == 0 else (lane_k >= HEAD_DIM), k, kx)
                       for hh in range(2)]
                vhs = [jnp.where((row_v < HEAD_DIM) if hh == 0 else (row_v >= HEAD_DIM), vt, jnp.ones_like(vt))
                       for hh in range(2)]
                loaded[si] = (kks, vhs)
            return loaded[si]

        items = [(si, c) for si, (_, d) in enumerate(steps) for c in range(2 * n_sub)
                 if d is None or q_off(c) + ATT_SUB > d * t]

        def qk(item):
            si, c = item
            return lax.dot_general(operands(si)[0][c // n_sub], qas[c], nt, preferred_element_type=F32)

        scs = {n: qk(items[n]) for n in range(min(ATT_LOOKAHEAD, len(items)))}
        new = list(state)
        for n, (si, c) in enumerate(items):
            if n + ATT_LOOKAHEAD < len(items):
                scs[n + ATT_LOOKAHEAD] = qk(items[n + ATT_LOOKAHEAD])
            d = steps[si][1]
            m_old, acc = new[c]
            sc = scs.pop(n)
            if d is not None and q_off(c) < (d + 1) * t:
                sc = jnp.where(key + d * t <= qry + q_off(c), sc, NEG)
            m_new = jnp.maximum(m_old, jnp.max(sc, axis=0, keepdims=True))
            p = jnp.exp2(sc - m_new).astype(BF16)
            acc = jnp.exp2(m_old - m_new) * acc + jnp.dot(operands(si)[1][c // n_sub], p,
                                                          preferred_element_type=F32)
            new[c] = (m_new, acc)
        return tuple(new)

    init = tuple((jnp.full((1, ATT_SUB), NEG, F32), jnp.zeros((LANES, ATT_SUB), F32)) for _ in range(2 * n_sub))
    per_iter = 2 if n_diag % 2 == 0 else 1
    state = lax.fori_loop(0, g * (n_diag // per_iter),
                          lambda j, st: update(st, [(j * per_iter + u, None) for u in range(per_iter)]), init)
    state = update(state, [(g * n_diag + d, d) for d in range(n_diag)])
    for r in range(n_sub):
        outs = [state[hh * n_sub + r][1] for hh in range(2)]
        outs = [acc / pltpu.roll(acc, HEAD_DIM, 0) for acc in outs]
        o_t = jnp.where(row_o < HEAD_DIM, outs[0], outs[1])
        o_ref[r * ATT_SUB:(r + 1) * ATT_SUB, :] = jnp.transpose(o_t).astype(o_ref.dtype)


def _attention(q, k, vt, qx, kx, n_heads, head_off):
    b, s, _ = q.shape
    t = min(ATT_QUERIES, s)
    po = head_off // 2
    shared_kx = kx.shape[0] == 1 and kx.shape[2] == LANES
    kx_map = (lambda bi, p, i: (0, 0, 0)) if shared_kx else (lambda bi, p, i: (bi, 0, p))
    return pl.pallas_call(
        _attn_kernel,
        grid=(b, n_heads // 2, s // t),
        in_specs=[
            pl.BlockSpec((None, t, LANES), lambda bi, p, i: (bi, i, p + po)),
            pl.BlockSpec((None, s, LANES), lambda bi, p, i: (bi, 0, p + po)),
            pl.BlockSpec((None, LANES, s), lambda bi, p, i: (bi, p + po, 0)),
            pl.BlockSpec((None, t, LANES), lambda bi, p, i: (bi, i, p)),
            pl.BlockSpec((None, s, LANES), kx_map),
        ],
        out_specs=pl.BlockSpec((None, t, LANES), lambda bi, p, i: (bi, i, p)),
        out_shape=jax.ShapeDtypeStruct((b, s, n_heads * HEAD_DIM), BF16),
        compiler_params=_cparams(("parallel", "parallel", "arbitrary")),
        name="attn",
    )(q, k, vt, qx, kx)


def _l0_tail_kernel(om_ref, of_ref, x_ref, wo_ref, wg_ref, wu_ref, wd_ref, mod0_ref, mod1_ref, g_ref,
                    x2_ref, h3_ref, *, d):
    m0 = mod0_ref[...]
    m1 = mod1_ref[...]
    hw = om_ref.shape[1]
    y = (jnp.dot(om_ref[...], wo_ref[0:hw, :], preferred_element_type=F32)
         + jnp.dot(of_ref[...], wo_ref[hw:, :], preferred_element_type=F32))
    x1 = x_ref[...] + m0[:, 2 * d:3 * d] * _rms(y, g_ref[0:1, :])
    h = (_rms(x1, g_ref[1:2, :]) * (1.0 + m0[:, 4 * d:5 * d]) + m0[:, 3 * d:4 * d]).astype(BF16)
    a = jnp.dot(h, wg_ref[...], preferred_element_type=F32)
    u = jnp.dot(h, wu_ref[...], preferred_element_type=F32)
    act = (a * jax.nn.sigmoid(a) * u).astype(BF16)
    y = jnp.dot(act, wd_ref[...], preferred_element_type=F32)
    x2 = x1 + m0[:, 5 * d:6 * d] * _rms(y, g_ref[2:3, :])
    x2_ref[...] = x2
    h3_ref[...] = (_rms(x2, g_ref[3:4, :]) * (1.0 + m1[:, d:2 * d]) + m1[:, 0:d]).astype(BF16)


def _l0_tail(o_m, o_f, x, w_out, w_gate, w_up, w_down, mods, gains, tm):
    b, s, d = x.shape
    d6 = mods.shape[-1]
    tok = lambda bi, i: (bi, i, 0)
    single = dict(pipeline_mode=pl.Buffered(1))
    return pl.pallas_call(
        functools.partial(_l0_tail_kernel, d=d),
        grid=(b, s // tm),
        in_specs=[
            pl.BlockSpec((None, tm, o_m.shape[2]), tok),
            pl.BlockSpec((None, tm, o_f.shape[2]), tok),
            pl.BlockSpec((None, tm, d), tok),
            pl.BlockSpec(w_out.shape, lambda bi, i: (0, 0), **single),
            pl.BlockSpec(w_gate.shape, lambda bi, i: (0, 0), **single),
            pl.BlockSpec(w_up.shape, lambda bi, i: (0, 0), **single),
            pl.BlockSpec(w_down.shape, lambda bi, i: (0, 0), **single),
            pl.BlockSpec((None, None, 1, d6), lambda bi, i: (0, bi, 0, 0)),
            pl.BlockSpec((None, None, 1, d6), lambda bi, i: (1, bi, 0, 0)),
            _const_spec(gains.shape),
        ],
        out_specs=[pl.BlockSpec((None, tm, d), tok), pl.BlockSpec((None, tm, d), tok)],
        out_shape=[jax.ShapeDtypeStruct((b, s, d), F32), jax.ShapeDtypeStruct((b, s, d), BF16)],
        compiler_params=_cparams(("parallel", "parallel"), VMEM_LIMIT),
        name="l0_tail",
    )(o_m, o_f, x, w_out, w_gate, w_up, w_down, mods, mods, gains)


def _lru_kernel(h_ref, x_ref, win_ref, cw_ref, cb_ref, wax_ref, ba_ref, bx_ref, lam_ref, wout_ref,
                mod_ref, g1_ref, g2_ref, wr_ref, br_ref,
                x3_ref, h4_ref, lg_ref, utail_ref, hc_ref, *, d, dr, n_rnn_blocks):
    tm = h_ref.shape[0]
    sub = utail_ref.shape[0]

    @pl.when(pl.program_id(1) == 0)
    def _():
        utail_ref[...] = jnp.zeros_like(utail_ref)
        hc_ref[...] = jnp.zeros_like(hc_ref)

    proj = jnp.dot(h_ref[...], win_ref[...], preferred_element_type=F32)
    gate_branch = proj[:, :dr]
    u = proj[:, dr:]
    row_sub = lax.broadcasted_iota(jnp.int32, (sub, dr), 0)
    n_groups = tm // sub
    taps = [cw_ref[j:j + 1, :] for j in range(CONV_WIDTH)]
    bias = cb_ref[...]
    prev = utail_ref[...]
    rolled_prev = [None] + [pltpu.roll(prev, back, 0) for back in range(1, CONV_WIDTH)]
    conv_groups = []
    for g in range(n_groups):
        ug = u[g * sub:(g + 1) * sub]
        cg = bias + ug * taps[CONV_WIDTH - 1]
        for back in range(1, CONV_WIDTH):
            rolled = pltpu.roll(ug, back, 0)
            cg = cg + jnp.where(row_sub >= back, rolled, rolled_prev[back]) * taps[CONV_WIDTH - 1 - back]
            rolled_prev[back] = rolled
        conv_groups.append(cg)
    conv = jnp.concatenate(conv_groups, axis=0)
    utail_ref[...] = u[tm - sub:tm]

    w = dr // n_rnn_blocks
    rs, is_ = [], []
    for n in range(n_rnn_blocks):
        cbk = conv[:, n * w:(n + 1) * w].astype(BF16)
        ra = jnp.dot(cbk, wax_ref[n], preferred_element_type=F32)
        rs.append(ra[:, :w])
        is_.append(ra[:, w:])
    r = jax.nn.sigmoid(jnp.concatenate(rs, axis=1) + ba_ref[...])
    ig = jax.nn.sigmoid(jnp.concatenate(is_, axis=1) + bx_ref[...])
    nl = -lam_ref[...]
    softplus = jnp.maximum(nl, 0.0) + jnp.log(1.0 + jnp.exp(-jnp.abs(nl)))
    log_a = (-RG_C * r) * softplus
    a = jnp.exp(log_a)
    xin = jnp.exp2(0.5 * jnp.log2(1.0 - a * a)) * (ig * conv)

    carry = hc_ref[...]
    groups = []
    for g in range(n_groups):
        sa, sx = a[g * sub:(g + 1) * sub], xin[g * sub:(g + 1) * sub]
        dist = 1
        while dist < sub:
            keep = row_sub >= dist
            xs = jnp.where(keep, pltpu.roll(sx, dist, 0), 0.0)
            as_ = jnp.where(keep, pltpu.roll(sa, dist, 0), 1.0)
            sx = sx + sa * xs
            sa = sa * as_
            dist *= 2
        hg = sx + sa * carry
        carry = hg[sub - 1:sub]
        groups.append(hg)
    hs = jnp.concatenate(groups, axis=0)
    hc_ref[...] = carry

    y = (jax.nn.gelu(gate_branch, approximate=True) * hs).astype(BF16)
    out = jnp.dot(y, wout_ref[...], preferred_element_type=F32)
    m = mod_ref[...]
    x3 = x_ref[...] + m[:, 2 * d:3 * d] * _rms(out, g1_ref[...])
    x3_ref[...] = x3
    h4 = _rms(x3, g2_ref[...]) * (1.0 + m[:, 4 * d:5 * d]) + m[:, 3 * d:4 * d]
    h4_ref[...] = h4
    h_hi = h4.astype(BF16)
    h_lo = (h4 - h_hi.astype(F32)).astype(BF16)
    lg_ref[...] = jnp.dot(jnp.concatenate([h_hi, h_hi, h_lo], axis=1), wr_ref[...],
                          preferred_element_type=F32) + br_ref[...]


def _lru(h3, x2, w_in, conv_w, conv_b, wax, b_a, b_x, lam, w_out, mods, g_post, g_pre2, w_r, b_r, tm):
    b, s, d = x2.shape
    dr = w_out.shape[0]
    d6 = mods.shape[-1]
    nb = wax.shape[0]
    tok = lambda bi, i: (bi, i, 0)
    return pl.pallas_call(
        functools.partial(_lru_kernel, d=d, dr=dr, n_rnn_blocks=nb),
        grid=(b, s // tm),
        in_specs=[
            pl.BlockSpec((None, tm, d), tok),
            pl.BlockSpec((None, tm, d), tok),
            _const_spec(w_in.shape),
            _const_spec(conv_w.shape),
            _const_spec((1, dr)),
            _const_spec(wax.shape),
            _const_spec((1, dr)),
            _const_spec((1, dr)),
            _const_spec((1, dr)),
            _const_spec(w_out.shape),
            pl.BlockSpec((None, None, 1, d6), lambda bi, i: (1, bi, 0, 0)),
            _const_spec((1, d)),
            _const_spec((1, d)),
            _const_spec(w_r.shape),
            _const_spec((1, LANES)),
        ],
        out_specs=[pl.BlockSpec((None, tm, d), tok), pl.BlockSpec((None, tm, d), tok),
                   pl.BlockSpec((None, tm, LANES), tok)],
        out_shape=[jax.ShapeDtypeStruct((b, s, d), F32), jax.ShapeDtypeStruct((b, s, d), F32),
                   jax.ShapeDtypeStruct((b, s, LANES), F32)],
        scratch_shapes=[pltpu.VMEM((SUBLANES, dr), F32), pltpu.VMEM((1, dr), F32)],
        compiler_params=_cparams(("parallel", "arbitrary"), VMEM_LIMIT),
        name="lru",
    )(h3, x2, w_in, conv_w, conv_b, wax, b_a, b_x, lam, w_out, mods, g_post, g_pre2, w_r, b_r)


def _moe_kernel(bexp_ref, nused_ref, tok_ref, dst_ref, h_hbm, wg_hbm, wu_hbm, wd_hbm, o_hbm,
                xbuf, ybuf, wg_bf, wu_bf, wd_bf, stage_g, stage_u, stage_d, gsem, ssem, wsem,
                *, n_blocks, dummy_base):
    mb = xbuf.shape[1]
    i = pl.program_id(0)
    nused = nused_ref[0]
    slot = lax.rem(i, 2)
    other = 1 - slot

    def gather_copy(blk, buf, r):
        tok = tok_ref[blk * mb + r]
        return pltpu.make_async_copy(h_hbm.at[pl.ds(tok, 1), :], xbuf.at[buf, pl.ds(r, 1), :], gsem.at[buf])

    def scatter_copy(dst, buf, r):
        return pltpu.make_async_copy(ybuf.at[buf, pl.ds(r, 1), :], o_hbm.at[pl.ds(dst, 1), :], ssem.at[buf])

    def wait_gather(buf):
        pltpu.make_async_copy(h_hbm.at[pl.ds(0, mb), :], xbuf.at[buf], gsem.at[buf]).wait()

    def wait_scatter(buf):
        pltpu.make_async_copy(ybuf.at[buf], o_hbm.at[pl.ds(0, mb), :], ssem.at[buf]).wait()

    @pl.when(i == 0)
    def _():
        def body(r, c):
            gather_copy(0, 0, r).start()
            return c
        lax.fori_loop(0, mb, body, 0, unroll=8)
        ybuf[1] = jnp.zeros(ybuf.shape[1:], F32)

    @pl.when((i >= 1) & (i <= nused))
    def _():
        wait_scatter(slot)

    ck = MOE_FF_CHUNK
    n_chunks = wg_bf.shape[1] // ck
    e_cur = bexp_ref[jnp.minimum(i, n_blocks - 1)]
    e_prev = bexp_ref[jnp.maximum(i - 1, 0)]
    new_expert = (i == 0) | (e_cur != e_prev)

    def weight_copies(c, buf):
        cs = pl.ds(c * ck, ck)
        return (pltpu.make_async_copy(wg_hbm.at[e_cur, :, cs], stage_g.at[buf], wsem.at[buf]),
                pltpu.make_async_copy(wu_hbm.at[e_cur, :, cs], stage_u.at[buf], wsem.at[buf]),
                pltpu.make_async_copy(wd_hbm.at[e_cur, cs, :], stage_d.at[buf], wsem.at[buf]))

    def compute(load_weights):
        wait_gather(slot)
        x = xbuf[slot].astype(BF16)
        acc = jnp.zeros((mb, wd_bf.shape[1]), F32)
        per = -(-mb // n_chunks)
        nxt = jnp.minimum(i + 1, n_blocks - 1)
        prv = jnp.maximum(i - 1, 0)
        if load_weights:
            for c in range(min(2, n_chunks)):
                for cp in weight_copies(c, c % 2):
                    cp.start()
        for c in range(n_chunks):
            for r in range(c * per, min((c + 1) * per, mb)):
                gather_copy(nxt, other, r).start()
                dst = jnp.where(i == 0, dummy_base + r, dst_ref[prv * mb + r])
                scatter_copy(dst, other, r).start(priority=1)
            cs = slice(c * ck, (c + 1) * ck)
            if load_weights:
                for cp in weight_copies(c, c % 2):
                    cp.wait()
                wg_bf[:, cs] = stage_g[c % 2].astype(BF16)
                wu_bf[:, cs] = stage_u[c % 2].astype(BF16)
                wd_bf[cs, :] = stage_d[c % 2].astype(BF16)
                if c + 2 < n_chunks:
                    for cp in weight_copies(c + 2, c % 2):
                        cp.start()
            a = jnp.dot(x, wg_bf[:, cs], preferred_element_type=F32)
            u = jnp.dot(x, wu_bf[:, cs], preferred_element_type=F32)
            act = (a * jax.nn.sigmoid(a) * u).astype(BF16)
            acc = acc + jnp.dot(act, wd_bf[cs, :], preferred_element_type=F32)
        ybuf[slot] = acc

    @pl.when((i < nused) & new_expert)
    def _():
        compute(True)

    @pl.when((i < nused) & jnp.logical_not(new_expert))
    def _():
        compute(False)

    @pl.when(i == nused)
    def _():
        wait_gather(slot)

        def body(r, c):
            scatter_copy(dst_ref[(i - 1) * mb + r], other, r).start()
            return c
        lax.fori_loop(0, mb, body, 0, unroll=8)
        wait_scatter(other)


def _moe(block_exp, nused, slot_tok, slot_dst, h4, w_gate, w_up, w_down, n_out_rows):
    n_tok, d = h4.shape
    mb = MOE_BLOCK_ROWS
    cap = slot_tok.shape[0]
    n_blocks = cap // mb
    e, _, dff = w_gate.shape
    ck = MOE_FF_CHUNK
    assert dff % ck == 0
    grid_spec = pltpu.PrefetchScalarGridSpec(
        num_scalar_prefetch=4,
        grid=(n_blocks + 1,),
        in_specs=[pl.BlockSpec(memory_space=pl.ANY)] * 4,
        out_specs=pl.BlockSpec(memory_space=pl.ANY),
        scratch_shapes=[pltpu.VMEM((2, mb, d), F32), pltpu.VMEM((2, mb, d), F32),
                        pltpu.VMEM((d, dff), BF16), pltpu.VMEM((d, dff), BF16), pltpu.VMEM((dff, d), BF16),
                        pltpu.VMEM((2, d, ck), F32), pltpu.VMEM((2, d, ck), F32), pltpu.VMEM((2, ck, d), F32),
                        pltpu.SemaphoreType.DMA((2,)), pltpu.SemaphoreType.DMA((2,)),
                        pltpu.SemaphoreType.DMA((2,))],
    )
    return pl.pallas_call(
        functools.partial(_moe_kernel, n_blocks=n_blocks, dummy_base=n_out_rows),
        grid_spec=grid_spec,
        out_shape=jax.ShapeDtypeStruct((n_out_rows + mb, d), F32),
        compiler_params=_cparams(("arbitrary",), VMEM_LIMIT),
        name="moe",
    )(block_exp, nused, slot_tok, slot_dst, h4, w_gate, w_up, w_down)


def _moe_out_kernel(*refs, d):
    y_refs, (gate_ref, x_ref, mod_ref, g_ref, o_ref) = refs[:TOP_K], refs[TOP_K:]
    g = gate_ref[...]
    y = y_refs[0][...] * g[:, 0:1]
    for kk in range(1, TOP_K):
        y = y + y_refs[kk][...] * g[:, kk:kk + 1]
    m = mod_ref[...]
    o_ref[...] = x_ref[...] + m[:, 5 * d:6 * d] * _rms(y, g_ref[...])


def _moe_out(y_rows, gates, x3, mods, g_post, s, tm):
    n_tok, d = x3.shape
    d6 = mods.shape[-1]
    per_b = s // tm
    nt = n_tok // tm
    y_specs = [pl.BlockSpec((tm, d), functools.partial(lambda i, kk: (kk * nt + i, 0), kk=kk))
               for kk in range(TOP_K)]
    return pl.pallas_call(
        functools.partial(_moe_out_kernel, d=d),
        grid=(nt,),
        in_specs=y_specs + [
            pl.BlockSpec((tm, TOP_K), lambda i: (i, 0)),
            pl.BlockSpec((tm, d), lambda i: (i, 0)),
            pl.BlockSpec((None, None, 1, d6), lambda i: (1, i // per_b, 0, 0)),
            pl.BlockSpec((1, d), lambda i: (0, 0)),
        ],
        out_specs=pl.BlockSpec((tm, d), lambda i: (i, 0)),
        out_shape=jax.ShapeDtypeStruct((n_tok, d), F32),
        compiler_params=_cparams(("parallel",), VMEM_LIMIT),
        name="moe_out",
    )(*([y_rows] * TOP_K), gates, x3, mods, g_post)


def _route(logits, n_experts, mb):
    n_tok = logits.shape[0]
    top_logit, top_idx = lax.top_k(logits, TOP_K)
    gates = jax.nn.softmax(top_logit, axis=-1)
    exp_flat = top_idx.reshape(-1).astype(jnp.int32)
    n_asg = n_tok * TOP_K
    onehot = (exp_flat[:, None] == jnp.arange(n_experts, dtype=jnp.int32)[None, :]).astype(jnp.int32)
    counts = jnp.sum(onehot, axis=0)
    padded = ((counts + mb - 1) // mb) * mb
    pends = jnp.cumsum(padded)
    cap = (-(-n_asg // mb) + n_experts) * mb
    pad_ends = jnp.cumsum(padded - counts)
    pad_exp = jnp.sum(jnp.arange(cap - n_asg, dtype=jnp.int32)[:, None] >= pad_ends[None, :], axis=1)
    keys = jnp.concatenate([exp_flat * 2, pad_exp.astype(jnp.int32) * 2 + 1])
    payload = jnp.concatenate([jnp.arange(n_asg, dtype=jnp.int32), jnp.full((cap - n_asg,), -1, jnp.int32)])
    _, slot_flat = lax.sort((keys, payload), num_keys=1, is_stable=True)
    valid = slot_flat >= 0
    slot_tok = jnp.where(valid, slot_flat // TOP_K, 0)
    slot_dst = jnp.where(valid, (slot_flat % TOP_K) * n_tok + slot_flat // TOP_K,
                         n_asg + jnp.arange(cap, dtype=jnp.int32) % mb)
    n_blocks = cap // mb
    block_starts = jnp.arange(n_blocks, dtype=jnp.int32) * mb
    block_exp = jnp.minimum(jnp.sum(block_starts[:, None] >= pends[None, :], axis=1), n_experts - 1).astype(jnp.int32)
    nused = (pends[-1] // mb).astype(jnp.int32).reshape(1)
    return gates, slot_tok, slot_dst, block_exp, nused, n_asg


def _rope_tables(s):
    half = ROPE_DIMS // 2
    inv_freq = jnp.power(ROPE_THETA, -jnp.arange(half, dtype=F32) / half)
    ang = jnp.arange(s, dtype=F32)[:, None] * inv_freq[None, :]
    cos, sin = jnp.cos(ang), jnp.sin(ang)
    lane = jnp.arange(LANES) % HEAD_DIM
    idx = lane % half
    is_x1 = lane < half
    is_x2 = (lane >= half) & (lane < ROPE_DIMS)
    c = jnp.where((is_x1 | is_x2)[None, :], cos[:, idx], 1.0)
    sa = jnp.where(is_x2[None, :], sin[:, idx], 0.0)
    sb = jnp.where(is_x1[None, :], -sin[:, idx], 0.0)
    return c.astype(F32), sa.astype(F32), sb.astype(F32)


def kernel(x, c, w_ada, b_ada, norm_g, attn_w_in, fox_b_f, attn_w_out, ffn_w_gate, ffn_w_up, ffn_w_down,
           lru_w_in, lru_conv_w, lru_conv_b, lru_w_a, lru_b_a, lru_w_x, lru_b_x, lru_lambda, lru_w_out,
           moe_w_router, moe_b_router, moe_w_gate, moe_w_up, moe_w_down):
    b, s, d = x.shape
    aw = attn_w_out.shape[1]
    n_experts = moe_w_router.shape[2]
    assert s % min(ATT_QUERIES, s) == 0 and s % ATT_KEYS == 0
    assert ATT_QUERIES % ATT_KEYS == 0 and ATT_KEYS % MOBA_BLOCK == 0
    assert ATT_QUERIES % ATT_SUB == 0
    tm = min(512, s)

    c_pad = jnp.zeros((8, d), F32).at[:b].set(c)
    mods = _ada(c_pad, w_ada, b_ada)[:, :b].reshape(w_ada.shape[0], b, 1, 6 * d)

    w_in = attn_w_in[0]
    w_qkv = w_in[:, :3 * aw].astype(BF16)
    w_f = jnp.zeros((d, LANES), F32).at[:, :N_FOX_HEADS].set(w_in[:, 3 * aw:]).astype(BF16)
    b_f = jnp.zeros((1, LANES), F32).at[0, :N_FOX_HEADS].set(fox_b_f[0])
    rope_c, rope_sa, rope_sb = _rope_tables(s)
    q, k, v, qx_f, kx_f = _l0_in(x, mods, norm_g[0, 0][None], w_qkv, w_f, b_f, rope_c, rope_sa, rope_sb,
                                 tm, N_FOX_HEADS)

    qx_m = _moba_gate(q, k, N_MOBA_HEADS)
    blk_of_pos = jnp.arange(s, dtype=jnp.int32) // MOBA_BLOCK
    lane_blk = jnp.arange(LANES, dtype=jnp.int32) % HEAD_DIM
    kx_m = (lane_blk[None, :] == blk_of_pos[:, None]).astype(BF16)[None]
    vt = jnp.swapaxes(v, 1, 2)
    o_m = _attention(q, k, vt, qx_m, kx_m, N_MOBA_HEADS, 0)
    o_f = _attention(q, k, vt, qx_f, kx_f, N_FOX_HEADS, N_MOBA_HEADS)

    gains = jnp.concatenate([norm_g[0, 1:4], norm_g[1, 0:1]], axis=0)
    x2, h3 = _l0_tail(o_m, o_f, x, attn_w_out[0].astype(BF16), ffn_w_gate[0].astype(BF16),
                      ffn_w_up[0].astype(BF16), ffn_w_down[0].astype(BF16), mods, gains, min(256, s))

    wax = jnp.concatenate([lru_w_a[0], lru_w_x[0]], axis=-1).astype(BF16)
    w_r = jnp.zeros((d, LANES), F32).at[:, :n_experts].set(moe_w_router[0])
    w_r_hi = w_r.astype(BF16)
    w_r_lo = (w_r - w_r_hi.astype(F32)).astype(BF16)
    w_r = jnp.concatenate([w_r_hi, w_r_lo, w_r_hi], axis=0)
    b_r = jnp.zeros((1, LANES), F32).at[0, :n_experts].set(moe_b_router[0])
    x3, h4, logits = _lru(h3, x2, lru_w_in[0].astype(BF16), lru_conv_w[0], lru_conv_b[0][None], wax,
                          lru_b_a[0][None], lru_b_x[0][None], lru_lambda[0][None], lru_w_out[0].astype(BF16),
                          mods, norm_g[1, 1][None], norm_g[1, 2][None], w_r, b_r, min(256, s))

    n_tok = b * s
    gates, slot_tok, slot_dst, block_exp, nused, n_rows = _route(
        logits.reshape(n_tok, LANES)[:, :n_experts], n_experts, MOE_BLOCK_ROWS)
    y_rows = _moe(block_exp, nused, slot_tok, slot_dst, h4.reshape(n_tok, d),
                  moe_w_gate.reshape(moe_w_gate.shape[1:]), moe_w_up.reshape(moe_w_up.shape[1:]),
                  moe_w_down.reshape(moe_w_down.shape[1:]), n_rows)
    out = _moe_out(y_rows, gates, x3.reshape(n_tok, d), mods, norm_g[1, 3][None], s, min(256, s))
    return out.reshape(b, s, d)
```

```python
import functools

import jax
import jax.numpy as jnp
from jax import lax
from jax.experimental import pallas as pl
from jax.experimental.pallas import tpu as pltpu

F32 = jnp.float32
BF16 = jnp.bfloat16
HIGHEST = lax.Precision.HIGHEST

NORM_EPS = 1e-6
HEAD_DIM = 64
N_MOBA_HEADS = 8
N_FOX_HEADS = 8
ROPE_DIMS = 16
ROPE_THETA = 500000.0
MOBA_BLOCK = 256
MOBA_TOPK = 3
CONV_WIDTH = 4
RG_C = 8.0
TOP_K = 2

LANES = 128
SUBLANES = 8
FOX_PREP_ROWS = 256
ATT_QUERIES = 2048
ATT_KEYS = 512
ATT_SUB = 256
ATT_LOOKAHEAD = 6
NEG = -1e30
LOG2E = 1.4426950408889634
MOE_BLOCK_ROWS = 512
MOE_FF_CHUNK = 512
FFN_CHUNKS = 2
VMEM_LIMIT = 56 * 1024 * 1024


def _cparams(sem, vmem=None):
    return pltpu.CompilerParams(dimension_semantics=sem, vmem_limit_bytes=vmem)


def _rms(x, g):
    return x * lax.rsqrt(jnp.mean(x * x, axis=-1, keepdims=True) + NORM_EPS) * g


def _const_spec(shape):
    n = len(shape)
    return pl.BlockSpec(shape, lambda *_: (0,) * n)


def _ada_kernel(c_ref, w_ref, b_ref, o_ref):
    c = c_ref[...]
    cond = c * jax.nn.sigmoid(c)
    o_ref[...] = jnp.dot(cond, w_ref[...], preferred_element_type=F32, precision=HIGHEST) + b_ref[...]


def _ada(c_pad, w_ada, b_ada):
    depth, d, d6 = w_ada.shape
    rows = c_pad.shape[0]
    nj = d6 // d
    return pl.pallas_call(
        _ada_kernel,
        grid=(depth, nj),
        in_specs=[
            pl.BlockSpec((rows, d), lambda l, j: (0, 0)),
            pl.BlockSpec((None, d, d), lambda l, j: (l, 0, j)),
            pl.BlockSpec((None, 1, d), lambda l, j: (l, 0, j)),
        ],
        out_specs=pl.BlockSpec((None, rows, d), lambda l, j: (l, 0, j)),
        out_shape=jax.ShapeDtypeStruct((depth, rows, d6), F32),
        compiler_params=_cparams(("parallel", "parallel")),
        name="ada",
    )(c_pad, w_ada, b_ada.reshape(depth, 1, d6))


def _fox_bias_terms(logf, carry_ref, qx_ref, kx_ref, row0, n_heads):
    t = logf.shape[0]
    row = lax.broadcasted_iota(jnp.int32, (t, t), 0)
    col = lax.broadcasted_iota(jnp.int32, (t, t), 1)
    tri = jnp.where(col <= row, 1.0, 0.0).astype(BF16)
    l_hi = logf.astype(BF16)
    r0 = logf - l_hi.astype(F32)
    l_mid = r0.astype(BF16)
    l_lo = (r0 - l_mid.astype(F32)).astype(BF16)
    c3 = jnp.dot(tri, jnp.concatenate([l_hi, l_mid, l_lo], axis=1), preferred_element_type=F32)
    cum = c3[:, 0:LANES] + c3[:, LANES:2 * LANES] + c3[:, 2 * LANES:3 * LANES] + carry_ref[...]
    carry_ref[...] = cum[t - 1:t, :]
    c = cum * LOG2E
    hi = c.astype(BF16).astype(F32)
    r1 = c - hi
    mid = r1.astype(BF16).astype(F32)
    lo = r1 - mid
    lane = lax.broadcasted_iota(jnp.int32, (t, LANES), 1)
    n_terms = 3
    in_terms = (lane % HEAD_DIM) < n_terms
    in_ones = ((lane % HEAD_DIM) >= n_terms) & ((lane % HEAD_DIM) < 2 * n_terms)
    for p in range(n_heads // 2):
        vals = jnp.zeros((t, LANES), F32)
        for h, base in ((2 * p, HEAD_DIM), (2 * p + 1, 0)):
            for n, term in enumerate((hi, mid, lo)):
                shift = (base + n - h) % LANES
                moved = pltpu.roll(term, shift, 1) if shift else term
                vals = jnp.where(lane == base + n, moved, vals)
        qx = jnp.where(in_ones, 1.0, vals)
        kx = jnp.where(in_terms, 1.0, jnp.where(in_ones, -pltpu.roll(vals, n_terms, 1), 0.0))
        qx_ref[row0:row0 + t, p * LANES:(p + 1) * LANES] = qx.astype(BF16)
        kx_ref[row0:row0 + t, p * LANES:(p + 1) * LANES] = kx.astype(BF16)


def _l0_in_kernel(x_ref, mod_ref, g_ref, w_ref, wf_ref, bf_ref, rc_ref, rsa_ref, rsb_ref,
                  q_ref, k_ref, v_ref, qx_ref, kx_ref, carry_ref, *, d, aw, n_rope_chunks, n_fox_heads):
    @pl.when(pl.program_id(1) == 0)
    def _():
        carry_ref[...] = jnp.zeros_like(carry_ref)

    m = mod_ref[...]
    h = _rms(x_ref[...], g_ref[...]) * (1.0 + m[:, d:2 * d]) + m[:, 0:d]
    hb = h.astype(BF16)
    proj = jnp.dot(hb, w_ref[...], preferred_element_type=F32)
    rc, rsa, rsb = rc_ref[...], rsa_ref[...], rsb_ref[...]

    def rope(t):
        return t * rc + pltpu.roll(t, ROPE_DIMS // 2, 1) * rsa + pltpu.roll(t, LANES - ROPE_DIMS // 2, 1) * rsb

    scale = HEAD_DIM ** -0.5 * LOG2E
    for c in range(aw // LANES):
        sl = slice(c * LANES, (c + 1) * LANES)
        qc = proj[:, sl] * scale
        kc = proj[:, aw + c * LANES:aw + (c + 1) * LANES]
        if c < n_rope_chunks:
            qc, kc = rope(qc), rope(kc)
        q_ref[:, sl] = qc.astype(BF16)
        k_ref[:, sl] = kc.astype(BF16)
    v_ref[...] = proj[:, 2 * aw:3 * aw].astype(BF16)
    fl = jnp.dot(hb, wf_ref[...], preferred_element_type=F32) + bf_ref[...]
    logf = jnp.minimum(fl, 0.0) - jnp.log(1.0 + jnp.exp(-jnp.abs(fl)))
    tm = logf.shape[0]
    sub = min(FOX_PREP_ROWS, tm)
    for r0 in range(0, tm, sub):
        _fox_bias_terms(logf[r0:r0 + sub], carry_ref, qx_ref, kx_ref, r0, n_fox_heads)


def _l0_in(x, mods, g_pre, w_qkv, w_f, b_f, rope_c, rope_sa, rope_sb, tm, n_fox_heads):
    b, s, d = x.shape
    aw = w_qkv.shape[1] // 3
    d6 = mods.shape[-1]
    tok = lambda bi, i: (bi, i, 0)
    aux_w = n_fox_heads // 2 * LANES
    aux_spec = pl.BlockSpec((None, tm, aux_w), tok)
    aux_shape = jax.ShapeDtypeStruct((b, s, aux_w), BF16)
    kern = functools.partial(_l0_in_kernel, d=d, aw=aw, n_fox_heads=n_fox_heads,
                             n_rope_chunks=N_MOBA_HEADS * HEAD_DIM // LANES)
    return pl.pallas_call(
        kern,
        grid=(b, s // tm),
        in_specs=[
            pl.BlockSpec((None, tm, d), tok),
            pl.BlockSpec((None, None, 1, d6), lambda bi, i: (0, bi, 0, 0)),
            _const_spec((1, d)),
            _const_spec(w_qkv.shape),
            _const_spec(w_f.shape),
            _const_spec((1, LANES)),
            pl.BlockSpec((tm, LANES), lambda bi, i: (i, 0)),
            pl.BlockSpec((tm, LANES), lambda bi, i: (i, 0)),
            pl.BlockSpec((tm, LANES), lambda bi, i: (i, 0)),
        ],
        out_specs=[
            pl.BlockSpec((None, tm, aw), tok),
            pl.BlockSpec((None, tm, aw), tok),
            pl.BlockSpec((None, tm, aw), tok),
            aux_spec,
            aux_spec,
        ],
        out_shape=[
            jax.ShapeDtypeStruct((b, s, aw), BF16),
            jax.ShapeDtypeStruct((b, s, aw), BF16),
            jax.ShapeDtypeStruct((b, s, aw), BF16),
            aux_shape,
            aux_shape,
        ],
        scratch_shapes=[pltpu.VMEM((1, LANES), F32)],
        compiler_params=_cparams(("parallel", "arbitrary"), VMEM_LIMIT),
        name="l0_in",
    )(x, mods, g_pre, w_qkv, w_f, b_f, rope_c, rope_sa, rope_sb)


def _moba_gate_kernel(q_ref, k_ref, qx_ref, *, n_blk):
    s = k_ref.shape[0]
    t = MOBA_BLOCK
    rowi = lax.broadcasted_iota(jnp.int32, (n_blk, s), 0)
    cs = lax.broadcasted_iota(jnp.int32, (n_blk, s), 1)
    own = (cs >= rowi * t) & (cs < (rowi + 1) * t)
    km = jnp.dot(jnp.where(own, 1.0 / t, 0.0).astype(BF16), k_ref[...], preferred_element_type=F32)
    lane_k = lax.broadcasted_iota(jnp.int32, (n_blk, LANES), 1)
    km2 = jnp.concatenate([jnp.where(lane_k < HEAD_DIM, km, 0.0), jnp.where(lane_k >= HEAD_DIM, km, 0.0)], axis=0)
    hi = km2.astype(BF16)
    r1 = km2 - hi.astype(F32)
    mid = r1.astype(BF16)
    lo = (r1 - mid.astype(F32)).astype(BF16)
    g3 = lax.dot_general(jnp.concatenate([hi, mid, lo], axis=0), q_ref[...], (((1,), (1,)), ((), ())),
                         preferred_element_type=F32)
    g = g3[0:2 * n_blk] + g3[2 * n_blk:4 * n_blk] + g3[4 * n_blk:6 * n_blk]
    fully_past = (rowi + 1) * t <= cs
    biases = []
    for hh in range(2):
        gv = jnp.where(fully_past, g[hh * n_blk:(hh + 1) * n_blk], -jnp.inf)
        keep = own
        for _ in range(MOBA_TOPK):
            mx = jnp.max(gv, axis=0, keepdims=True)
            cand = jnp.where((gv == mx) & (mx > -jnp.inf), rowi, n_blk)
            pick = rowi == jnp.min(cand, axis=0, keepdims=True)
            keep = keep | pick
            gv = jnp.where(pick, -jnp.inf, gv)
        biases.append(jnp.where(keep, 0.0, NEG))
    fill = jnp.zeros((HEAD_DIM - n_blk, s), F32)
    qx_ref[...] = jnp.transpose(jnp.concatenate([biases[1], fill, biases[0], fill], axis=0)).astype(BF16)


def _moba_gate(q, k, n_heads):
    b, s, _ = q.shape
    n_blk = s // MOBA_BLOCK
    assert n_blk <= HEAD_DIM
    return pl.pallas_call(
        functools.partial(_moba_gate_kernel, n_blk=n_blk),
        grid=(b, n_heads // 2),
        in_specs=[
            pl.BlockSpec((None, s, LANES), lambda bi, p: (bi, 0, p)),
            pl.BlockSpec((None, s, LANES), lambda bi, p: (bi, 0, p)),
        ],
        out_specs=pl.BlockSpec((None, s, LANES), lambda bi, p: (bi, 0, p)),
        out_shape=jax.ShapeDtypeStruct((b, s, n_heads // 2 * LANES), BF16),
        compiler_params=_cparams(("parallel", "parallel")),
        name="moba_gate",
    )(q, k)


def _attn_kernel(q_ref, k_ref, vt_ref, qx_ref, kx_ref, o_ref):
    tq = q_ref.shape[0]
    t = ATT_KEYS
    n_sub = tq // ATT_SUB
    n_diag = tq // t
    g = pl.program_id(2)
    lane = lax.broadcasted_iota(jnp.int32, (ATT_SUB, LANES), 1)
    lane_k = lax.broadcasted_iota(jnp.int32, (t, LANES), 1)
    row_v = lax.broadcasted_iota(jnp.int32, (LANES, t), 0)
    row_o = lax.broadcasted_iota(jnp.int32, (LANES, ATT_SUB), 0)
    key = lax.broadcasted_iota(jnp.int32, (t, ATT_SUB), 0)
    qry = lax.broadcasted_iota(jnp.int32, (t, ATT_SUB), 1)
    nt = (((1,), (1,)), ((), ()))
    qas = []
    for hh in range(2):
        in_head = (lane < HEAD_DIM) if hh == 0 else (lane >= HEAD_DIM)
        for r in range(n_sub):
            rs = slice(r * ATT_SUB, (r + 1) * ATT_SUB)
            qas.append(jnp.where(in_head, q_ref[rs, :], qx_ref[rs, :]))

    def q_off(c):
        return (c % n_sub) * ATT_SUB

    def update(state, steps):
        loaded = {}

        def operands(si):
            if si not in loaded:
                j = steps[si][0]
                rows = pl.ds(pl.multiple_of(j * t, t), t)
                k = k_ref[rows, :]
                kx = kx_ref[rows, :]
                vt = vt_ref[:, rows]
                kks = [jnp.where((lane_k < HEAD_DIM) if hh == 0 else (lane_k >= HEAD_DIM), k, kx)
                       for hh in range(2)]
                vhs = [jnp.where((row_v < HEAD_DIM) if hh == 0 else (row_v >= HEAD_DIM), vt, jnp.ones_like(vt))
                       for hh in range(2)]
                loaded[si] = (kks, vhs)
            return loaded[si]

        items = [(si, c) for si, (_, d) in enumerate(steps) for c in range(2 * n_sub)
                 if d is None or q_off(c) + ATT_SUB > d * t]

        def qk(item):
            si, c = item
            return lax.dot_general(operands(si)[0][c // n_sub], qas[c], nt, preferred_element_type=F32)

        scs = {n: qk(items[n]) for n in range(min(ATT_LOOKAHEAD, len(items)))}
        new = list(state)
        for n, (si, c) in enumerate(items):
            if n + ATT_LOOKAHEAD < len(items):
                scs[n + ATT_LOOKAHEAD] = qk(items[n + ATT_LOOKAHEAD])
            d = steps[si][1]
            m_old, acc = new[c]
            sc = scs.pop(n)
            if d is not None and q_off(c) < (d + 1) * t:
                sc = jnp.where(key + d * t <= qry + q_off(c), sc, NEG)
            m_new = jnp.maximum(m_old, jnp.max(sc, axis=0, keepdims=True))
            p = jnp.exp2(sc - m_new).astype(BF16)
            acc = jnp.exp2(m_old - m_new) * acc + jnp.dot(operands(si)[1][c // n_sub], p,
                                                          preferred_element_type=F32)
            new[c] = (m_new, acc)
        return tuple(new)

    init = tuple((jnp.full((1, ATT_SUB), NEG, F32), jnp.zeros((LANES, ATT_SUB), F32)) for _ in range(2 * n_sub))
    per_iter = 2 if n_diag % 2 == 0 else 1
    state = lax.fori_loop(0, g * (n_diag // per_iter),
                          lambda j, st: update(st, [(j * per_iter + u, None) for u in range(per_iter)]), init)
    state = update(state, [(g * n_diag + d, d) for d in range(n_diag)])
    for r in range(n_sub):
        outs = [state[hh * n_sub + r][1] for hh in range(2)]
        outs = [acc / pltpu.roll(acc, HEAD_DIM, 0) for acc in outs]
        o_t = jnp.where(row_o < HEAD_DIM, outs[0], outs[1])
        o_ref[r * ATT_SUB:(r + 1) * ATT_SUB, :] = jnp.transpose(o_t).astype(o_ref.dtype)


def _attention(q, k, vt, qx, kx, n_heads, head_off):
    b, s, _ = q.shape
    t = min(ATT_QUERIES, s)
    po = head_off // 2
    shared_kx = kx.shape[0] == 1 and kx.shape[2] == LANES
    kx_map = (lambda bi, p, i: (0, 0, 0)) if shared_kx else (lambda bi, p, i: (bi, 0, p))
    return pl.pallas_call(
        _attn_kernel,
        grid=(b, n_heads // 2, s // t),
        in_specs=[
            pl.BlockSpec((None, t, LANES), lambda bi, p, i: (bi, i, p + po)),
            pl.BlockSpec((None, s, LANES), lambda bi, p, i: (bi, 0, p + po)),
            pl.BlockSpec((None, LANES, s), lambda bi, p, i: (bi, p + po, 0)),
            pl.BlockSpec((None, t, LANES), lambda bi, p, i: (bi, i, p)),
            pl.BlockSpec((None, s, LANES), kx_map),
        ],
        out_specs=pl.BlockSpec((None, t, LANES), lambda bi, p, i: (bi, i, p)),
        out_shape=jax.ShapeDtypeStruct((b, s, n_heads * HEAD_DIM), BF16),
        compiler_params=_cparams(("parallel", "parallel", "arbitrary")),
        name="attn",
    )(q, k, vt, qx, kx)


def _l0_tail_kernel(om_ref, of_ref, x_ref, wo_ref, wg_ref, wu_ref, wd_ref, mod0_ref, mod1_ref, g_ref,
                    x2_ref, h3_ref, *, d):
    m0 = mod0_ref[...]
    m1 = mod1_ref[...]
    hw = om_ref.shape[1]
    y = (jnp.dot(om_ref[...], wo_ref[0:hw, :], preferred_element_type=F32)
         + jnp.dot(of_ref[...], wo_ref[hw:, :], preferred_element_type=F32))
    x1 = x_ref[...] + m0[:, 2 * d:3 * d] * _rms(y, g_ref[0:1, :])
    h = (_rms(x1, g_ref[1:2, :]) * (1.0 + m0[:, 4 * d:5 * d]) + m0[:, 3 * d:4 * d]).astype(BF16)
    dff = wg_ref.shape[1]
    ck = dff // FFN_CHUNKS
    y = None
    for c in range(FFN_CHUNKS):
        cs = slice(c * ck, (c + 1) * ck)
        a = jnp.dot(h, wg_ref[:, cs], preferred_element_type=F32)
        u = jnp.dot(h, wu_ref[:, cs], preferred_element_type=F32)
        act = (a * jax.nn.sigmoid(a) * u).astype(BF16)
        part = jnp.dot(act, wd_ref[cs, :], preferred_element_type=F32)
        y = part if y is None else y + part
    x2 = x1 + m0[:, 5 * d:6 * d] * _rms(y, g_ref[2:3, :])
    x2_ref[...] = x2
    h3_ref[...] = (_rms(x2, g_ref[3:4, :]) * (1.0 + m1[:, d:2 * d]) + m1[:, 0:d]).astype(BF16)


def _l0_tail(o_m, o_f, x, w_out, w_gate, w_up, w_down, mods, gains, tm):
    b, s, d = x.shape
    d6 = mods.shape[-1]
    tok = lambda bi, i: (bi, i, 0)
    single = dict(pipeline_mode=pl.Buffered(1))
    return pl.pallas_call(
        functools.partial(_l0_tail_kernel, d=d),
        grid=(b, s // tm),
        in_specs=[
            pl.BlockSpec((None, tm, o_m.shape[2]), tok),
            pl.BlockSpec((None, tm, o_f.shape[2]), tok),
            pl.BlockSpec((None, tm, d), tok),
            pl.BlockSpec(w_out.shape, lambda bi, i: (0, 0), **single),
            pl.BlockSpec(w_gate.shape, lambda bi, i: (0, 0), **single),
            pl.BlockSpec(w_up.shape, lambda bi, i: (0, 0), **single),
            pl.BlockSpec(w_down.shape, lambda bi, i: (0, 0), **single),
            pl.BlockSpec((None, None, 1, d6), lambda bi, i: (0, bi, 0, 0)),
            pl.BlockSpec((None, None, 1, d6), lambda bi, i: (1, bi, 0, 0)),
            _const_spec(gains.shape),
        ],
        out_specs=[pl.BlockSpec((None, tm, d), tok), pl.BlockSpec((None, tm, d), tok)],
        out_shape=[jax.ShapeDtypeStruct((b, s, d), F32), jax.ShapeDtypeStruct((b, s, d), BF16)],
        compiler_params=_cparams(("parallel", "parallel"), VMEM_LIMIT),
        name="l0_tail",
    )(o_m, o_f, x, w_out, w_gate, w_up, w_down, mods, mods, gains)


def _lru_kernel(h_ref, x_ref, win_ref, cw_ref, cb_ref, wax_ref, ba_ref, bx_ref, lam_ref, wout_ref,
                mod_ref, g1_ref, g2_ref, wr_ref, br_ref,
                x3_ref, h4_ref, lg_ref, utail_ref, hc_ref, *, d, dr, n_rnn_blocks):
    tm = h_ref.shape[0]
    sub = utail_ref.shape[0]

    @pl.when(pl.program_id(1) == 0)
    def _():
        utail_ref[...] = jnp.zeros_like(utail_ref)
        hc_ref[...] = jnp.zeros_like(hc_ref)

    proj = jnp.dot(h_ref[...], win_ref[...], preferred_element_type=F32)
    gate_branch = proj[:, :dr]
    u = proj[:, dr:]
    row_sub = lax.broadcasted_iota(jnp.int32, (sub, dr), 0)
    n_groups = tm // sub
    taps = [cw_ref[j:j + 1, :] for j in range(CONV_WIDTH)]
    bias = cb_ref[...]
    prev = utail_ref[...]
    rolled_prev = [None] + [pltpu.roll(prev, back, 0) for back in range(1, CONV_WIDTH)]
    conv_groups = []
    for g in range(n_groups):
        ug = u[g * sub:(g + 1) * sub]
        cg = bias + ug * taps[CONV_WIDTH - 1]
        for back in range(1, CONV_WIDTH):
            rolled = pltpu.roll(ug, back, 0)
            cg = cg + jnp.where(row_sub >= back, rolled, rolled_prev[back]) * taps[CONV_WIDTH - 1 - back]
            rolled_prev[back] = rolled
        conv_groups.append(cg)
    conv = jnp.concatenate(conv_groups, axis=0)
    utail_ref[...] = u[tm - sub:tm]

    w = dr // n_rnn_blocks
    rs, is_ = [], []
    for n in range(n_rnn_blocks):
        cbk = conv[:, n * w:(n + 1) * w].astype(BF16)
        ra = jnp.dot(cbk, wax_ref[n], preferred_element_type=F32)
        rs.append(ra[:, :w])
        is_.append(ra[:, w:])
    r = jax.nn.sigmoid(jnp.concatenate(rs, axis=1) + ba_ref[...])
    ig = jax.nn.sigmoid(jnp.concatenate(is_, axis=1) + bx_ref[...])
    nl = -lam_ref[...]
    softplus = jnp.maximum(nl, 0.0) + jnp.log(1.0 + jnp.exp(-jnp.abs(nl)))
    log_a = (-RG_C * r) * softplus
    a = jnp.exp(log_a)
    xin = jnp.exp2(0.5 * jnp.log2(1.0 - a * a)) * (ig * conv)

    carry = hc_ref[...]
    groups = []
    for g in range(n_groups):
        sa, sx = a[g * sub:(g + 1) * sub], xin[g * sub:(g + 1) * sub]
        dist = 1
        while dist < sub:
            keep = row_sub >= dist
            xs = jnp.where(keep, pltpu.roll(sx, dist, 0), 0.0)
            as_ = jnp.where(keep, pltpu.roll(sa, dist, 0), 1.0)
            sx = sx + sa * xs
            sa = sa * as_
            dist *= 2
        hg = sx + sa * carry
        carry = hg[sub - 1:sub]
        groups.append(hg)
    hs = jnp.concatenate(groups, axis=0)
    hc_ref[...] = carry

    y = (jax.nn.gelu(gate_branch, approximate=True) * hs).astype(BF16)
    out = jnp.dot(y, wout_ref[...], preferred_element_type=F32)
    m = mod_ref[...]
    x3 = x_ref[...] + m[:, 2 * d:3 * d] * _rms(out, g1_ref[...])
    x3_ref[...] = x3
    h4 = _rms(x3, g2_ref[...]) * (1.0 + m[:, 4 * d:5 * d]) + m[:, 3 * d:4 * d]
    h4_ref[...] = h4
    h_hi = h4.astype(BF16)
    h_lo = (h4 - h_hi.astype(F32)).astype(BF16)
    lg_ref[...] = jnp.dot(jnp.concatenate([h_hi, h_hi, h_lo], axis=1), wr_ref[...],
                          preferred_element_type=F32) + br_ref[...]


def _lru(h3, x2, w_in, conv_w, conv_b, wax, b_a, b_x, lam, w_out, mods, g_post, g_pre2, w_r, b_r, tm):
    b, s, d = x2.shape
    dr = w_out.shape[0]
    d6 = mods.shape[-1]
    nb = wax.shape[0]
    tok = lambda bi, i: (bi, i, 0)
    return pl.pallas_call(
        functools.partial(_lru_kernel, d=d, dr=dr, n_rnn_blocks=nb),
        grid=(b, s // tm),
        in_specs=[
            pl.BlockSpec((None, tm, d), tok),
            pl.BlockSpec((None, tm, d), tok),
            _const_spec(w_in.shape),
            _const_spec(conv_w.shape),
            _const_spec((1, dr)),
            _const_spec(wax.shape),
            _const_spec((1, dr)),
            _const_spec((1, dr)),
            _const_spec((1, dr)),
            _const_spec(w_out.shape),
            pl.BlockSpec((None, None, 1, d6), lambda bi, i: (1, bi, 0, 0)),
            _const_spec((1, d)),
            _const_spec((1, d)),
            _const_spec(w_r.shape),
            _const_spec((1, LANES)),
        ],
        out_specs=[pl.BlockSpec((None, tm, d), tok), pl.BlockSpec((None, tm, d), tok),
                   pl.BlockSpec((None, tm, LANES), tok)],
        out_shape=[jax.ShapeDtypeStruct((b, s, d), F32), jax.ShapeDtypeStruct((b, s, d), F32),
                   jax.ShapeDtypeStruct((b, s, LANES), F32)],
        scratch_shapes=[pltpu.VMEM((SUBLANES, dr), F32), pltpu.VMEM((1, dr), F32)],
        compiler_params=_cparams(("parallel", "arbitrary"), VMEM_LIMIT),
        name="lru",
    )(h3, x2, w_in, conv_w, conv_b, wax, b_a, b_x, lam, w_out, mods, g_post, g_pre2, w_r, b_r)


def _moe_kernel(bexp_ref, nused_ref, tok_ref, dst_ref, h_hbm, wg_hbm, wu_hbm, wd_hbm, o_hbm,
                xbuf, ybuf, wg_bf, wu_bf, wd_bf, stage_g, stage_u, stage_d, gsem, ssem, wsem,
                *, n_blocks, dummy_base):
    mb = xbuf.shape[1]
    i = pl.program_id(0)
    nused = nused_ref[0]
    slot = lax.rem(i, 2)
    other = 1 - slot

    def gather_copy(blk, buf, r):
        tok = tok_ref[blk * mb + r]
        return pltpu.make_async_copy(h_hbm.at[pl.ds(tok, 1), :], xbuf.at[buf, pl.ds(r, 1), :], gsem.at[buf])

    def scatter_copy(dst, buf, r):
        return pltpu.make_async_copy(ybuf.at[buf, pl.ds(r, 1), :], o_hbm.at[pl.ds(dst, 1), :], ssem.at[buf])

    def wait_gather(buf):
        pltpu.make_async_copy(h_hbm.at[pl.ds(0, mb), :], xbuf.at[buf], gsem.at[buf]).wait()

    def wait_scatter(buf):
        pltpu.make_async_copy(ybuf.at[buf], o_hbm.at[pl.ds(0, mb), :], ssem.at[buf]).wait()

    @pl.when(i == 0)
    def _():
        def body(r, c):
            gather_copy(0, 0, r).start()
            return c
        lax.fori_loop(0, mb, body, 0, unroll=8)
        ybuf[1] = jnp.zeros(ybuf.shape[1:], F32)

    @pl.when((i >= 1) & (i <= nused))
    def _():
        wait_scatter(slot)

    ck = MOE_FF_CHUNK
    n_chunks = wg_bf.shape[1] // ck
    e_cur = bexp_ref[jnp.minimum(i, n_blocks - 1)]
    e_prev = bexp_ref[jnp.maximum(i - 1, 0)]
    new_expert = (i == 0) | (e_cur != e_prev)

    def weight_copies(c, buf):
        cs = pl.ds(c * ck, ck)
        return (pltpu.make_async_copy(wg_hbm.at[e_cur, :, cs], stage_g.at[buf], wsem.at[buf]),
                pltpu.make_async_copy(wu_hbm.at[e_cur, :, cs], stage_u.at[buf], wsem.at[buf]),
                pltpu.make_async_copy(wd_hbm.at[e_cur, cs, :], stage_d.at[buf], wsem.at[buf]))

    def compute(load_weights):
        wait_gather(slot)
        x = xbuf[slot].astype(BF16)
        acc = jnp.zeros((mb, wd_bf.shape[1]), F32)
        per = -(-mb // n_chunks)
        nxt = jnp.minimum(i + 1, n_blocks - 1)
        prv = jnp.maximum(i - 1, 0)
        if load_weights:
            for c in range(min(2, n_chunks)):
                for cp in weight_copies(c, c % 2):
                    cp.start()
        for c in range(n_chunks):
            for r in range(c * per, min((c + 1) * per, mb)):
                gather_copy(nxt, other, r).start()
                dst = jnp.where(i == 0, dummy_base + r, dst_ref[prv * mb + r])
                scatter_copy(dst, other, r).start(priority=1)
            cs = slice(c * ck, (c + 1) * ck)
            if load_weights:
                for cp in weight_copies(c, c % 2):
                    cp.wait()
                wg_bf[:, cs] = stage_g[c % 2].astype(BF16)
                wu_bf[:, cs] = stage_u[c % 2].astype(BF16)
                wd_bf[cs, :] = stage_d[c % 2].astype(BF16)
                if c + 2 < n_chunks:
                    for cp in weight_copies(c + 2, c % 2):
                        cp.start()
            a = jnp.dot(x, wg_bf[:, cs], preferred_element_type=F32)
            u = jnp.dot(x, wu_bf[:, cs], preferred_element_type=F32)
            act = (a * jax.nn.sigmoid(a) * u).astype(BF16)
            acc = acc + jnp.dot(act, wd_bf[cs, :], preferred_element_type=F32)
        ybuf[slot] = acc

    @pl.when((i < nused) & new_expert)
    def _():
        compute(True)

    @pl.when((i < nused) & jnp.logical_not(new_expert))
    def _():
        compute(False)

    @pl.when(i == nused)
    def _():
        wait_gather(slot)

        def body(r, c):
            scatter_copy(dst_ref[(i - 1) * mb + r], other, r).start()
            return c
        lax.fori_loop(0, mb, body, 0, unroll=8)
        wait_scatter(other)


def _moe(block_exp, nused, slot_tok, slot_dst, h4, w_gate, w_up, w_down, n_out_rows):
    n_tok, d = h4.shape
    mb = MOE_BLOCK_ROWS
    cap = slot_tok.shape[0]
    n_blocks = cap // mb
    e, _, dff = w_gate.shape
    ck = MOE_FF_CHUNK
    assert dff % ck == 0
    grid_spec = pltpu.PrefetchScalarGridSpec(
        num_scalar_prefetch=4,
        grid=(n_blocks + 1,),
        in_specs=[pl.BlockSpec(memory_space=pl.ANY)] * 4,
        out_specs=pl.BlockSpec(memory_space=pl.ANY),
        scratch_shapes=[pltpu.VMEM((2, mb, d), F32), pltpu.VMEM((2, mb, d), F32),
                        pltpu.VMEM((d, dff), BF16), pltpu.VMEM((d, dff), BF16), pltpu.VMEM((dff, d), BF16),
                        pltpu.VMEM((2, d, ck), F32), pltpu.VMEM((2, d, ck), F32), pltpu.VMEM((2, ck, d), F32),
                        pltpu.SemaphoreType.DMA((2,)), pltpu.SemaphoreType.DMA((2,)),
                        pltpu.SemaphoreType.DMA((2,))],
    )
    return pl.pallas_call(
        functools.partial(_moe_kernel, n_blocks=n_blocks, dummy_base=n_out_rows),
        grid_spec=grid_spec,
        out_shape=jax.ShapeDtypeStruct((n_out_rows + mb, d), F32),
        compiler_params=_cparams(("arbitrary",), VMEM_LIMIT),
        name="moe",
    )(block_exp, nused, slot_tok, slot_dst, h4, w_gate, w_up, w_down)


def _moe_out_kernel(*refs, d):
    y_refs, (gate_ref, x_ref, mod_ref, g_ref, o_ref) = refs[:TOP_K], refs[TOP_K:]
    g = gate_ref[...]
    y = y_refs[0][...] * g[:, 0:1]
    for kk in range(1, TOP_K):
        y = y + y_refs[kk][...] * g[:, kk:kk + 1]
    m = mod_ref[...]
    o_ref[...] = x_ref[...] + m[:, 5 * d:6 * d] * _rms(y, g_ref[...])


def _moe_out(y_rows, gates, x3, mods, g_post, s, tm):
    n_tok, d = x3.shape
    d6 = mods.shape[-1]
    per_b = s // tm
    nt = n_tok // tm
    y_specs = [pl.BlockSpec((tm, d), functools.partial(lambda i, kk: (kk * nt + i, 0), kk=kk))
               for kk in range(TOP_K)]
    return pl.pallas_call(
        functools.partial(_moe_out_kernel, d=d),
        grid=(nt,),
        in_specs=y_specs + [
            pl.BlockSpec((tm, TOP_K), lambda i: (i, 0)),
            pl.BlockSpec((tm, d), lambda i: (i, 0)),
            pl.BlockSpec((None, None, 1, d6), lambda i: (1, i // per_b, 0, 0)),
            pl.BlockSpec((1, d), lambda i: (0, 0)),
        ],
        out_specs=pl.BlockSpec((tm, d), lambda i: (i, 0)),
        out_shape=jax.ShapeDtypeStruct((n_tok, d), F32),
        compiler_params=_cparams(("parallel",), VMEM_LIMIT),
        name="moe_out",
    )(*([y_rows] * TOP_K), gates, x3, mods, g_post)


def _route(logits, n_experts, mb):
    n_tok = logits.shape[0]
    top_logit, top_idx = lax.top_k(logits, TOP_K)
    gates = jax.nn.softmax(top_logit, axis=-1)
    exp_flat = top_idx.reshape(-1).astype(jnp.int32)
    n_asg = n_tok * TOP_K
    onehot = (exp_flat[:, None] == jnp.arange(n_experts, dtype=jnp.int32)[None, :]).astype(jnp.int32)
    counts = jnp.sum(onehot, axis=0)
    padded = ((counts + mb - 1) // mb) * mb
    pends = jnp.cumsum(padded)
    cap = (-(-n_asg // mb) + n_experts) * mb
    pad_ends = jnp.cumsum(padded - counts)
    pad_exp = jnp.sum(jnp.arange(cap - n_asg, dtype=jnp.int32)[:, None] >= pad_ends[None, :], axis=1)
    keys = jnp.concatenate([exp_flat * 2, pad_exp.astype(jnp.int32) * 2 + 1])
    payload = jnp.concatenate([jnp.arange(n_asg, dtype=jnp.int32), jnp.full((cap - n_asg,), -1, jnp.int32)])
    _, slot_flat = lax.sort((keys, payload), num_keys=1, is_stable=True)
    valid = slot_flat >= 0
    slot_tok = jnp.where(valid, slot_flat // TOP_K, 0)
    slot_dst = jnp.where(valid, (slot_flat % TOP_K) * n_tok + slot_flat // TOP_K,
                         n_asg + jnp.arange(cap, dtype=jnp.int32) % mb)
    n_blocks = cap // mb
    block_starts = jnp.arange(n_blocks, dtype=jnp.int32) * mb
    block_exp = jnp.minimum(jnp.sum(block_starts[:, None] >= pends[None, :], axis=1), n_experts - 1).astype(jnp.int32)
    nused = (pends[-1] // mb).astype(jnp.int32).reshape(1)
    return gates, slot_tok, slot_dst, block_exp, nused, n_asg


def _rope_tables(s):
    half = ROPE_DIMS // 2
    inv_freq = jnp.power(ROPE_THETA, -jnp.arange(half, dtype=F32) / half)
    ang = jnp.arange(s, dtype=F32)[:, None] * inv_freq[None, :]
    cos, sin = jnp.cos(ang), jnp.sin(ang)
    lane = jnp.arange(LANES) % HEAD_DIM
    idx = lane % half
    is_x1 = lane < half
    is_x2 = (lane >= half) & (lane < ROPE_DIMS)
    c = jnp.where((is_x1 | is_x2)[None, :], cos[:, idx], 1.0)
    sa = jnp.where(is_x2[None, :], sin[:, idx], 0.0)
    sb = jnp.where(is_x1[None, :], -sin[:, idx], 0.0)
    return c.astype(F32), sa.astype(F32), sb.astype(F32)


def kernel(x, c, w_ada, b_ada, norm_g, attn_w_in, fox_b_f, attn_w_out, ffn_w_gate, ffn_w_up, ffn_w_down,
           lru_w_in, lru_conv_w, lru_conv_b, lru_w_a, lru_b_a, lru_w_x, lru_b_x, lru_lambda, lru_w_out,
           moe_w_router, moe_b_router, moe_w_gate, moe_w_up, moe_w_down):
    b, s, d = x.shape
    aw = attn_w_out.shape[1]
    n_experts = moe_w_router.shape[2]
    assert s % min(ATT_QUERIES, s) == 0 and s % ATT_KEYS == 0
    assert ATT_QUERIES % ATT_KEYS == 0 and ATT_KEYS % MOBA_BLOCK == 0
    assert ATT_QUERIES % ATT_SUB == 0
    tm = min(512, s)

    c_pad = jnp.zeros((8, d), F32).at[:b].set(c)
    mods = _ada(c_pad, w_ada, b_ada)[:, :b].reshape(w_ada.shape[0], b, 1, 6 * d)

    w_in = attn_w_in[0]
    w_qkv = w_in[:, :3 * aw].astype(BF16)
    w_f = jnp.zeros((d, LANES), F32).at[:, :N_FOX_HEADS].set(w_in[:, 3 * aw:]).astype(BF16)
    b_f = jnp.zeros((1, LANES), F32).at[0, :N_FOX_HEADS].set(fox_b_f[0])
    rope_c, rope_sa, rope_sb = _rope_tables(s)
    q, k, v, qx_f, kx_f = _l0_in(x, mods, norm_g[0, 0][None], w_qkv, w_f, b_f, rope_c, rope_sa, rope_sb,
                                 tm, N_FOX_HEADS)

    qx_m = _moba_gate(q, k, N_MOBA_HEADS)
    blk_of_pos = jnp.arange(s, dtype=jnp.int32) // MOBA_BLOCK
    lane_blk = jnp.arange(LANES, dtype=jnp.int32) % HEAD_DIM
    kx_m = (lane_blk[None, :] == blk_of_pos[:, None]).astype(BF16)[None]
    vt = jnp.swapaxes(v, 1, 2)
    o_m = _attention(q, k, vt, qx_m, kx_m, N_MOBA_HEADS, 0)
    o_f = _attention(q, k, vt, qx_f, kx_f, N_FOX_HEADS, N_MOBA_HEADS)

    gains = jnp.concatenate([norm_g[0, 1:4], norm_g[1, 0:1]], axis=0)
    x2, h3 = _l0_tail(o_m, o_f, x, attn_w_out[0].astype(BF16), ffn_w_gate[0].astype(BF16),
                      ffn_w_up[0].astype(BF16), ffn_w_down[0].astype(BF16), mods, gains, tm)

    wax = jnp.concatenate([lru_w_a[0], lru_w_x[0]], axis=-1).astype(BF16)
    w_r = jnp.zeros((d, LANES), F32).at[:, :n_experts].set(moe_w_router[0])
    w_r_hi = w_r.astype(BF16)
    w_r_lo = (w_r - w_r_hi.astype(F32)).astype(BF16)
    w_r = jnp.concatenate([w_r_hi, w_r_lo, w_r_hi], axis=0)
    b_r = jnp.zeros((1, LANES), F32).at[0, :n_experts].set(moe_b_router[0])
    x3, h4, logits = _lru(h3, x2, lru_w_in[0].astype(BF16), lru_conv_w[0], lru_conv_b[0][None], wax,
                          lru_b_a[0][None], lru_b_x[0][None], lru_lambda[0][None], lru_w_out[0].astype(BF16),
                          mods, norm_g[1, 1][None], norm_g[1, 2][None], w_r, b_r, min(256, s))

    n_tok = b * s
    gates, slot_tok, slot_dst, block_exp, nused, n_rows = _route(
        logits.reshape(n_tok, LANES)[:, :n_experts], n_experts, MOE_BLOCK_ROWS)
    y_rows = _moe(block_exp, nused, slot_tok, slot_dst, h4.reshape(n_tok, d),
                  moe_w_gate.reshape(moe_w_gate.shape[1:]), moe_w_up.reshape(moe_w_up.shape[1:]),
                  moe_w_down.reshape(moe_w_down.shape[1:]), n_rows)
    out = _moe_out(y_rows, gates, x3.reshape(n_tok, d), mods, norm_g[1, 3][None], s, min(256, s))
    return out.reshape(b, s, d)
```
